```python
import math
import jax
import jax.numpy as jnp
from jax import lax
import numpy as np

D_MODEL = 1024
BATCH = 2
SEQ = 8192
DEPTH = 2
DEC_BATCH = 32
DEC_SEQ = 4
PAST_LEN = 8192
PAGE_SIZE = 128

HEAD_DIM = 64
MIX_W = D_MODEL
N_HEADS = MIX_W // HEAD_DIM
H_FOX = N_HEADS // 2
H_NSA = N_HEADS - H_FOX
NSA_KV = 2
NSA_GH = H_NSA // NSA_KV
FOX_W = H_FOX * HEAD_DIM
NSA_W = H_NSA * HEAD_DIM
KV_W = NSA_KV * HEAD_DIM
CMP_LEN = 32
CMP_STRIDE = 16
CMP_HID = 4 * HEAD_DIM
SEL_LEN = 64
N_SELECT = 16
WINDOW = 512
Q_BLOCK = 128
N_BUCKETS = 32
MAX_DISTANCE = 128
D_FF = ((8 * D_MODEL // 3 + 127) // 128) * 128
CONV_W = 3
FORGET_BIAS = 4.0
IN_COLS = 3 * FOX_W + H_FOX + NSA_W + 6 * KV_W + 3 * H_NSA
EPS = 1e-6
NEG = -1e30
TINY = 1e-30
FORCE_SCORE = 1e4
SCALE = HEAD_DIM ** -0.5

kernel_name = 'hymba_fox_nsa_convffn_step'


def rmsnorm(x, g):
    xf = x.astype(jnp.float32)
    r = lax.rsqrt(jnp.mean(xf * xf, axis=-1, keepdims=True) + EPS)
    return (xf * r).astype(x.dtype) * g


def masked_softmax(s, mask):
    s = jnp.where(mask, s, NEG)
    m = jnp.max(s, axis=-1, keepdims=True)
    p = jnp.where(mask, jnp.exp(s - m), 0.0)
    return p / jnp.maximum(jnp.sum(p, axis=-1, keepdims=True), TINY)


def rel_bucket(dist):
    n = jnp.maximum(dist, 0)
    exact = N_BUCKETS // 2
    nf = jnp.maximum(n, 1).astype(jnp.float32)
    far = exact + (jnp.log(nf / exact) / math.log(MAX_DISTANCE / exact) * (N_BUCKETS - exact)).astype(jnp.int32)
    return jnp.where(n < exact, n, jnp.minimum(far, N_BUCKETS - 1))


def project(h, w_in, b_forget):
    B, T = h.shape[:2]
    z = h @ w_in
    sizes = (FOX_W, FOX_W, FOX_W, H_FOX, NSA_W) + (KV_W,) * 6 + (3 * H_NSA,)
    parts = jnp.split(z, np.cumsum(sizes)[:-1].tolist(), axis=-1)
    fox = [p.reshape(B, T, H_FOX, HEAD_DIM) for p in parts[0:3]]
    kvs = [p.reshape(B, T, NSA_KV, HEAD_DIM) for p in parts[5:11]]
    return {
        'q_fox': fox[0],
        'fox_kv': jnp.stack([fox[1], fox[2]], axis=2),
        'logf': jax.nn.log_sigmoid((parts[3] + b_forget).astype(jnp.float32)),
        'q_nsa': parts[4].reshape(B, T, NSA_KV, NSA_GH, HEAD_DIM),
        'nsa_kv': jnp.stack(kvs[0:4], axis=2),
        'win_kv': jnp.stack(kvs[4:6], axis=2),
        'gates': jax.nn.sigmoid(parts[11].reshape(B, T, 3, NSA_KV, NSA_GH)),
    }


def fox_attend(q, q_cum, q_pos, k, v, k_cum, k_pos):
    s = jnp.einsum('bqhd,bkhd->bhqk', q, k).astype(jnp.float32) * SCALE
    s = s + (q_cum.transpose(0, 2, 1)[:, :, :, None] - k_cum.transpose(0, 2, 1)[:, :, None, :])
    p = masked_softmax(s, k_pos[None, :] <= q_pos[:, None])
    return jnp.einsum('bhqk,bkhd->bqhd', p.astype(v.dtype), v)


def fox_prompt(q, fox_kv, logf):
    B, S = q.shape[:2]
    cum = jnp.cumsum(logf, axis=1)
    k, v = fox_kv[:, :, 0], fox_kv[:, :, 1]
    pos = jnp.arange(S)

    def block(i):
        qs = i * Q_BLOCK
        return fox_attend(lax.dynamic_slice_in_dim(q, qs, Q_BLOCK, 1),
                          lax.dynamic_slice_in_dim(cum, qs, Q_BLOCK, 1),
                          qs + jnp.arange(Q_BLOCK), k, v, cum, pos)

    o = lax.map(block, jnp.arange(S // Q_BLOCK))
    return o.transpose(1, 0, 2, 3, 4).reshape(B, S, FOX_W)


def fox_sample(q, kv_new, logf_new, kv_past, logf_past):
    B, Tq = q.shape[:2]
    past = kv_past.shape[1]
    kv = jnp.concatenate([kv_past, kv_new.astype(kv_past.dtype)], axis=1)
    lf = jnp.concatenate([logf_past.astype(jnp.float32), logf_new.astype(jnp.float32)], axis=1)
    cum = jnp.cumsum(lf, axis=1)
    o = fox_attend(q, cum[:, past:], past + jnp.arange(Tq), kv[:, :, 0], kv[:, :, 1], cum,
                   jnp.arange(past + Tq))
    return o.reshape(B, Tq, FOX_W)


def compress(rows, pos, w1, b1, w2):
    B, T = rows.shape[:2]
    c = rows.reshape(B, T // CMP_STRIDE, CMP_STRIDE, NSA_KV, HEAD_DIM)
    blocks = jnp.concatenate([c[:, :-1], c[:, 1:]], axis=2) + pos[None, None, :, None, :]
    n_cmp = blocks.shape[1]
    flat = blocks.transpose(0, 1, 3, 2, 4).reshape(B, n_cmp, NSA_KV, CMP_LEN * HEAD_DIM)
    return jax.nn.gelu(flat @ w1 + b1) @ w2


def nsa_branch_keys(rows, cmp_l):
    pos, w1, b1, w2 = cmp_l
    B, T = rows.shape[:2]
    kc = compress(rows[:, :, 0], pos[0], w1[0], b1[0], w2[0])
    vc = compress(rows[:, :, 1], pos[1], w1[1], b1[1], w2[1])
    sel = rows[:, :, 2:4].reshape(B, T // SEL_LEN, SEL_LEN, 2, NSA_KV, HEAD_DIM).transpose(3, 0, 4, 1, 2, 5)
    return kc, vc, sel[0], sel[1]


def cmp_to_sel(n_cmp, n_sel):
    start = jnp.arange(n_cmp) * CMP_STRIDE
    js = jnp.arange(n_sel) * SEL_LEN
    return ((start[:, None] < js[None, :] + SEL_LEN) & (start[:, None] + CMP_LEN > js[None, :])).astype(jnp.float32)


def nsa_attend(q, q_pos, kc, vc, ks_b, vs_b, kw, vw, kw_pos, gates, rel_bias):
    B, Tq = q.shape[:2]
    rb = rel_bias.reshape(N_BUCKETS, NSA_KV, NSA_GH)
    n_cmp = kc.shape[1]
    dist_c = q_pos[:, None] - (jnp.arange(n_cmp) * CMP_STRIDE + (CMP_LEN - 1))[None, :]
    s_c = jnp.einsum('bqghd,bngd->bghqn', q, kc).astype(jnp.float32) * SCALE
    s_c = s_c + rb[rel_bucket(dist_c)].transpose(2, 3, 0, 1)
    p_c = masked_softmax(s_c, dist_c >= 0)
    o_c = jnp.einsum('bghqn,bngd->bqghd', p_c.astype(vc.dtype), vc)
    n_sel = ks_b.shape[2]
    k_eff = min(N_SELECT, n_sel)
    p_slc = jnp.einsum('bghqn,nj->bqgj', p_c, cmp_to_sel(n_cmp, n_sel))
    jb = jnp.arange(n_sel)[None, :]
    qblk = (q_pos // SEL_LEN)[:, None]
    valid = jb * SEL_LEN <= q_pos[:, None]
    forced = (jb == 0) | (jb == qblk) | (jb == qblk - 1)
    score = jnp.where(forced[None, :, None, :], FORCE_SCORE, jnp.where(valid[None, :, None, :], p_slc, -1.0))
    top_s, idx = lax.top_k(score, k_eff)
    idx_t = idx.transpose(0, 2, 1, 3).reshape(B, NSA_KV, Tq * k_eff)
    bi = jnp.arange(B)[:, None, None]
    gi = jnp.arange(NSA_KV)[None, :, None]
    n_s = k_eff * SEL_LEN
    kg = ks_b[bi, gi, idx_t].reshape(B, NSA_KV, Tq, n_s, HEAD_DIM)
    vg = vs_b[bi, gi, idx_t].reshape(B, NSA_KV, Tq, n_s, HEAD_DIM)
    s_pos = (idx_t[..., None] * SEL_LEN + jnp.arange(SEL_LEN)).reshape(B, NSA_KV, Tq, n_s)
    dist_s = q_pos[None, None, :, None] - s_pos
    ok = jnp.repeat(top_s.transpose(0, 2, 1, 3) >= 0, SEL_LEN, axis=-1)
    bias_s = rb.transpose(1, 0, 2)[gi[..., None], rel_bucket(dist_s)].transpose(0, 1, 4, 2, 3)
    s_s = jnp.einsum('bqghd,bgqsd->bghqs', q, kg).astype(jnp.float32) * SCALE + bias_s
    p_s = masked_softmax(s_s, (ok & (dist_s >= 0))[:, :, None])
    o_s = jnp.einsum('bghqs,bgqsd->bqghd', p_s.astype(vg.dtype), vg)
    dist_w = q_pos[:, None] - kw_pos[None, :]
    mask_w = (dist_w >= 0) & (dist_w < WINDOW) & (kw_pos[None, :] >= 0)
    s_w = jnp.einsum('bqghd,bkgd->bghqk', q, kw).astype(jnp.float32) * SCALE
    s_w = s_w + rb[rel_bucket(dist_w)].transpose(2, 3, 0, 1)
    p_w = masked_softmax(s_w, mask_w)
    o_w = jnp.einsum('bghqk,bkgd->bqghd', p_w.astype(vw.dtype), vw)
    o = gates[:, :, 0][..., None] * o_c + gates[:, :, 1][..., None] * o_s + gates[:, :, 2][..., None] * o_w
    return o.reshape(B, Tq, NSA_W)


def nsa_prompt(q, nsa_kv, win_kv, gates, cmp_l, rel_bias):
    B, S = q.shape[:2]
    kc, vc, ks_b, vs_b = nsa_branch_keys(nsa_kv, cmp_l)
    kwp = jnp.pad(win_kv, ((0, 0), (WINDOW, 0), (0, 0), (0, 0), (0, 0)))

    def block(i):
        qs = i * Q_BLOCK
        kw = lax.dynamic_slice_in_dim(kwp, qs, WINDOW + Q_BLOCK, 1)
        kw_pos = qs - WINDOW + jnp.arange(WINDOW + Q_BLOCK)
        return nsa_attend(lax.dynamic_slice_in_dim(q, qs, Q_BLOCK, 1), qs + jnp.arange(Q_BLOCK),
                          kc, vc, ks_b, vs_b, kw[:, :, 0], kw[:, :, 1], kw_pos,
                          lax.dynamic_slice_in_dim(gates, qs, Q_BLOCK, 1), rel_bias)

    o = lax.map(block, jnp.arange(S // Q_BLOCK))
    return o.transpose(1, 0, 2, 3).reshape(B, S, NSA_W)


def nsa_sample(q, nsa_new, win_new, gates, nsa_past, win_buf, cmp_l, rel_bias):
    B, Tq = q.shape[:2]
    past = nsa_past.shape[1]
    t_all = past + Tq
    t_pad = -(-t_all // SEL_LEN) * SEL_LEN
    rows = jnp.concatenate([nsa_past, nsa_new.astype(nsa_past.dtype)], axis=1)
    rows = jnp.pad(rows, ((0, 0), (0, t_pad - t_all), (0, 0), (0, 0), (0, 0)))
    kc, vc, ks_b, vs_b = nsa_branch_keys(rows, cmp_l)
    w_buf = win_buf.shape[1]
    kw = jnp.concatenate([win_buf, win_new.astype(win_buf.dtype)], axis=1)
    kw_pos = past - w_buf + jnp.arange(w_buf + Tq)
    o = nsa_attend(q, past + jnp.arange(Tq), kc, vc, ks_b, vs_b, kw[:, :, 0], kw[:, :, 1], kw_pos, gates, rel_bias)
    return o, kw[:, -w_buf:]


def merge_heads(o_fox, o_nsa, g, w_out):
    o = jnp.concatenate([rmsnorm(o_fox, g[:FOX_W]), rmsnorm(o_nsa, g[FOX_W:])], axis=-1)
    return o @ w_out


def conv_ffn(x, hist, g_norm, w_gu, conv_w, conv_b, w_down):
    T = x.shape[1]
    gu = rmsnorm(x, g_norm) @ w_gu
    g, u = gu[..., :D_FF], gu[..., D_FF:]
    g_ext = jnp.concatenate([hist.astype(g.dtype), g], axis=1)
    gc = conv_b
    for i in range(CONV_W):
        gc = gc + conv_w[i] * g_ext[:, i:i + T]
    y = (jax.nn.silu(gc) * u) @ w_down
    return x + y, g_ext[:, -(CONV_W - 1):]


def setup_inputs(seed: int = 0) -> dict:
    key = jax.random.key(seed)
    ks = jax.random.split(key, 24)
    n_pages = PAST_LEN // PAGE_SIZE
    n_used = DEC_BATCH * n_pages
    n_pool = n_used + max(1, n_used // 4)
    w_buf = min(WINDOW, PAST_LEN)

    def nrm(k, shape, s=1.0):
        return s * jax.random.normal(k, shape, jnp.float32)

    page_table = jax.random.permutation(ks[0], n_pool)[:n_used].reshape(DEC_BATCH, n_pages).astype(jnp.int32)
    return {
        'x_prompt': nrm(ks[1], (BATCH, SEQ, D_MODEL)),
        'x_sample': nrm(ks[2], (DEC_BATCH, DEC_SEQ, D_MODEL)),
        'cache_fox_kv': nrm(ks[3], (DEPTH, n_pool, PAGE_SIZE, 2, H_FOX, HEAD_DIM)),
        'cache_fox_logf': jax.nn.log_sigmoid(FORGET_BIAS + nrm(ks[4], (DEPTH, n_pool, PAGE_SIZE, H_FOX))),
        'cache_nsa_kv': nrm(ks[5], (DEPTH, n_pool, PAGE_SIZE, 4, NSA_KV, HEAD_DIM)),
        'state_win_kv': nrm(ks[6], (DEPTH, DEC_BATCH, w_buf, 2, NSA_KV, HEAD_DIM)),
        'state_conv': nrm(ks[7], (DEPTH, DEC_BATCH, CONV_W - 1, D_FF)),
        'page_table': page_table,
        'norm1_g': 1.0 + nrm(ks[8], (DEPTH, D_MODEL), 0.05),
        'w_in': nrm(ks[9], (DEPTH, D_MODEL, IN_COLS), D_MODEL ** -0.5),
        'b_forget': FORGET_BIAS + nrm(ks[10], (DEPTH, H_FOX), 0.1),
        'cmp_pos': nrm(ks[11], (DEPTH, 2, CMP_LEN, HEAD_DIM), 0.1),
        'cmp_w1': nrm(ks[12], (DEPTH, 2, CMP_LEN * HEAD_DIM, CMP_HID), (CMP_LEN * HEAD_DIM) ** -0.5),
        'cmp_b1': nrm(ks[13], (DEPTH, 2, CMP_HID), 0.01),
        'cmp_w2': nrm(ks[14], (DEPTH, 2, CMP_HID, HEAD_DIM), CMP_HID ** -0.5),
        'out_norm_g': 1.0 + nrm(ks[15], (DEPTH, MIX_W), 0.05),
        'w_out': nrm(ks[16], (DEPTH, MIX_W, D_MODEL), MIX_W ** -0.5),
        'norm2_g': 1.0 + nrm(ks[17], (DEPTH, D_MODEL), 0.05),
        'w_gu': nrm(ks[18], (DEPTH, D_MODEL, 2 * D_FF), D_MODEL ** -0.5),
        'conv_w': nrm(ks[19], (DEPTH, CONV_W, D_FF), CONV_W ** -0.5),
        'conv_b': nrm(ks[20], (DEPTH, D_FF), 0.01),
        'w_down': nrm(ks[21], (DEPTH, D_FF, D_MODEL), D_FF ** -0.5),
        'rel_bias': nrm(ks[22], (N_BUCKETS, H_NSA), 0.3),
        'final_norm_g': 1.0 + nrm(ks[23], (D_MODEL,), 0.05),
    }


def reference(x_prompt, x_sample, cache_fox_kv, cache_fox_logf, cache_nsa_kv, state_win_kv, state_conv,
              page_table, norm1_g, w_in, b_forget, cmp_pos, cmp_w1, cmp_b1, cmp_w2, out_norm_g, w_out,
              norm2_g, w_gu, conv_w, conv_b, w_down, rel_bias, final_norm_g):
    n_dec, n_pages = page_table.shape
    past = n_pages * PAGE_SIZE
    bp, sp = x_prompt.shape[:2]
    xp, xs = x_prompt, x_sample
    fkv_p, fkv_s, flf_p, flf_s, nkv_p, nkv_s, win_p, win_s, cv_p, cv_s = ([] for _ in range(10))
    for l in range(DEPTH):
        cmp_l = (cmp_pos[l], cmp_w1[l], cmp_b1[l], cmp_w2[l])
        pp = project(rmsnorm(xp, norm1_g[l]), w_in[l], b_forget[l])
        o_fox = fox_prompt(pp['q_fox'], pp['fox_kv'], pp['logf'])
        o_nsa = nsa_prompt(pp['q_nsa'], pp['nsa_kv'], pp['win_kv'], pp['gates'], cmp_l, rel_bias)
        xp = xp + merge_heads(o_fox, o_nsa, out_norm_g[l], w_out[l])
        xp, conv_new_p = conv_ffn(xp, jnp.zeros((bp, CONV_W - 1, D_FF), xp.dtype), norm2_g[l], w_gu[l],
                                  conv_w[l], conv_b[l], w_down[l])
        ps = project(rmsnorm(xs, norm1_g[l]), w_in[l], b_forget[l])
        fox_past = cache_fox_kv[l, page_table].reshape(n_dec, past, 2, H_FOX, HEAD_DIM)
        logf_past = cache_fox_logf[l, page_table].reshape(n_dec, past, H_FOX)
        nsa_past = cache_nsa_kv[l, page_table].reshape(n_dec, past, 4, NSA_KV, HEAD_DIM)
        o_fox_s = fox_sample(ps['q_fox'], ps['fox_kv'], ps['logf'], fox_past, logf_past)
        o_nsa_s, win_new_s = nsa_sample(ps['q_nsa'], ps['nsa_kv'], ps['win_kv'], ps['gates'], nsa_past,
                                        state_win_kv[l], cmp_l, rel_bias)
        xs = xs + merge_heads(o_fox_s, o_nsa_s, out_norm_g[l], w_out[l])
        xs, conv_new_s = conv_ffn(xs, state_conv[l], norm2_g[l], w_gu[l], conv_w[l], conv_b[l], w_down[l])
        fkv_p.append(pp['fox_kv'])
        fkv_s.append(ps['fox_kv'])
        flf_p.append(pp['logf'])
        flf_s.append(ps['logf'])
        nkv_p.append(pp['nsa_kv'])
        nkv_s.append(ps['nsa_kv'])
        win_p.append(pp['win_kv'][:, -min(WINDOW, sp):])
        win_s.append(win_new_s)
        cv_p.append(conv_new_p)
        cv_s.append(conv_new_s)
    y_prompt = rmsnorm(xp, final_norm_g)
    y_sample = rmsnorm(xs, final_norm_g)
    return (y_prompt, y_sample,
            jnp.stack(fkv_p), jnp.stack(fkv_s),
            jnp.stack(flf_p), jnp.stack(flf_s),
            jnp.stack(nkv_p), jnp.stack(nkv_s),
            jnp.stack(win_p), jnp.stack(win_s),
            jnp.stack(cv_p), jnp.stack(cv_s))
```

```python
import functools
import math

import numpy as np
import jax
import jax.numpy as jnp
from jax import lax
from jax.experimental import pallas as pl
from jax.experimental.pallas import tpu as pltpu

F32 = jnp.float32
BF16 = jnp.bfloat16
I32 = jnp.int32

HD = 64
H_FOX = 8
H_NSA = 8
NSA_KV = 2
NSA_GH = 4
FOX_W = H_FOX * HD
NSA_W = H_NSA * HD
CMP_LEN = 32
CMP_STRIDE = 16
CMP_HID = 256
SEL_LEN = 64
N_SELECT = 16
WINDOW = 512
N_BUCKETS = 32
MAX_DISTANCE = 128
CONV_W = 3
PAGE = 128
EPS = 1e-6
NEG = -1e30
TINY = 1e-30
FORCE_SCORE = 1e4
SCALE = HD ** -0.5

LANES = 128
VMEM_LIMIT = 56 * 1024 * 1024

C_QF, C_FKV, C_QN, C_NKV, C_WKV, C_MISC, C_END = 0, 512, 1536, 2048, 2560, 2816, 2944
MISC_LOGF = 0
MISC_GATE = 8

AUG0 = HD
AUG1 = HD + 3

PAGES_PER_STEP = 8


def _bucket_thresholds():
    exact = N_BUCKETS // 2
    n = np.arange(1, 4 * MAX_DISTANCE, dtype=np.float64)
    far = exact + (np.log(n / exact) / math.log(MAX_DISTANCE / exact) * (N_BUCKETS - exact)).astype(np.int64)
    b = np.where(n < exact, n, np.minimum(far, N_BUCKETS - 1)).astype(np.int64)
    return [int(n[b >= k].min()) for k in range(exact + 1, N_BUCKETS)]


_THR = _bucket_thresholds()


def _cparams(sem):
    return pltpu.CompilerParams(dimension_semantics=sem, vmem_limit_bytes=VMEM_LIMIT)


def _dot(a, b):
    return jnp.dot(a, b, preferred_element_type=F32)


def _dot_nt(a, b):
    return lax.dot_general(a, b, (((1,), (1,)), ((), ())), preferred_element_type=F32)


def _split2(x):
    hi = x.astype(BF16)
    lo = (x - hi.astype(F32)).astype(BF16)
    return hi, lo


def _split3(x):
    hi = x.astype(BF16)
    r = x - hi.astype(F32)
    mid = r.astype(BF16)
    lo = (r - mid.astype(F32)).astype(BF16)
    return hi, mid, lo


def _dot3_l(x, m):
    hi, mid, lo = _split3(x)
    return _dot(hi, m) + _dot(mid, m) + _dot(lo, m)


def _dot3_r(m, x):
    hi, mid, lo = _split3(x)
    return _dot(m, hi) + _dot(m, mid) + _dot(m, lo)


def _dot2_l(x, m):
    hi, lo = _split2(x)
    return _dot(hi, m) + _dot(lo, m)


def _iota(shape, dim):
    return lax.broadcasted_iota(I32, shape, dim)


def _div(x, k):
    return lax.shift_right_arithmetic(x, jnp.int32(k.bit_length() - 1))


def _mod(x, k):
    return x & (k - 1)


def _rms(x, g):
    r = lax.rsqrt(jnp.mean(x * x, axis=-1, keepdims=True) + EPS)
    return (x * r) * g


def _bucket(d):
    far = jnp.full(d.shape, N_BUCKETS // 2, I32)
    for thr in _THR:
        far = far + (d >= thr).astype(I32)
    return jnp.where(d < N_BUCKETS // 2, d, far)


def _bias_lookup(bucket, rb_get):
    acc = jnp.zeros(bucket.shape, F32)
    for b in range(N_BUCKETS):
        acc = jnp.where(bucket == b, rb_get(b), acc)
    return acc


def _topk_mask(score, k):
    idx = _iota(score.shape, 1).astype(F32)

    def body(_, c):
        work, sel = c
        mx = jnp.max(work, axis=1, keepdims=True)
        first = jnp.min(jnp.where(work == mx, idx, 1e9), axis=1, keepdims=True)
        hit = idx == first
        return jnp.where(hit, -3.0, work), jnp.where(hit, 1.0, sel)

    _, sel = lax.fori_loop(0, k, body, (score, jnp.zeros(score.shape, F32)))
    return sel


def _proj_kernel(x_ref, g_ref, w_ref, bf_ref, qf_ref, fkv_ref, qn_ref, nkv_ref, wkv_ref, misc_ref):
    h = _rms(x_ref[...], g_ref[...]).astype(BF16)
    qf_ref[...] = _dot(h, w_ref[:, C_QF:C_FKV])
    fkv_ref[...] = _dot(h, w_ref[:, C_FKV:C_QN])
    qn_ref[...] = _dot(h, w_ref[:, C_QN:C_NKV])
    nkv_ref[...] = _dot(h, w_ref[:, C_NKV:C_WKV])
    wkv_ref[...] = _dot(h, w_ref[:, C_WKV:C_MISC])
    z = _dot(h, w_ref[:, C_MISC:C_END]) + bf_ref[...]
    lane = _iota(z.shape, 1)
    logsig = jnp.minimum(z, 0.0) - jnp.log(1.0 + jnp.exp(-jnp.abs(z)))
    sig = 1.0 / (1.0 + jnp.exp(-z))
    misc_ref[...] = jnp.where(lane < MISC_GATE, logsig, jnp.where(lane < MISC_GATE + 3 * H_NSA, sig, 0.0))


def _proj(x2, g, w, bf):
    T, D = x2.shape
    tm = min(512, T)
    widths = (C_FKV - C_QF, C_QN - C_FKV, C_NKV - C_QN, C_WKV - C_NKV, C_MISC - C_WKV, C_END - C_MISC)
    return pl.pallas_call(
        _proj_kernel,
        grid=(T // tm,),
        in_specs=[pl.BlockSpec((tm, D), lambda i: (i, 0)),
                  pl.BlockSpec((1, D), lambda i: (0, 0)),
                  pl.BlockSpec((D, C_END), lambda i: (0, 0)),
                  pl.BlockSpec((1, LANES), lambda i: (0, 0))],
        out_specs=[pl.BlockSpec((tm, wd), lambda i: (i, 0)) for wd in widths],
        out_shape=[jax.ShapeDtypeStruct((T, wd), F32) for wd in widths],
        compiler_params=_cparams(("arbitrary",)),
        name="proj",
    )(x2, g, w, bf)


def _fox_prep_kernel(qf_ref, k_ref, v_ref, misc_ref, qa_ref, ka_ref, vb_ref, carry_ref):
    i = pl.program_id(1)
    tm = qf_ref.shape[1]

    @pl.when(i == 0)
    def _():
        carry_ref[...] = jnp.zeros_like(carry_ref)

    lane = _iota((tm, LANES), 1)
    lf = jnp.where(lane < H_FOX, misc_ref[0], 0.0)
    tril = (_iota((tm, tm), 0) >= _iota((tm, tm), 1)).astype(BF16)
    cum = _dot3_r(tril, lf) + carry_ref[0:1, :]
    carry_ref[...] = jnp.broadcast_to(cum[tm - 1:tm, :], carry_ref.shape)
    hi = cum.astype(BF16).astype(F32)
    r = cum - hi
    mid = r.astype(BF16).astype(F32)
    lo = (r - mid).astype(BF16).astype(F32)
    cc = (hi + pltpu.roll(mid, H_FOX, 1) + pltpu.roll(lo, 2 * H_FOX, 1)).astype(BF16)
    er = _iota((LANES, H_FOX * LANES), 0)
    ec = _iota((LANES, H_FOX * LANES), 1)
    part = _div(er, H_FOX)
    head = _mod(er, H_FOX)
    inb = er < 3 * H_FOX
    eq = jnp.where(inb & (ec == head * LANES + AUG0 + part), 1.0, 0.0).astype(BF16)
    ek = jnp.where(inb & (ec == head * LANES + AUG1 + part), -1.0, 0.0).astype(BF16)
    cl = _mod(_iota((1, H_FOX * LANES), 1), LANES)
    ones_q = jnp.where((cl >= AUG1) & (cl < AUG1 + 3), 1.0, 0.0)
    ones_k = jnp.where((cl >= AUG0) & (cl < AUG0 + 3), 1.0, 0.0)
    cols_q = _dot(cc, eq) + ones_q
    cols_k = _dot(cc, ek) + ones_k
    q = qf_ref[0]
    k = k_ref[0]
    for h in range(H_FOX):
        a = (h // 2) * LANES
        qt = q[:, a:a + LANES]
        kt = k[:, a:a + LANES]
        if h % 2:
            qt = pltpu.roll(qt, HD, 1)
            kt = pltpu.roll(kt, HD, 1)
        qa_ref[0, h] = jnp.where(lane < HD, qt * SCALE, cols_q[:, h * LANES:(h + 1) * LANES]).astype(BF16)
        ka_ref[0, h] = jnp.where(lane < HD, kt, cols_k[:, h * LANES:(h + 1) * LANES]).astype(BF16)
    vb_ref[0] = v_ref[0].astype(BF16)


def _fox_prep(qf, fkv, misc):
    B, S, _ = qf.shape
    tm = min(512, S)
    return pl.pallas_call(
        _fox_prep_kernel,
        grid=(B, S // tm),
        in_specs=[pl.BlockSpec((1, tm, FOX_W), lambda b, i: (b, i, 0)),
                  pl.BlockSpec((1, tm, FOX_W), lambda b, i: (b, i, 0)),
                  pl.BlockSpec((1, tm, FOX_W), lambda b, i: (b, i, 1)),
                  pl.BlockSpec((1, tm, LANES), lambda b, i: (b, i, 0))],
        out_specs=[pl.BlockSpec((1, H_FOX, tm, LANES), lambda b, i: (b, 0, i, 0)),
                   pl.BlockSpec((1, H_FOX, tm, LANES), lambda b, i: (b, 0, i, 0)),
                   pl.BlockSpec((1, tm, FOX_W), lambda b, i: (b, i, 0))],
        out_shape=[jax.ShapeDtypeStruct((B, H_FOX, S, LANES), BF16),
                   jax.ShapeDtypeStruct((B, H_FOX, S, LANES), BF16),
                   jax.ShapeDtypeStruct((B, S, FOX_W), BF16)],
        scratch_shapes=[pltpu.VMEM((8, LANES), F32)],
        compiler_params=_cparams(("arbitrary", "arbitrary")),
        name="fox_prep",
    )(qf, fkv, fkv, misc)


def _softmax_update(s, v, m_ref, l_ref, acc_ref):
    m_old = m_ref[...]
    m_new = jnp.maximum(m_old, jnp.max(s, axis=1, keepdims=True))
    alpha = jnp.exp(m_old - m_new)
    p = jnp.exp(s - m_new)
    l_ref[...] = alpha * l_ref[...] + jnp.sum(p, axis=1, keepdims=True)
    acc_ref[...] = alpha * acc_ref[...] + _dot(p.astype(BF16), v)
    m_ref[...] = m_new


def _softmax_init(m_ref, l_ref, acc_ref):
    m_ref[...] = jnp.full(m_ref.shape, NEG, F32)
    l_ref[...] = jnp.zeros(l_ref.shape, F32)
    acc_ref[...] = jnp.zeros(acc_ref.shape, F32)


def _fox_flash_kernel(qa_ref, ka_ref, v_ref, o_ref, m_ref, l_ref, acc_ref):
    qi = pl.program_id(2)
    tq = qa_ref.shape[2]
    outs = []
    for hh in range(2):
        q = qa_ref[0, hh]
        _softmax_init(m_ref, l_ref, acc_ref)

        def full_tile(kt, c, hh=hh, q=q):
            ks = pl.multiple_of(kt * tq, tq)
            s = _dot_nt(q, ka_ref[0, hh, pl.ds(ks, tq), :])
            _softmax_update(s, v_ref[0, pl.ds(ks, tq), :], m_ref, l_ref, acc_ref)
            return c

        lax.fori_loop(0, qi, full_tile, 0)
        ks = pl.multiple_of(qi * tq, tq)
        s = _dot_nt(q, ka_ref[0, hh, pl.ds(ks, tq), :])
        s = jnp.where(_iota(s.shape, 1) <= _iota(s.shape, 0), s, NEG)
        _softmax_update(s, v_ref[0, pl.ds(ks, tq), :], m_ref, l_ref, acc_ref)
        outs.append(acc_ref[...] / jnp.maximum(l_ref[...], TINY))
    lane = _iota(outs[0].shape, 1)
    o_ref[0] = jnp.where(lane < HD, outs[0], outs[1])


def _fox_flash(qa, ka, vb):
    B, H, S, _ = qa.shape
    tq = min(512, S)
    return pl.pallas_call(
        _fox_flash_kernel,
        grid=(B, H // 2, S // tq),
        in_specs=[pl.BlockSpec((1, 2, tq, LANES), lambda b, p, i: (b, p, i, 0)),
                  pl.BlockSpec((1, 2, S, LANES), lambda b, p, i: (b, p, 0, 0)),
                  pl.BlockSpec((1, S, LANES), lambda b, p, i: (b, 0, p))],
        out_specs=pl.BlockSpec((1, tq, LANES), lambda b, p, i: (b, i, p)),
        out_shape=jax.ShapeDtypeStruct((B, S, FOX_W), F32),
        scratch_shapes=[pltpu.VMEM((tq, 1), F32), pltpu.VMEM((tq, 1), F32), pltpu.VMEM((tq, LANES), F32)],
        compiler_params=_cparams(("arbitrary", "arbitrary", "arbitrary")),
        name="fox_flash",
    )(qa, ka, vb)


def _gelu_tanh(x):
    return 0.5 * x * (1.0 + jnp.tanh(math.sqrt(2.0 / math.pi) * (x + 0.044715 * (x * x * x))))


def _compress_core(x, pt, pb, w1t, w1b, b1, w2w):
    n = x.shape[0]
    a = _dot((x + pt).astype(BF16), w1t)
    b = _dot((x + pb).astype(BF16), w1b)
    h = a + pltpu.roll(b, n - 1, 0) + b1
    return _dot(_gelu_tanh(h).astype(BF16), w2w)


CMP_PAD_FRONT = 16
CMP_PAD_BACK = 112


def _store_cmp(out_ref, idx, res, n):
    out_ref[idx] = jnp.zeros(out_ref.shape[len(idx):], BF16)
    out_ref[idx + (slice(CMP_PAD_FRONT, CMP_PAD_FRONT + n), slice(None))] = res.astype(BF16)


def _compress_prompt_kernel(x_ref, pt_ref, pb_ref, w1t_ref, w1b_ref, b1_ref, w2w_ref, out_ref):
    n = x_ref.shape[1] // CMP_STRIDE
    x = jnp.concatenate([x_ref[0, pl.ds(j, n, stride=CMP_STRIDE), :] for j in range(CMP_STRIDE)], axis=1)
    res = _compress_core(x, pt_ref[0], pb_ref[0], w1t_ref[0], w1b_ref[0], b1_ref[0], w2w_ref[0])
    _store_cmp(out_ref, (0, 0), res, n)


def _compress_prompt(nkv, cw):
    B, S, _ = nkv.shape
    n = S // CMP_STRIDE
    npad = n + CMP_PAD_FRONT + CMP_PAD_BACK
    kx = CMP_STRIDE * LANES
    return pl.pallas_call(
        _compress_prompt_kernel,
        grid=(B, 2),
        in_specs=[pl.BlockSpec((1, S, LANES), lambda b, w: (b, 0, w)),
                  pl.BlockSpec((1, 1, kx), lambda b, w: (w, 0, 0)),
                  pl.BlockSpec((1, 1, kx), lambda b, w: (w, 0, 0)),
                  pl.BlockSpec((1, kx, 2 * CMP_HID), lambda b, w: (w, 0, 0)),
                  pl.BlockSpec((1, kx, 2 * CMP_HID), lambda b, w: (w, 0, 0)),
                  pl.BlockSpec((1, 1, 2 * CMP_HID), lambda b, w: (w, 0, 0)),
                  pl.BlockSpec((1, 2 * CMP_HID, NSA_W), lambda b, w: (w, 0, 0))],
        out_specs=pl.BlockSpec((1, 1, npad, NSA_W), lambda b, w: (b, w, 0, 0)),
        out_shape=jax.ShapeDtypeStruct((B, 2, npad, NSA_W), BF16),
        compiler_params=_cparams(("arbitrary", "arbitrary")),
        name="compress_prompt",
    )(nkv, cw["pt"], cw["pb"], cw["w1t"], cw["w1b"], cw["b1"], cw["w2w"])


def _compress_sample_kernel(pt_tab, *refs):
    P = PAGES_PER_STEP
    pages = (refs[:P], refs[P:2 * P])
    pt_ref, pb_ref, w1t_ref, w1b_ref, b1_ref, w2w_ref, out_ref, x_sc = refs[2 * P:]
    s = pl.program_id(1)
    n = x_sc.shape[1]
    rows = P * (PAGE // CMP_STRIDE)
    r0 = pl.multiple_of(s * rows, rows)
    for w in range(2):
        for j in range(CMP_STRIDE):
            x_sc[w, pl.ds(r0, rows), j * LANES:(j + 1) * LANES] = jnp.concatenate(
                [pg[pl.ds(j, PAGE // CMP_STRIDE, stride=CMP_STRIDE), :] for pg in pages[w]], axis=0)

    @pl.when(s == pl.num_programs(1) - 1)
    def _():
        for w in range(2):
            res = _compress_core(x_sc[w], pt_ref[w], pb_ref[w], w1t_ref[w], w1b_ref[w], b1_ref[w], w2w_ref[w])
            _store_cmp(out_ref, (0, w), res, n)


def _compress_sample(cache_nsa, page_table, layer, cw):
    nb, npages = page_table.shape
    P = PAGES_PER_STEP
    n = npages * (PAGE // CMP_STRIDE)
    npad = n + CMP_PAD_FRONT + CMP_PAD_BACK
    kx = CMP_STRIDE * LANES

    def page_spec(k, w):
        return pl.BlockSpec((None, None, PAGE, LANES), lambda b, s, pt: (layer, pt[b, s * P + k], 0, w))

    const3 = lambda b, s, pt: (0, 0, 0)
    return pl.pallas_call(
        _compress_sample_kernel,
        grid_spec=pltpu.PrefetchScalarGridSpec(
            num_scalar_prefetch=1,
            grid=(nb, npages // P),
            in_specs=[page_spec(k, 0) for k in range(P)] + [page_spec(k, 1) for k in range(P)] + [
                pl.BlockSpec((2, 1, kx), const3), pl.BlockSpec((2, 1, kx), const3),
                pl.BlockSpec((2, kx, 2 * CMP_HID), const3), pl.BlockSpec((2, kx, 2 * CMP_HID), const3),
                pl.BlockSpec((2, 1, 2 * CMP_HID), const3), pl.BlockSpec((2, 2 * CMP_HID, NSA_W), const3)],
            out_specs=pl.BlockSpec((1, 2, npad, NSA_W), lambda b, s, pt: (b, 0, 0, 0)),
            scratch_shapes=[pltpu.VMEM((2, n, kx), F32)]),
        out_shape=jax.ShapeDtypeStruct((nb, 2, npad, NSA_W), BF16),
        compiler_params=_cparams(("arbitrary", "arbitrary")),
        name="compress_sample",
    )(page_table, *([cache_nsa] * (2 * P)), cw["pt"], cw["pb"], cw["w1t"], cw["w1b"], cw["b1"], cw["w2w"])


def _nsa_prep_kernel(sel_ref, win_ref, ksa_ref, vsd_ref, kwa_ref, vwd_ref):
    i = pl.program_id(1)
    tm = sel_ref.shape[1]
    lane = _iota((tm, LANES), 1)
    pos = i * tm + _iota((tm, LANES), 0)
    onehot = jnp.where(_div(pos, SEL_LEN) == lane, 1.0, 0.0)
    for src, ka_ref, vd_ref, with_onehot in ((sel_ref, ksa_ref, vsd_ref, True), (win_ref, kwa_ref, vwd_ref, False)):
        x = src[0]
        kk = x[:, 0:LANES]
        vv = x[:, LANES:2 * LANES]
        rk = pltpu.roll(kk, HD, 1)
        rv = pltpu.roll(vv, HD, 1)
        for g in range(NSA_KV):
            kt = kk if g == 0 else rk
            if with_onehot:
                left = jnp.where(lane < HD, kt, jnp.where(lane < HD + 3, 1.0, 0.0))
                ka_ref[0, g] = jnp.concatenate([left, onehot], axis=1).astype(BF16)
            else:
                ka_ref[0, g] = jnp.where(lane < HD, kt, 0.0).astype(BF16)
            vd = jnp.where(lane < HD, vv, rv) if g == 0 else jnp.where(lane < HD, rv, vv)
            vd_ref[0, g] = vd.astype(BF16)


def _nsa_prep(nkv, wkv):
    B, S, _ = nkv.shape
    tm = min(512, S)
    spec128 = pl.BlockSpec((1, NSA_KV, tm, LANES), lambda b, i: (b, 0, i, 0))
    return pl.pallas_call(
        _nsa_prep_kernel,
        grid=(B, S // tm),
        in_specs=[pl.BlockSpec((1, tm, 2 * LANES), lambda b, i: (b, i, 1)),
                  pl.BlockSpec((1, tm, 2 * LANES), lambda b, i: (b, i, 0))],
        out_specs=[pl.BlockSpec((1, NSA_KV, tm, 2 * LANES), lambda b, i: (b, 0, i, 0)), spec128, spec128, spec128],
        out_shape=[jax.ShapeDtypeStruct((B, NSA_KV, S, 2 * LANES), BF16),
                   jax.ShapeDtypeStruct((B, NSA_KV, S, LANES), BF16),
                   jax.ShapeDtypeStruct((B, NSA_KV, S, LANES), BF16),
                   jax.ShapeDtypeStruct((B, NSA_KV, S, LANES), BF16)],
        compiler_params=_cparams(("arbitrary", "arbitrary")),
        name="nsa_prep",
    )(nkv, wkv)


def _overlap(i_blk, j_blk):
    start = i_blk * CMP_STRIDE
    return (start < j_blk * SEL_LEN + SEL_LEN) & (start + CMP_LEN > j_blk * SEL_LEN)


NEAR_BACK = 16


def _nsa_cmp_kernel(rb_ref, q_ref, kcw_ref, oc_ref, mq_ref, fc_ref):
    b = pl.program_id(0)
    i = pl.program_id(1)
    tq = q_ref.shape[1]
    n = kcw_ref.shape[2] - CMP_PAD_FRONT - CMP_PAD_BACK
    n_sel = n * CMP_STRIDE // SEL_LEN
    k_eff = min(N_SELECT, n_sel)

    @pl.when((b == 0) & (i == 0))
    def _():
        dist = _iota((tq, LANES), 0) + (NEAR_BACK * CMP_STRIDE - (CMP_LEN - 1)) - CMP_STRIDE * _iota((tq, LANES), 1)
        bk = _bucket(jnp.clip(dist, 0, MAX_DISTANCE - 1))
        for h in range(H_NSA):
            fc_ref[h] = jnp.where(dist >= 0, _bias_lookup(bk, lambda bb, h=h: rb_ref[bb, h]), NEG)

    qs = i * tq
    i0 = qs // CMP_STRIDE - NEAR_BACK
    q = q_ref[0] * SCALE
    lanehead = _div(_iota((tq, NSA_W), 1), HD)
    kfar = kcw_ref[0, 0, CMP_PAD_FRONT:CMP_PAD_FRONT + n, :]
    vfar = kcw_ref[0, 1, CMP_PAD_FRONT:CMP_PAD_FRONT + n, :]
    st = pl.multiple_of(qs // CMP_STRIDE, 16)
    knear = kcw_ref[0, 0, pl.ds(st, LANES), :]
    vnear = kcw_ref[0, 1, pl.ds(st, LANES), :]
    farmask = _iota((tq, n), 1) < i0
    nearmask = (_iota((tq, LANES), 1) + i0) >= 0
    oc = jnp.zeros((tq, NSA_W), F32)
    ps_far = [jnp.zeros((tq, n), F32) for _ in range(NSA_KV)]
    ps_near = [jnp.zeros((tq, LANES), F32) for _ in range(NSA_KV)]
    for h in range(H_NSA):
        g = h // NSA_GH
        qm = jnp.where(lanehead == h, q, 0.0).astype(BF16)
        sf = jnp.where(farmask, _dot_nt(qm, kfar) + rb_ref[N_BUCKETS - 1, h], NEG)
        sn = jnp.where(nearmask, _dot_nt(qm, knear) + fc_ref[h], NEG)
        m = jnp.maximum(jnp.max(sf, axis=1, keepdims=True), jnp.max(sn, axis=1, keepdims=True))
        pf = jnp.where(sf > 0.5 * NEG, jnp.exp(sf - m), 0.0)
        pn = jnp.where(sn > 0.5 * NEG, jnp.exp(sn - m), 0.0)
        l = jnp.sum(pf, axis=1, keepdims=True) + jnp.sum(pn, axis=1, keepdims=True)
        inv = 1.0 / jnp.maximum(l, TINY)
        pf = pf * inv
        pn = pn * inv
        o = _dot(pf.astype(BF16), vfar) + _dot(pn.astype(BF16), vnear)
        oc = jnp.where(lanehead == h, o, oc)
        ps_far[g] = ps_far[g] + pf
        ps_near[g] = ps_near[g] + pn
    oc_ref[0] = oc

    mov_far = jnp.where(_overlap(_iota((n, LANES), 0), _iota((n, LANES), 1)), 1.0, 0.0).astype(BF16)
    mov_near = jnp.where(_overlap(_iota((LANES, LANES), 0) + i0, _iota((LANES, LANES), 1)), 1.0, 0.0).astype(BF16)
    qpos = qs + _iota((tq, LANES), 0)
    jb = _iota((tq, LANES), 1)
    qblk = _div(qpos, SEL_LEN)
    valid = jb * SEL_LEN <= qpos
    forced = (jb == 0) | (jb == qblk) | (jb == qblk - 1)
    for g in range(NSA_KV):
        p_slc = _dot2_l(ps_far[g], mov_far) + _dot2_l(ps_near[g], mov_near)
        score = jnp.where(forced, FORCE_SCORE, jnp.where(valid, p_slc, -1.0))
        sel = _topk_mask(score, k_eff)
        mq_ref[0, g] = jnp.where((sel > 0.5) & (score >= 0.0), 0.0, NEG).astype(BF16)


def _nsa_cmp(rb, qn, kcw):
    B, S, _ = qn.shape
    tq = min(256, S)
    npad = kcw.shape[2]
    return pl.pallas_call(
        _nsa_cmp_kernel,
        grid=(B, S // tq),
        in_specs=[pl.BlockSpec(memory_space=pltpu.SMEM),
                  pl.BlockSpec((1, tq, NSA_W), lambda b, i: (b, i, 0)),
                  pl.BlockSpec((1, 2, npad, NSA_W), lambda b, i: (b, 0, 0, 0))],
        out_specs=[pl.BlockSpec((1, tq, NSA_W), lambda b, i: (b, i, 0)),
                   pl.BlockSpec((1, NSA_KV, tq, LANES), lambda b, i: (b, 0, i, 0))],
        out_shape=[jax.ShapeDtypeStruct((B, S, NSA_W), F32),
                   jax.ShapeDtypeStruct((B, NSA_KV, S, LANES), BF16)],
        scratch_shapes=[pltpu.VMEM((H_NSA, tq, LANES), F32)],
        compiler_params=_cparams(("arbitrary", "arbitrary")),
        name="nsa_cmp",
    )(rb, qn, kcw)


def _stack_heads(q, g_rows, extra, qs_ref):
    tq = q.shape[0]
    lane = _iota((tq, LANES), 1)
    for hh in range(NSA_GH):
        a = (hh // 2) * LANES
        t = q[:, a:a + LANES]
        if hh % 2:
            t = pltpu.roll(t, HD, 1)
        left = jnp.where(lane < HD, t * SCALE, g_rows[hh:hh + 1, :]).astype(BF16)
        if extra is None:
            qs_ref[hh * tq:(hh + 1) * tq, :] = left
        else:
            qs_ref[hh * tq:(hh + 1) * tq, :] = jnp.concatenate([left, extra], axis=1)


def _unstack_heads(a, tq):
    lane = _iota((tq, LANES), 1)
    p0 = jnp.where(lane < HD, a[0:tq], a[tq:2 * tq])
    p1 = jnp.where(lane < HD, a[2 * tq:3 * tq], a[3 * tq:4 * tq])
    return jnp.concatenate([p0, p1], axis=1)


SEL_BACK = 128


def _nsa_sel_kernel(rb_ref, q_ref, mq_ref, brow_ref, ksa_ref, vsd_ref, o_ref, dn_ref, qs_ref, m_ref, l_ref, acc_ref):
    b = pl.program_id(0)
    g = pl.program_id(1)
    i = pl.program_id(2)
    tq = q_ref.shape[1]
    wn = tq + SEL_BACK

    @pl.when((b == 0) & (g == 0) & (i == 0))
    def _():
        dist = _iota((tq, wn), 0) + SEL_BACK - _iota((tq, wn), 1)
        bk = _bucket(jnp.clip(dist, 0, MAX_DISTANCE - 1))
        for h in range(H_NSA):
            far = rb_ref[N_BUCKETS - 1, h]
            val = _bias_lookup(bk, lambda bb, h=h: rb_ref[bb, h]) - far
            dn_ref[h // NSA_GH, (h % NSA_GH) * tq:(h % NSA_GH + 1) * tq, :] = jnp.where(dist >= 0, val, NEG)

    qs = i * tq
    _stack_heads(q_ref[0], brow_ref[0], mq_ref[0, 0], qs_ref)
    qq = qs_ref[...]
    _softmax_init(m_ref, l_ref, acc_ref)

    @pl.when(i == 0)
    def _():
        s = _dot_nt(qq, ksa_ref[0, 0, 0:tq, :]) + dn_ref[g][:, SEL_BACK:]
        _softmax_update(s, vsd_ref[0, 0, 0:tq, :], m_ref, l_ref, acc_ref)

    @pl.when(i > 0)
    def _():
        st = pl.multiple_of(qs - SEL_BACK, LANES)
        s = _dot_nt(qq, ksa_ref[0, 0, pl.ds(st, wn), :]) + dn_ref[g]
        _softmax_update(s, vsd_ref[0, 0, pl.ds(st, wn), :], m_ref, l_ref, acc_ref)
        st2 = pl.multiple_of(qs - tq, LANES)
        s2 = _dot_nt(qq, ksa_ref[0, 0, pl.ds(st2, tq - SEL_BACK), :])
        _softmax_update(s2, vsd_ref[0, 0, pl.ds(st2, tq - SEL_BACK), :], m_ref, l_ref, acc_ref)

    def far_tile(kt, c):
        ks = pl.multiple_of(kt * tq, tq)
        s = _dot_nt(qq, ksa_ref[0, 0, pl.ds(ks, tq), :])
        _softmax_update(s, vsd_ref[0, 0, pl.ds(ks, tq), :], m_ref, l_ref, acc_ref)
        return c

    lax.fori_loop(0, jnp.maximum(i - 1, 0), far_tile, 0)
    a = acc_ref[...] / jnp.maximum(l_ref[...], TINY)
    o_ref[0] = _unstack_heads(a, tq)


def _nsa_sel(rb, qn, maskq, brow, ksa, vsd):
    B, S, _ = qn.shape
    tq = min(256, S)
    G = NSA_KV
    return pl.pallas_call(
        _nsa_sel_kernel,
        grid=(B, G, S // tq),
        in_specs=[pl.BlockSpec(memory_space=pltpu.SMEM),
                  pl.BlockSpec((1, tq, 2 * LANES), lambda b, g, i: (b, i, g)),
                  pl.BlockSpec((1, 1, tq, LANES), lambda b, g, i: (b, g, i, 0)),
                  pl.BlockSpec((1, NSA_GH, LANES), lambda b, g, i: (g, 0, 0)),
                  pl.BlockSpec((1, 1, S, 2 * LANES), lambda b, g, i: (b, g, 0, 0)),
                  pl.BlockSpec((1, 1, S, LANES), lambda b, g, i: (b, g, 0, 0))],
        out_specs=pl.BlockSpec((1, tq, 2 * LANES), lambda b, g, i: (b, i, g)),
        out_shape=jax.ShapeDtypeStruct((B, S, NSA_W), F32),
        scratch_shapes=[pltpu.VMEM((G, NSA_GH * tq, tq + SEL_BACK), F32),
                        pltpu.VMEM((NSA_GH * tq, 2 * LANES), BF16),
                        pltpu.VMEM((NSA_GH * tq, 1), F32), pltpu.VMEM((NSA_GH * tq, 1), F32),
                        pltpu.VMEM((NSA_GH * tq, LANES), F32)],
        compiler_params=_cparams(("arbitrary", "arbitrary", "arbitrary")),
        name="nsa_sel",
    )(rb, qn, maskq, brow, ksa, vsd)


def _nsa_win_kernel(rb_ref, q_ref, kwa_ref, vwd_ref, o_ref, dw_ref, qs_ref):
    b = pl.program_id(0)
    g = pl.program_id(1)
    i = pl.program_id(2)
    tq = q_ref.shape[1]
    wk = tq + WINDOW

    @pl.when((b == 0) & (g == 0) & (i == 0))
    def _():
        dist = _iota((tq, wk), 0) + WINDOW - _iota((tq, wk), 1)
        bk = _bucket(jnp.clip(dist, 0, MAX_DISTANCE - 1))
        ok = (dist >= 0) & (dist < WINDOW)
        for h in range(H_NSA):
            val = _bias_lookup(bk, lambda bb, h=h: rb_ref[bb, h])
            dw_ref[h // NSA_GH, (h % NSA_GH) * tq:(h % NSA_GH + 1) * tq, :] = jnp.where(ok, val, NEG)

    qs = pl.multiple_of(i * tq, tq)
    _stack_heads(q_ref[0], jnp.zeros((NSA_GH, LANES), F32), None, qs_ref)
    s = _dot_nt(qs_ref[...], kwa_ref[0, 0, pl.ds(qs, wk), :]) + dw_ref[g]
    s = jnp.where(_iota(s.shape, 1) + qs >= WINDOW, s, NEG)
    m = jnp.max(s, axis=1, keepdims=True)
    p = jnp.exp(s - m)
    l = jnp.sum(p, axis=1, keepdims=True)
    a = _dot(p.astype(BF16), vwd_ref[0, 0, pl.ds(qs, wk), :]) / jnp.maximum(l, TINY)
    o_ref[0] = _unstack_heads(a, tq)


def _nsa_win(rb, qn, kwa_p, vwd_p):
    B, S, _ = qn.shape
    tq = min(256, S)
    G = NSA_KV
    sp = kwa_p.shape[2]
    return pl.pallas_call(
        _nsa_win_kernel,
        grid=(B, G, S // tq),
        in_specs=[pl.BlockSpec(memory_space=pltpu.SMEM),
                  pl.BlockSpec((1, tq, 2 * LANES), lambda b, g, i: (b, i, g)),
                  pl.BlockSpec((1, 1, sp, LANES), lambda b, g, i: (b, g, 0, 0)),
                  pl.BlockSpec((1, 1, sp, LANES), lambda b, g, i: (b, g, 0, 0))],
        out_specs=pl.BlockSpec((1, tq, 2 * LANES), lambda b, g, i: (b, i, g)),
        out_shape=jax.ShapeDtypeStruct((B, S, NSA_W), F32),
        scratch_shapes=[pltpu.VMEM((G, NSA_GH * tq, tq + WINDOW), F32),
                        pltpu.VMEM((NSA_GH * tq, LANES), BF16)],
        compiler_params=_cparams(("arbitrary", "arbitrary", "arbitrary")),
        name="nsa_win",
    )(rb, qn, kwa_p, vwd_p)


def _merge_kernel(x_ref, of_ref, oc_ref, os_ref, ow_ref, misc_ref, g_ref, w_ref, out_ref):
    tm = x_ref.shape[0]
    hi, lo = _split2(misc_ref[...])
    er = _iota((LANES, NSA_W), 0)
    ec = _iota((LANES, NSA_W), 1)
    onsa = jnp.zeros((tm, NSA_W), F32)
    for k, o_ref in enumerate((oc_ref, os_ref, ow_ref)):
        e = jnp.where(er == MISC_GATE + k * H_NSA + _div(ec, HD), 1.0, 0.0).astype(BF16)
        onsa = onsa + (_dot(hi, e) + _dot(lo, e)) * o_ref[...]
    g = g_ref[...]
    a = _rms(of_ref[...], g[:, :FOX_W]).astype(BF16)
    c = _rms(onsa, g[:, FOX_W:]).astype(BF16)
    out_ref[...] = x_ref[...] + _dot(a, w_ref[0:FOX_W, :]) + _dot(c, w_ref[FOX_W:, :])


def _merge(x2, ofox, oc, os_, ow, misc, g, w):
    T, D = x2.shape
    tm = min(512, T)
    row = lambda wd: pl.BlockSpec((tm, wd), lambda i: (i, 0))
    return pl.pallas_call(
        _merge_kernel,
        grid=(T // tm,),
        in_specs=[row(D), row(FOX_W), row(NSA_W), row(NSA_W), row(NSA_W), row(LANES),
                  pl.BlockSpec((1, D), lambda i: (0, 0)),
                  pl.BlockSpec((D, D), lambda i: (0, 0))],
        out_specs=row(D),
        out_shape=jax.ShapeDtypeStruct((T, D), F32),
        compiler_params=_cparams(("arbitrary",)),
        name="merge",
    )(x2, ofox, oc, os_, ow, misc, g, w)


FF_CHUNK = 256


def _ffn_kernel(*refs, seq_len, short_len, final):
    if seq_len is None:
        (x_ref, g2_ref, wg_ref, wu_ref, cw_ref, cb_ref, wd_ref, gf_ref, hm1_ref, hm2_ref,
         out_ref, gt_ref, h_sc, acc_sc) = refs
    else:
        (x_ref, g2_ref, wg_ref, wu_ref, cw_ref, cb_ref, wd_ref, gf_ref,
         out_ref, gt_ref, h_sc, acc_sc, carry_sc) = refs
    i = pl.program_id(0)
    j = pl.program_id(1)
    tm = x_ref.shape[0]

    @pl.when(j == 0)
    def _():
        h_sc[...] = _rms(x_ref[...], g2_ref[...]).astype(BF16)
        acc_sc[...] = jnp.zeros_like(acc_sc)

    h = h_sc[...]
    gch = _dot(h, wg_ref[...])
    u = _dot(h, wu_ref[...])
    r1 = pltpu.roll(gch, 1, 0)
    r2 = pltpu.roll(gch, 2, 0)
    row = _iota(gch.shape, 0)
    if seq_len is None:
        t = _mod(row, short_len)
        m1 = jnp.where(t == 0, hm1_ref[...], r1)
        m2 = jnp.where(t < 2, hm2_ref[...], r2)
        gt_ref[...] = gch
    else:
        first = (i % (seq_len // tm)) == 0
        c = jnp.where(first, 0.0, carry_sc[j])
        m1 = jnp.where(row == 0, c[1:2, :], r1)
        m2 = jnp.where(row == 0, c[0:1, :], jnp.where(row == 1, c[1:2, :], r2))
        carry_sc[j, 0:2, :] = gch[tm - 2:tm, :]
        gt_ref[0] = gch[tm - 8:tm, :]
    cw = cw_ref[...]
    gc = cb_ref[...] + cw[0:1, :] * m2 + cw[1:2, :] * m1 + cw[2:3, :] * gch
    act = gc * (1.0 / (1.0 + jnp.exp(-gc)))
    acc_sc[...] += _dot((act * u).astype(BF16), wd_ref[...])

    @pl.when(j == pl.num_programs(1) - 1)
    def _():
        y = x_ref[...] + acc_sc[...]
        if final:
            y = _rms(y, gf_ref[...])
        out_ref[...] = y


def _ffn(x2, g2, wgu, cw, cb, wd, gf, *, seq_len, final, hist=None, short_len=None):
    T, D = x2.shape
    dff = wd.shape[0]
    fc = FF_CHUNK
    nff = dff // fc
    tm = min(1024, T) if seq_len is not None else T
    nt = T // tm
    in_specs = [pl.BlockSpec((tm, D), lambda i, j: (i, 0)),
                pl.BlockSpec((1, D), lambda i, j: (0, 0)),
                pl.BlockSpec((D, fc), lambda i, j: (0, j)),
                pl.BlockSpec((D, fc), lambda i, j: (0, nff + j)),
                pl.BlockSpec((CONV_W, fc), lambda i, j: (0, j)),
                pl.BlockSpec((1, fc), lambda i, j: (0, j)),
                pl.BlockSpec((fc, D), lambda i, j: (j, 0)),
                pl.BlockSpec((1, D), lambda i, j: (0, 0))]
    args = [x2, g2, wgu, wgu, cw, cb, wd, gf]
    scratch = [pltpu.VMEM((tm, D), BF16), pltpu.VMEM((tm, D), F32)]
    if seq_len is None:
        in_specs += [pl.BlockSpec((tm, fc), lambda i, j: (i, j)), pl.BlockSpec((tm, fc), lambda i, j: (i, j))]
        args += list(hist)
        gt_spec = pl.BlockSpec((tm, fc), lambda i, j: (i, j))
        gt_shape = jax.ShapeDtypeStruct((T, dff), F32)
    else:
        scratch.append(pltpu.VMEM((nff, 8, fc), F32))
        gt_spec = pl.BlockSpec((1, 8, fc), lambda i, j: (i, 0, j))
        gt_shape = jax.ShapeDtypeStruct((nt, 8, dff), F32)
    return pl.pallas_call(
        functools.partial(_ffn_kernel, seq_len=seq_len, short_len=short_len, final=final),
        grid=(nt, nff),
        in_specs=in_specs,
        out_specs=[pl.BlockSpec((tm, D), lambda i, j: (i, 0)), gt_spec],
        out_shape=[jax.ShapeDtypeStruct((T, D), F32), gt_shape],
        scratch_shapes=scratch,
        compiler_params=_cparams(("arbitrary", "arbitrary")),
        name="ffn",
    )(*args)


def _rows_th(q):
    tq = q.shape[0]
    rows = jnp.concatenate([jnp.broadcast_to(q[t:t + 1, :], (8, q.shape[1])) for t in range(tq)], axis=0)
    keep = _div(_iota(rows.shape, 1), HD) == _mod(_iota(rows.shape, 0), 8)
    return jnp.where(keep, rows * SCALE, 0.0).astype(BF16)


def _diag_rows(o_ref, o32):
    keep = _div(_iota(o32.shape, 1), HD) == _mod(_iota(o32.shape, 0), 8)
    od = jnp.where(keep, o32, 0.0)
    for t in range(o32.shape[0] // 8):
        o_ref[0, t:t + 1, :] = jnp.sum(od[t * 8:(t + 1) * 8, :], axis=0, keepdims=True)


def _fox_sample_kernel(pt_tab, *refs):
    P = PAGES_PER_STEP
    q_ref, kvn_ref, lfn_ref = refs[0:3]
    kv_refs = refs[3:3 + P]
    lf_refs = refs[3 + P:3 + 2 * P]
    o_ref, q_sc, m_ref, l_ref, acc_ref, carry_ref, new_sc = refs[3 + 2 * P:]
    s = pl.program_id(1)
    tq = q_ref.shape[1]
    nr = tq * 8

    @pl.when(s == 0)
    def _():
        q_sc[...] = _rows_th(q_ref[0])
        _softmax_init(m_ref, l_ref, acc_ref)
        carry_ref[...] = jnp.zeros_like(carry_ref)
        new_sc[...] = jnp.zeros_like(new_sc)
        new_sc[0:tq, :] = kvn_ref[0]

    qq = q_sc[...]
    triu = (_iota((PAGE, PAGE), 0) <= _iota((PAGE, PAGE), 1)).astype(BF16)

    def attend(kvs, lfs, extra_mask):
        off = carry_ref[...]
        sc = []
        for kv, lf in zip(kvs, lfs):
            cum = _dot3_l(lf, triu) + off
            off = jnp.broadcast_to(cum[:, PAGE - 1:PAGE], cum.shape)
            sc.append(_dot_nt(qq, kv[:, :FOX_W].astype(BF16)) - jnp.concatenate([cum] * tq, axis=0))
        carry_ref[...] = off
        sc = jnp.concatenate(sc, axis=1)
        if extra_mask is not None:
            sc = jnp.where(extra_mask, sc, NEG)
        m_old = m_ref[...]
        m_new = jnp.maximum(m_old, jnp.max(sc, axis=1, keepdims=True))
        alpha = jnp.exp(m_old - m_new)
        p = jnp.exp(sc - m_new)
        l_ref[...] = alpha * l_ref[...] + jnp.sum(p, axis=1, keepdims=True)
        acc = alpha * acc_ref[...]
        for k, kv in enumerate(kvs):
            acc = acc + _dot(p[:, k * PAGE:(k + 1) * PAGE].astype(BF16), kv[:, FOX_W:].astype(BF16))
        acc_ref[...] = acc
        m_ref[...] = m_new

    attend([r[...] for r in kv_refs], [r[...] for r in lf_refs], None)

    @pl.when(s == pl.num_programs(1) - 1)
    def _():
        key = _iota((nr, PAGE), 1)
        ok = (key < tq) & (key <= _div(_iota((nr, PAGE), 0), 8))
        attend([new_sc[...]], [lfn_ref[0]], ok)
        _diag_rows(o_ref, acc_ref[...] / jnp.maximum(l_ref[...], TINY))


def _fox_sample(qf, fkv_new, lfn_t, cache_kv, cache_lft, page_table, layer):
    nb, npages = page_table.shape
    tq = qf.shape[1]
    P = PAGES_PER_STEP
    kvw = 2 * FOX_W
    seq = lambda b, s, pt: (b, 0, 0)

    def kv_spec(k):
        return pl.BlockSpec((None, None, PAGE, kvw), lambda b, s, pt: (layer, pt[b, s * P + k], 0, 0))

    def lf_spec(k):
        return pl.BlockSpec((None, None, H_FOX, PAGE), lambda b, s, pt: (layer, pt[b, s * P + k], 0, 0))

    return pl.pallas_call(
        _fox_sample_kernel,
        grid_spec=pltpu.PrefetchScalarGridSpec(
            num_scalar_prefetch=1,
            grid=(nb, npages // P),
            in_specs=[pl.BlockSpec((1, tq, FOX_W), seq), pl.BlockSpec((1, tq, kvw), seq),
                      pl.BlockSpec((1, H_FOX, PAGE), seq)]
            + [kv_spec(k) for k in range(P)] + [lf_spec(k) for k in range(P)],
            out_specs=pl.BlockSpec((1, tq, FOX_W), seq),
            scratch_shapes=[pltpu.VMEM((tq * 8, FOX_W), BF16),
                            pltpu.VMEM((tq * 8, 1), F32), pltpu.VMEM((tq * 8, 1), F32),
                            pltpu.VMEM((tq * 8, FOX_W), F32),
                            pltpu.VMEM((H_FOX, PAGE), F32),
                            pltpu.VMEM((PAGE, kvw), F32)]),
        out_shape=jax.ShapeDtypeStruct((nb, tq, FOX_W), F32),
        compiler_params=_cparams(("arbitrary", "arbitrary")),
        name="fox_sample",
    )(page_table, qf, fkv_new, lfn_t, *([cache_kv] * P), *([cache_lft] * P))


def _place_wide():
    r = _iota((LANES, NSA_W), 0)
    c = _iota((LANES, NSA_W), 1)
    return jnp.where(r == _div(c, NSA_GH * HD) * HD + _mod(c, HD), 1.0, 0.0).astype(BF16)


def _rb_col(rbt_ref):
    return lambda bb: rbt_ref[:, bb:bb + 1]


def _ns_attend_kernel(q_ref, kcw_ref, win_ref, wn_ref, rbt_ref, oc_ref, ow_ref, mb_ref, new_sc, *, past):
    tq = q_ref.shape[1]
    nr = tq * 8
    n = kcw_ref.shape[2] - CMP_PAD_FRONT - CMP_PAD_BACK
    n_selp = past // SEL_LEN
    wb = win_ref.shape[0]
    qq = _rows_th(q_ref[0])
    kc = kcw_ref[0, 0, CMP_PAD_FRONT:CMP_PAD_FRONT + n, :]
    vc = kcw_ref[0, 1, CMP_PAD_FRONT:CMP_PAD_FRONT + n, :]
    trow = _div(_iota((nr, n), 0), 8)
    dist = past + trow - CMP_STRIDE * _iota((nr, n), 1) - (CMP_LEN - 1)
    bias = _bias_lookup(_bucket(jnp.clip(dist, 0, MAX_DISTANCE - 1)), _rb_col(rbt_ref))
    ok = dist >= 0
    s = jnp.where(ok, _dot_nt(qq, kc) + bias, NEG)
    m = jnp.max(s, axis=1, keepdims=True)
    p = jnp.where(ok, jnp.exp(s - m), 0.0)
    p = p / jnp.maximum(jnp.sum(p, axis=1, keepdims=True), TINY)
    _diag_rows(oc_ref, _dot(p.astype(BF16), vc))
    mov = jnp.where(_overlap(_iota((n, LANES), 0), _iota((n, LANES), 1)), 1.0, 0.0).astype(BF16)
    x = _dot2_l(p, mov)
    z = x + pltpu.roll(x, nr - 1, 0) + pltpu.roll(x, nr - 2, 0) + pltpu.roll(x, nr - 3, 0)
    z0 = jnp.where(_mod(_iota(z.shape, 0), NSA_GH) == 0, z, 0.0)
    p_slc = z0 + pltpu.roll(z0, 1, 0) + pltpu.roll(z0, 2, 0) + pltpu.roll(z0, 3, 0)
    jb = _iota((nr, LANES), 1)
    forced = (jb == 0) | (jb == n_selp - 1)
    score = jnp.where(forced, FORCE_SCORE, jnp.where(jb < n_selp, p_slc, -1.0))
    k_past = min(N_SELECT, n_selp + 1) - 1
    sel = _topk_mask(score, k_past)
    mb_ref[0] = jnp.where((sel > 0.5) & (score >= 0.0), 0.0, NEG).astype(BF16)
    place = _place_wide()
    w = win_ref[...]
    kw = _dot(w[:, 0:LANES].astype(BF16), place).astype(BF16)
    vw = _dot(w[:, LANES:2 * LANES].astype(BF16), place).astype(BF16)
    new_sc[...] = jnp.zeros_like(new_sc)
    new_sc[0:tq, :] = wn_ref[0]
    wnew = new_sc[...]
    kn = _dot(wnew[:, 0:LANES].astype(BF16), place).astype(BF16)
    vn = _dot(wnew[:, LANES:2 * LANES].astype(BF16), place).astype(BF16)
    dw = _div(_iota((nr, wb), 0), 8) + wb - _iota((nr, wb), 1)
    okw = (dw >= 0) & (dw < WINDOW)
    sw = _dot_nt(qq, kw) + _bias_lookup(_bucket(jnp.clip(dw, 0, MAX_DISTANCE - 1)), _rb_col(rbt_ref))
    sw = jnp.where(okw, sw, NEG)
    dn = _div(_iota((nr, PAGE), 0), 8) - _iota((nr, PAGE), 1)
    okn = (dn >= 0) & (_iota((nr, PAGE), 1) < tq)
    sn = _dot_nt(qq, kn) + _bias_lookup(_bucket(jnp.clip(dn, 0, MAX_DISTANCE - 1)), _rb_col(rbt_ref))
    sn = jnp.where(okn, sn, NEG)
    mw = jnp.maximum(jnp.max(sw, axis=1, keepdims=True), jnp.max(sn, axis=1, keepdims=True))
    pw = jnp.where(okw, jnp.exp(sw - mw), 0.0)
    pn = jnp.where(okn, jnp.exp(sn - mw), 0.0)
    lw = jnp.maximum(jnp.sum(pw, axis=1, keepdims=True) + jnp.sum(pn, axis=1, keepdims=True), TINY)
    _diag_rows(ow_ref, (_dot(pw.astype(BF16), vw) + _dot(pn.astype(BF16), vn)) / lw)


def _ns_attend(qn, kcw, state_win, wkv_new, rbt, layer, past):
    nb, tq, _ = qn.shape
    npad = kcw.shape[2]
    wb = state_win.shape[2]
    seq = lambda b: (b, 0, 0)
    return pl.pallas_call(
        functools.partial(_ns_attend_kernel, past=past),
        grid=(nb,),
        in_specs=[pl.BlockSpec((1, tq, NSA_W), seq),
                  pl.BlockSpec((1, 2, npad, NSA_W), lambda b: (b, 0, 0, 0)),
                  pl.BlockSpec((None, None, wb, 2 * LANES), lambda b: (layer, b, 0, 0)),
                  pl.BlockSpec((1, tq, 2 * LANES), seq),
                  pl.BlockSpec((tq * 8, N_BUCKETS), lambda b: (0, 0))],
        out_specs=[pl.BlockSpec((1, tq, NSA_W), seq), pl.BlockSpec((1, tq, NSA_W), seq),
                   pl.BlockSpec((1, tq * 8, LANES), seq)],
        out_shape=[jax.ShapeDtypeStruct((nb, tq, NSA_W), F32), jax.ShapeDtypeStruct((nb, tq, NSA_W), F32),
                   jax.ShapeDtypeStruct((nb, tq * 8, LANES), BF16)],
        scratch_shapes=[pltpu.VMEM((PAGE, 2 * LANES), F32)],
        compiler_params=_cparams(("arbitrary",)),
        name="ns_attend",
    )(qn, kcw, state_win, wkv_new, rbt)


def _ns_select_kernel(pt_tab, *refs, past):
    P = PAGES_PER_STEP
    q_ref, mb_ref, nn_ref, rbt_ref = refs[0:4]
    pg_refs = refs[4:4 + P]
    o_ref, q_sc, m_ref, l_ref, acc_ref, new_sc = refs[4 + P:]
    s = pl.program_id(1)
    tq = q_ref.shape[1]
    nr = tq * 8

    @pl.when(s == 0)
    def _():
        r = _iota((NSA_W, LANES), 0)
        c = _iota((NSA_W, LANES), 1)
        fold = jnp.where(c == _div(r, NSA_GH * HD) * HD + _mod(r, HD), 1.0, 0.0).astype(BF16)
        q_sc[...] = _dot(_rows_th(q_ref[0]), fold).astype(BF16)
        _softmax_init(m_ref, l_ref, acc_ref)
        new_sc[...] = jnp.zeros_like(new_sc)
        new_sc[0:tq, :] = nn_ref[0][:, 2 * LANES:]

    qq = q_sc[...]
    trow = _div(_iota((nr, PAGE), 0), 8)
    key = _iota((nr, PAGE), 1)

    def attend(tiles, biases):
        sc = jnp.concatenate([_dot_nt(qq, x[:, 0:LANES].astype(BF16)) for x in tiles], axis=1) + biases
        m_old = m_ref[...]
        m_new = jnp.maximum(m_old, jnp.max(sc, axis=1, keepdims=True))
        alpha = jnp.exp(m_old - m_new)
        p = jnp.exp(sc - m_new)
        l_ref[...] = alpha * l_ref[...] + jnp.sum(p, axis=1, keepdims=True)
        acc = alpha * acc_ref[...]
        for k, x in enumerate(tiles):
            acc = acc + _dot(p[:, k * PAGE:(k + 1) * PAGE].astype(BF16), x[:, LANES:2 * LANES].astype(BF16))
        acc_ref[...] = acc
        m_ref[...] = m_new

    blk = _iota((LANES, P * PAGE), 0)
    kcol = _iota((LANES, P * PAGE), 1)
    expand = jnp.where(blk == _div(s * (P * PAGE) + kcol, SEL_LEN), 1.0, 0.0).astype(BF16)
    bias = _dot(mb_ref[0], expand) + rbt_ref[:, N_BUCKETS - 1:N_BUCKETS]
    tiles = [r[...] for r in pg_refs]

    @pl.when(s < pl.num_programs(1) - 1)
    def _():
        attend(tiles, bias)

    @pl.when(s == pl.num_programs(1) - 1)
    def _():
        d_last = past + trow - (past - PAGE + key)
        b_last = _bias_lookup(_bucket(jnp.clip(d_last, 0, MAX_DISTANCE - 1)), _rb_col(rbt_ref))
        fix = jnp.concatenate([jnp.zeros((nr, (P - 1) * PAGE), F32),
                               b_last - rbt_ref[:, N_BUCKETS - 1:N_BUCKETS]], axis=1)
        attend(tiles, bias + fix)
        d_new = trow - key
        b_new = _bias_lookup(_bucket(jnp.clip(d_new, 0, MAX_DISTANCE - 1)), _rb_col(rbt_ref))
        attend([new_sc[...]], jnp.where((d_new >= 0) & (key < tq), b_new, NEG))
        a = acc_ref[...] / jnp.maximum(l_ref[...], TINY)
        place = _place_wide()
        _diag_rows(o_ref, _dot2_l(a, place))


def _ns_select(qn, maskb, nkv_new, rbt, cache_nsa, page_table, layer, past):
    nb, npages = page_table.shape
    tq = qn.shape[1]
    P = PAGES_PER_STEP
    seq = lambda b, s, pt: (b, 0, 0)

    def page_spec(k):
        return pl.BlockSpec((None, None, PAGE, 2 * LANES), lambda b, s, pt: (layer, pt[b, s * P + k], 0, 1))

    return pl.pallas_call(
        functools.partial(_ns_select_kernel, past=past),
        grid_spec=pltpu.PrefetchScalarGridSpec(
            num_scalar_prefetch=1,
            grid=(nb, npages // P),
            in_specs=[pl.BlockSpec((1, tq, NSA_W), seq), pl.BlockSpec((1, tq * 8, LANES), seq),
                      pl.BlockSpec((1, tq, 4 * LANES), seq),
                      pl.BlockSpec((tq * 8, N_BUCKETS), lambda b, s, pt: (0, 0))]
            + [page_spec(k) for k in range(P)],
            out_specs=pl.BlockSpec((1, tq, NSA_W), seq),
            scratch_shapes=[pltpu.VMEM((tq * 8, LANES), BF16),
                            pltpu.VMEM((tq * 8, 1), F32), pltpu.VMEM((tq * 8, 1), F32),
                            pltpu.VMEM((tq * 8, LANES), F32),
                            pltpu.VMEM((PAGE, 2 * LANES), F32)]),
        out_shape=jax.ShapeDtypeStruct((nb, tq, NSA_W), F32),
        compiler_params=_cparams(("arbitrary", "arbitrary")),
        name="ns_select",
    )(page_table, qn, maskb, nkv_new, rbt, *([cache_nsa] * P))


def _prep_w_in(w):
    d = w.shape[0]
    o_logf = 3 * FOX_W
    o_qn = o_logf + H_FOX
    o_kv = o_qn + NSA_W
    o_gate = o_kv + 6 * NSA_KV * HD
    misc = jnp.concatenate([w[:, o_logf:o_qn], w[:, o_gate:o_gate + 3 * H_NSA],
                            jnp.zeros((d, LANES - H_FOX - 3 * H_NSA), w.dtype)], axis=1)
    return jnp.concatenate([w[:, :o_logf], w[:, o_qn:o_kv], w[:, o_kv:o_gate], misc], axis=1).astype(BF16)


def _prep_cmp(pos, w1, b1, w2):
    eye = jnp.eye(NSA_KV, dtype=w1.dtype)
    w1r = w1.reshape(2, 2, CMP_STRIDE, HD, CMP_HID)
    wide = jnp.einsum("whjdc,ab->whjadbc", w1r, eye).reshape(2, 2, CMP_STRIDE * LANES, 2 * CMP_HID)
    posr = jnp.broadcast_to(pos.reshape(2, 2, CMP_STRIDE, 1, HD), (2, 2, CMP_STRIDE, NSA_KV, HD))
    posr = posr.reshape(2, 2, 1, CMP_STRIDE * LANES)
    group_of_head = (jnp.arange(H_NSA) // NSA_GH)[None, :] == jnp.arange(NSA_KV)[:, None]
    w2w = jnp.einsum("wcd,ah->wachd", w2, group_of_head.astype(w2.dtype)).reshape(2, 2 * CMP_HID, NSA_W)
    return {"pt": posr[:, 0], "pb": posr[:, 1],
            "w1t": wide[:, 0].astype(BF16), "w1b": wide[:, 1].astype(BF16),
            "b1": jnp.concatenate([b1, b1], axis=-1)[:, None, :],
            "w2w": w2w.astype(BF16)}


def _far_bias_rows(rel_bias):
    far = rel_bias[N_BUCKETS - 1]
    hi = far.astype(BF16).astype(F32)
    r = far - hi
    mid = r.astype(BF16).astype(F32)
    lo = (r - mid).astype(BF16).astype(F32)
    rows = jnp.zeros((H_NSA, LANES), F32)
    rows = rows.at[:, HD].set(hi).at[:, HD + 1].set(mid).at[:, HD + 2].set(lo)
    return rows.reshape(NSA_KV, NSA_GH, LANES)


def kernel(x_prompt, x_sample, cache_fox_kv, cache_fox_logf, cache_nsa_kv, state_win_kv, state_conv,
           page_table, norm1_g, w_in, b_forget, cmp_pos, cmp_w1, cmp_b1, cmp_w2, out_norm_g, w_out,
           norm2_g, w_gu, conv_w, conv_b, w_down, rel_bias, final_norm_g):
    B, S, D = x_prompt.shape
    nb, tq, _ = x_sample.shape
    depth = w_in.shape[0]
    n_pool = cache_fox_kv.shape[1]
    npages = page_table.shape[1]
    past = npages * PAGE
    dff = w_down.shape[1]
    wb = state_win_kv.shape[2]
    assert tq & (tq - 1) == 0 and tq >= CONV_W - 1
    assert S % 256 == 0 and S // SEL_LEN <= LANES and past // SEL_LEN <= LANES
    assert npages % PAGES_PER_STEP == 0 and wb == WINDOW and past >= WINDOW and tq * 8 <= LANES

    cache_kv = cache_fox_kv.reshape(depth, n_pool, PAGE, 2 * FOX_W)
    cache_lft = jnp.swapaxes(cache_fox_logf, 2, 3)
    cache_nsa = cache_nsa_kv.reshape(depth, n_pool, PAGE, 4 * NSA_KV * HD)
    state_win = state_win_kv.reshape(depth, nb, wb, 2 * NSA_KV * HD)
    rbt = jnp.tile(rel_bias.T, (tq, 1))
    brow = _far_bias_rows(rel_bias)
    gf = final_norm_g.reshape(1, D)

    xp = x_prompt.reshape(B * S, D)
    xs = x_sample.reshape(nb * tq, D)
    outs = [[] for _ in range(10)]
    for l in range(depth):
        w_l = _prep_w_in(w_in[l])
        bf = jnp.zeros((1, LANES), F32).at[0, :H_FOX].set(b_forget[l])
        cw = _prep_cmp(cmp_pos[l], cmp_w1[l], cmp_b1[l], cmp_w2[l])
        g1 = norm1_g[l].reshape(1, D)
        g2 = norm2_g[l].reshape(1, D)
        go = out_norm_g[l].reshape(1, D)
        wo = w_out[l].astype(BF16)
        wgu = w_gu[l].astype(BF16)
        wd = w_down[l].astype(BF16)
        cb = conv_b[l].reshape(1, dff)
        final = l == depth - 1

        qf, fkv, qn, nkv, wkv, misc = _proj(xp, g1, w_l, bf)
        r3 = lambda a: a.reshape(B, S, a.shape[-1])
        qa, ka, vb = _fox_prep(r3(qf), r3(fkv), r3(misc))
        o_fox = _fox_flash(qa, ka, vb)
        kcw = _compress_prompt(r3(nkv), cw)
        ksa, vsd, kwa, vwd = _nsa_prep(r3(nkv), r3(wkv))
        o_c, maskq = _nsa_cmp(rel_bias, r3(qn), kcw)
        o_s = _nsa_sel(rel_bias, r3(qn), maskq, brow, ksa, vsd)
        padw = ((0, 0), (0, 0), (WINDOW, 0), (0, 0))
        o_w = _nsa_win(rel_bias, r3(qn), jnp.pad(kwa, padw), jnp.pad(vwd, padw))
        f2 = lambda a: a.reshape(B * S, a.shape[-1])
        xp = _merge(xp, f2(o_fox), f2(o_c), f2(o_s), f2(o_w), misc, go, wo)
        xp, gtail = _ffn(xp, g2, wgu, conv_w[l], cb, wd, gf, seq_len=S, final=final)
        tiles_per_seq = gtail.shape[0] // B
        conv_p = gtail.reshape(B, tiles_per_seq, 8, dff)[:, -1, 8 - (CONV_W - 1):, :]
        outs[0].append(fkv.reshape(B, S, 2, H_FOX, HD))
        outs[2].append(misc[:, :H_FOX].reshape(B, S, H_FOX))
        outs[4].append(nkv.reshape(B, S, 4, NSA_KV, HD))
        outs[6].append(wkv.reshape(B, S, 2, NSA_KV, HD)[:, -min(WINDOW, S):])
        outs[8].append(conv_p)

        qf, fkv, qn, nkv, wkv, misc = _proj(xs, g1, w_l, bf)
        s3 = lambda a: a.reshape(nb, tq, a.shape[-1])
        lfn = jnp.swapaxes(s3(misc)[:, :, :H_FOX], 1, 2)
        lfn = jnp.pad(lfn, ((0, 0), (0, 0), (0, PAGE - tq)))
        o_fox = _fox_sample(s3(qf), s3(fkv), lfn, cache_kv, cache_lft, page_table, l)
        kcw = _compress_sample(cache_nsa, page_table, l, cw)
        o_c, o_w, maskb = _ns_attend(s3(qn), kcw, state_win, s3(wkv), rbt, l, past)
        o_s = _ns_select(s3(qn), maskb, s3(nkv), rbt, cache_nsa, page_table, l, past)
        s2 = lambda a: a.reshape(nb * tq, a.shape[-1])
        xs = _merge(xs, s2(o_fox), s2(o_c), s2(o_s), s2(o_w), misc, go, wo)
        hist = state_conv[l]
        zero = jnp.zeros((nb, 1, dff), F32)
        hm1 = jnp.concatenate([hist[:, 1:2], zero, zero, zero][:tq], axis=1).reshape(nb * tq, dff)
        hm2 = jnp.concatenate([hist[:, 0:1], hist[:, 1:2], zero, zero][:tq], axis=1).reshape(nb * tq, dff)
        xs, gfull = _ffn(xs, g2, wgu, conv_w[l], cb, wd, gf, seq_len=None, final=final, hist=(hm1, hm2),
                         short_len=tq)
        win_new = jnp.concatenate([state_win_kv[l], wkv.reshape(nb, tq, 2, NSA_KV, HD)], axis=1)[:, -wb:]
        outs[1].append(fkv.reshape(nb, tq, 2, H_FOX, HD))
        outs[3].append(misc[:, :H_FOX].reshape(nb, tq, H_FOX))
        outs[5].append(nkv.reshape(nb, tq, 4, NSA_KV, HD))
        outs[7].append(win_new)
        outs[9].append(gfull.reshape(nb, tq, dff)[:, -(CONV_W - 1):])

    st = [jnp.stack(o) for o in outs]
    return (xp.reshape(B, S, D), xs.reshape(nb, tq, D),
            st[0], st[1], st[2], st[3], st[4], st[5], st[6], st[7], st[8], st[9])
```

```python
import functools
import math

import numpy as np
import jax
import jax.numpy as jnp
from jax import lax
from jax.experimental import pallas as pl
from jax.experimental.pallas import tpu as pltpu

F32 = jnp.float32
BF16 = jnp.bfloat16
I32 = jnp.int32

HD = 64
H_FOX = 8
H_NSA = 8
NSA_KV = 2
NSA_GH = 4
FOX_W = H_FOX * HD
NSA_W = H_NSA * HD
CMP_LEN = 32
CMP_STRIDE = 16
CMP_HID = 256
SEL_LEN = 64
N_SELECT = 16
WINDOW = 512
N_BUCKETS = 32
MAX_DISTANCE = 128
CONV_W = 3
PAGE = 128
EPS = 1e-6
NEG = -1e30
TINY = 1e-30
FORCE_SCORE = 1e4
SCALE = HD ** -0.5

LANES = 128
VMEM_LIMIT = 56 * 1024 * 1024

C_QF, C_FKV, C_QN, C_NKV, C_WKV, C_MISC, C_END = 0, 512, 1536, 2048, 2560, 2816, 2944
MISC_LOGF = 0
MISC_GATE = 8

AUG0 = HD
AUG1 = HD + 3

PAGES_PER_STEP = 16


def _bucket_thresholds():
    exact = N_BUCKETS // 2
    n = np.arange(1, 4 * MAX_DISTANCE, dtype=np.float64)
    far = exact + (np.log(n / exact) / math.log(MAX_DISTANCE / exact) * (N_BUCKETS - exact)).astype(np.int64)
    b = np.where(n < exact, n, np.minimum(far, N_BUCKETS - 1)).astype(np.int64)
    return [int(n[b >= k].min()) for k in range(exact + 1, N_BUCKETS)]


_THR = _bucket_thresholds()


def _cparams(sem):
    return pltpu.CompilerParams(dimension_semantics=sem, vmem_limit_bytes=VMEM_LIMIT)


def _dot(a, b):
    return jnp.dot(a, b, preferred_element_type=F32)


def _dot_nt(a, b):
    return lax.dot_general(a, b, (((1,), (1,)), ((), ())), preferred_element_type=F32)


def _split2(x):
    hi = x.astype(BF16)
    lo = (x - hi.astype(F32)).astype(BF16)
    return hi, lo


def _split3(x):
    hi = x.astype(BF16)
    r = x - hi.astype(F32)
    mid = r.astype(BF16)
    lo = (r - mid.astype(F32)).astype(BF16)
    return hi, mid, lo


def _dot3_l(x, m):
    hi, mid, lo = _split3(x)
    return _dot(hi, m) + _dot(mid, m) + _dot(lo, m)


def _dot3_r(m, x):
    hi, mid, lo = _split3(x)
    return _dot(m, hi) + _dot(m, mid) + _dot(m, lo)


def _dot2_l(x, m):
    hi, lo = _split2(x)
    return _dot(hi, m) + _dot(lo, m)


def _iota(shape, dim):
    return lax.broadcasted_iota(I32, shape, dim)


def _div(x, k):
    return lax.shift_right_arithmetic(x, jnp.int32(k.bit_length() - 1))


def _mod(x, k):
    return x & (k - 1)


def _rms(x, g):
    r = lax.rsqrt(jnp.mean(x * x, axis=-1, keepdims=True) + EPS)
    return (x * r) * g


def _bucket(d):
    far = jnp.full(d.shape, N_BUCKETS // 2, I32)
    for thr in _THR:
        far = far + (d >= thr).astype(I32)
    return jnp.where(d < N_BUCKETS // 2, d, far)


def _bias_lookup(bucket, rb_get):
    acc = jnp.zeros(bucket.shape, F32)
    for b in range(N_BUCKETS):
        acc = jnp.where(bucket == b, rb_get(b), acc)
    return acc


def _topk_mask(score, k):
    idx = _iota(score.shape, 1).astype(F32)

    def body(_, c):
        work, sel = c
        mx = jnp.max(work, axis=1, keepdims=True)
        first = jnp.min(jnp.where(work == mx, idx, 1e9), axis=1, keepdims=True)
        hit = idx == first
        return jnp.where(hit, -3.0, work), jnp.where(hit, 1.0, sel)

    _, sel = lax.fori_loop(0, k, body, (score, jnp.zeros(score.shape, F32)))
    return sel


def _proj_kernel(x_ref, g_ref, w_ref, bf_ref, qf_ref, fkv_ref, qn_ref, nkv_ref, wkv_ref, misc_ref):
    h = _rms(x_ref[...], g_ref[...]).astype(BF16)
    qf_ref[...] = _dot(h, w_ref[:, C_QF:C_FKV])
    fkv_ref[...] = _dot(h, w_ref[:, C_FKV:C_QN])
    qn_ref[...] = _dot(h, w_ref[:, C_QN:C_NKV])
    nkv_ref[...] = _dot(h, w_ref[:, C_NKV:C_WKV])
    wkv_ref[...] = _dot(h, w_ref[:, C_WKV:C_MISC])
    z = _dot(h, w_ref[:, C_MISC:C_END]) + bf_ref[...]
    lane = _iota(z.shape, 1)
    logsig = jnp.minimum(z, 0.0) - jnp.log(1.0 + jnp.exp(-jnp.abs(z)))
    sig = 1.0 / (1.0 + jnp.exp(-z))
    misc_ref[...] = jnp.where(lane < MISC_GATE, logsig, jnp.where(lane < MISC_GATE + 3 * H_NSA, sig, 0.0))


def _proj(x2, g, w, bf):
    T, D = x2.shape
    tm = min(512, T)
    widths = (C_FKV - C_QF, C_QN - C_FKV, C_NKV - C_QN, C_WKV - C_NKV, C_MISC - C_WKV, C_END - C_MISC)
    return pl.pallas_call(
        _proj_kernel,
        grid=(T // tm,),
        in_specs=[pl.BlockSpec((tm, D), lambda i: (i, 0)),
                  pl.BlockSpec((1, D), lambda i: (0, 0)),
                  pl.BlockSpec((D, C_END), lambda i: (0, 0)),
                  pl.BlockSpec((1, LANES), lambda i: (0, 0))],
        out_specs=[pl.BlockSpec((tm, wd), lambda i: (i, 0)) for wd in widths],
        out_shape=[jax.ShapeDtypeStruct((T, wd), F32) for wd in widths],
        compiler_params=_cparams(("arbitrary",)),
        name="proj",
    )(x2, g, w, bf)


def _fox_prep_kernel(qf_ref, k_ref, v_ref, misc_ref, qa_ref, ka_ref, vb_ref, carry_ref):
    i = pl.program_id(1)
    tm = qf_ref.shape[1]

    @pl.when(i == 0)
    def _():
        carry_ref[...] = jnp.zeros_like(carry_ref)

    lane = _iota((tm, LANES), 1)
    lf = jnp.where(lane < H_FOX, misc_ref[0], 0.0)
    tril = (_iota((tm, tm), 0) >= _iota((tm, tm), 1)).astype(BF16)
    cum = _dot3_r(tril, lf) + carry_ref[0:1, :]
    carry_ref[...] = jnp.broadcast_to(cum[tm - 1:tm, :], carry_ref.shape)
    hi = cum.astype(BF16).astype(F32)
    r = cum - hi
    mid = r.astype(BF16).astype(F32)
    lo = (r - mid).astype(BF16).astype(F32)
    cc = (hi + pltpu.roll(mid, H_FOX, 1) + pltpu.roll(lo, 2 * H_FOX, 1)).astype(BF16)
    er = _iota((LANES, H_FOX * LANES), 0)
    ec = _iota((LANES, H_FOX * LANES), 1)
    part = _div(er, H_FOX)
    head = _mod(er, H_FOX)
    inb = er < 3 * H_FOX
    eq = jnp.where(inb & (ec == head * LANES + AUG0 + part), 1.0, 0.0).astype(BF16)
    ek = jnp.where(inb & (ec == head * LANES + AUG1 + part), -1.0, 0.0).astype(BF16)
    cl = _mod(_iota((1, H_FOX * LANES), 1), LANES)
    ones_q = jnp.where((cl >= AUG1) & (cl < AUG1 + 3), 1.0, 0.0)
    ones_k = jnp.where((cl >= AUG0) & (cl < AUG0 + 3), 1.0, 0.0)
    cols_q = _dot(cc, eq) + ones_q
    cols_k = _dot(cc, ek) + ones_k
    q = qf_ref[0]
    k = k_ref[0]
    for h in range(H_FOX):
        a = (h // 2) * LANES
        qt = q[:, a:a + LANES]
        kt = k[:, a:a + LANES]
        if h % 2:
            qt = pltpu.roll(qt, HD, 1)
            kt = pltpu.roll(kt, HD, 1)
        qa_ref[0, h] = jnp.where(lane < HD, qt * SCALE, cols_q[:, h * LANES:(h + 1) * LANES]).astype(BF16)
        ka_ref[0, h] = jnp.where(lane < HD, kt, cols_k[:, h * LANES:(h + 1) * LANES]).astype(BF16)
    vb_ref[0] = v_ref[0].astype(BF16)


def _fox_prep(qf, fkv, misc):
    B, S, _ = qf.shape
    tm = min(512, S)
    return pl.pallas_call(
        _fox_prep_kernel,
        grid=(B, S // tm),
        in_specs=[pl.BlockSpec((1, tm, FOX_W), lambda b, i: (b, i, 0)),
                  pl.BlockSpec((1, tm, FOX_W), lambda b, i: (b, i, 0)),
                  pl.BlockSpec((1, tm, FOX_W), lambda b, i: (b, i, 1)),
                  pl.BlockSpec((1, tm, LANES), lambda b, i: (b, i, 0))],
        out_specs=[pl.BlockSpec((1, H_FOX, tm, LANES), lambda b, i: (b, 0, i, 0)),
                   pl.BlockSpec((1, H_FOX, tm, LANES), lambda b, i: (b, 0, i, 0)),
                   pl.BlockSpec((1, tm, FOX_W), lambda b, i: (b, i, 0))],
        out_shape=[jax.ShapeDtypeStruct((B, H_FOX, S, LANES), BF16),
                   jax.ShapeDtypeStruct((B, H_FOX, S, LANES), BF16),
                   jax.ShapeDtypeStruct((B, S, FOX_W), BF16)],
        scratch_shapes=[pltpu.VMEM((8, LANES), F32)],
        compiler_params=_cparams(("arbitrary", "arbitrary")),
        name="fox_prep",
    )(qf, fkv, fkv, misc)


def _softmax_update(s, v, m_ref, l_ref, acc_ref):
    m_old = m_ref[...]
    m_new = jnp.maximum(m_old, jnp.max(s, axis=1, keepdims=True))
    alpha = jnp.exp(m_old - m_new)
    p = jnp.exp(s - m_new)
    l_ref[...] = alpha * l_ref[...] + jnp.sum(p, axis=1, keepdims=True)
    acc_ref[...] = alpha * acc_ref[...] + _dot(p.astype(BF16), v)
    m_ref[...] = m_new


def _softmax_init(m_ref, l_ref, acc_ref):
    m_ref[...] = jnp.full(m_ref.shape, NEG, F32)
    l_ref[...] = jnp.zeros(l_ref.shape, F32)
    acc_ref[...] = jnp.zeros(acc_ref.shape, F32)


def _softmax_update_t(s, vt, m_ref, l_ref, acc_ref):
    m_old = m_ref[...]
    m_new = jnp.maximum(m_old, jnp.max(s, axis=0, keepdims=True))
    alpha = jnp.exp(m_old - m_new)
    p = jnp.exp(s - m_new)
    l_ref[...] = alpha * l_ref[...] + jnp.sum(p, axis=0, keepdims=True)
    acc_ref[...] = alpha * acc_ref[...] + _dot(vt, p.astype(BF16))
    m_ref[...] = m_new


def _fox_flash_kernel(qa_ref, ka_ref, vt_ref, o_ref, m_ref, l_ref, acc_ref):
    qi = pl.program_id(2)
    tq = qa_ref.shape[2]
    _softmax_init(m_ref, l_ref, acc_ref)

    def tile(kt, diag):
        ks = pl.multiple_of(kt * tq, tq)
        vt = vt_ref[0, :, pl.ds(ks, tq)]
        for hh in range(2):
            s = _dot_nt(ka_ref[0, hh, pl.ds(ks, tq), :], qa_ref[0, hh])
            if diag:
                s = jnp.where(_iota(s.shape, 0) <= _iota(s.shape, 1), s, NEG)
            _softmax_update_t(s, vt, m_ref.at[hh], l_ref.at[hh], acc_ref.at[hh])

    def full_tile(kt, c):
        tile(kt, False)
        return c

    lax.fori_loop(0, qi, full_tile, 0)
    tile(qi, True)
    halves = [(acc_ref[hh] / jnp.maximum(l_ref[hh], TINY))[hh * HD:(hh + 1) * HD, :] for hh in range(2)]
    o_ref[0] = jnp.concatenate(halves, axis=0).T


def _fox_flash(qa, ka, vt):
    B, H, S, _ = qa.shape
    tq = min(512, S)
    return pl.pallas_call(
        _fox_flash_kernel,
        grid=(B, H // 2, S // tq),
        in_specs=[pl.BlockSpec((1, 2, tq, LANES), lambda b, p, i: (b, p, i, 0)),
                  pl.BlockSpec((1, 2, S, LANES), lambda b, p, i: (b, p, 0, 0)),
                  pl.BlockSpec((1, LANES, S), lambda b, p, i: (b, p, 0))],
        out_specs=pl.BlockSpec((1, tq, LANES), lambda b, p, i: (b, i, p)),
        out_shape=jax.ShapeDtypeStruct((B, S, FOX_W), F32),
        scratch_shapes=[pltpu.VMEM((2, 1, tq), F32), pltpu.VMEM((2, 1, tq), F32), pltpu.VMEM((2, LANES, tq), F32)],
        compiler_params=_cparams(("arbitrary", "arbitrary", "arbitrary")),
        name="fox_flash",
    )(qa, ka, vt)


def _gelu_tanh(x):
    return 0.5 * x * (1.0 + jnp.tanh(math.sqrt(2.0 / math.pi) * (x + 0.044715 * (x * x * x))))


def _compress_core(x, pt, pb, w1t, w1b, b1, w2w):
    n = x.shape[0]
    a = _dot((x + pt).astype(BF16), w1t)
    b = _dot((x + pb).astype(BF16), w1b)
    h = a + pltpu.roll(b, n - 1, 0) + b1
    return _dot(_gelu_tanh(h).astype(BF16), w2w)


CMP_PAD_FRONT = 16
CMP_PAD_BACK = 112


def _store_cmp(out_ref, idx, res, n):
    out_ref[idx] = jnp.zeros(out_ref.shape[len(idx):], BF16)
    out_ref[idx + (slice(CMP_PAD_FRONT, CMP_PAD_FRONT + n), slice(None))] = res.astype(BF16)


def _compress_prompt_kernel(x_ref, pt_ref, pb_ref, w1t_ref, w1b_ref, b1_ref, w2w_ref, out_ref):
    n = x_ref.shape[1] // CMP_STRIDE
    x = jnp.concatenate([x_ref[0, pl.ds(j, n, stride=CMP_STRIDE), :] for j in range(CMP_STRIDE)], axis=1)
    res = _compress_core(x, pt_ref[0], pb_ref[0], w1t_ref[0], w1b_ref[0], b1_ref[0], w2w_ref[0])
    _store_cmp(out_ref, (0, 0), res, n)


def _compress_prompt(nkv, cw):
    B, S, _ = nkv.shape
    n = S // CMP_STRIDE
    npad = n + CMP_PAD_FRONT + CMP_PAD_BACK
    kx = CMP_STRIDE * LANES
    return pl.pallas_call(
        _compress_prompt_kernel,
        grid=(B, 2),
        in_specs=[pl.BlockSpec((1, S, LANES), lambda b, w: (b, 0, w)),
                  pl.BlockSpec((1, 1, kx), lambda b, w: (w, 0, 0)),
                  pl.BlockSpec((1, 1, kx), lambda b, w: (w, 0, 0)),
                  pl.BlockSpec((1, kx, 2 * CMP_HID), lambda b, w: (w, 0, 0)),
                  pl.BlockSpec((1, kx, 2 * CMP_HID), lambda b, w: (w, 0, 0)),
                  pl.BlockSpec((1, 1, 2 * CMP_HID), lambda b, w: (w, 0, 0)),
                  pl.BlockSpec((1, 2 * CMP_HID, NSA_W), lambda b, w: (w, 0, 0))],
        out_specs=pl.BlockSpec((1, 1, npad, NSA_W), lambda b, w: (b, w, 0, 0)),
        out_shape=jax.ShapeDtypeStruct((B, 2, npad, NSA_W), BF16),
        compiler_params=_cparams(("arbitrary", "arbitrary")),
        name="compress_prompt",
    )(nkv, cw["pt"], cw["pb"], cw["w1t"], cw["w1b"], cw["b1"], cw["w2w"])


def _compress_sample_kernel(pt_tab, *refs):
    P = PAGES_PER_STEP
    pages = (refs[:P], refs[P:2 * P])
    pt_ref, pb_ref, w1t_ref, w1b_ref, b1_ref, w2w_ref, out_ref, x_sc = refs[2 * P:]
    s = pl.program_id(1)
    n = x_sc.shape[1]
    rows = P * (PAGE // CMP_STRIDE)
    r0 = pl.multiple_of(s * rows, rows)
    for w in range(2):
        for j in range(CMP_STRIDE):
            x_sc[w, pl.ds(r0, rows), j * LANES:(j + 1) * LANES] = jnp.concatenate(
                [pg[pl.ds(j, PAGE // CMP_STRIDE, stride=CMP_STRIDE), :] for pg in pages[w]], axis=0)

    @pl.when(s == pl.num_programs(1) - 1)
    def _():
        for w in range(2):
            res = _compress_core(x_sc[w], pt_ref[w], pb_ref[w], w1t_ref[w], w1b_ref[w], b1_ref[w], w2w_ref[w])
            _store_cmp(out_ref, (0, w), res, n)


def _compress_sample(cache_nsa, page_table, layer, cw):
    nb, npages = page_table.shape
    P = PAGES_PER_STEP
    n = npages * (PAGE // CMP_STRIDE)
    npad = n + CMP_PAD_FRONT + CMP_PAD_BACK
    kx = CMP_STRIDE * LANES

    def page_spec(k, w):
        return pl.BlockSpec((None, None, PAGE, LANES), lambda b, s, pt: (layer, pt[b, s * P + k], 0, w))

    const3 = lambda b, s, pt: (0, 0, 0)
    return pl.pallas_call(
        _compress_sample_kernel,
        grid_spec=pltpu.PrefetchScalarGridSpec(
            num_scalar_prefetch=1,
            grid=(nb, npages // P),
            in_specs=[page_spec(k, 0) for k in range(P)] + [page_spec(k, 1) for k in range(P)] + [
                pl.BlockSpec((2, 1, kx), const3), pl.BlockSpec((2, 1, kx), const3),
                pl.BlockSpec((2, kx, 2 * CMP_HID), const3), pl.BlockSpec((2, kx, 2 * CMP_HID), const3),
                pl.BlockSpec((2, 1, 2 * CMP_HID), const3), pl.BlockSpec((2, 2 * CMP_HID, NSA_W), const3)],
            out_specs=pl.BlockSpec((1, 2, npad, NSA_W), lambda b, s, pt: (b, 0, 0, 0)),
            scratch_shapes=[pltpu.VMEM((2, n, kx), F32)]),
        out_shape=jax.ShapeDtypeStruct((nb, 2, npad, NSA_W), BF16),
        compiler_params=_cparams(("arbitrary", "arbitrary")),
        name="compress_sample",
    )(page_table, *([cache_nsa] * (2 * P)), cw["pt"], cw["pb"], cw["w1t"], cw["w1b"], cw["b1"], cw["w2w"])


def _nsa_prep_kernel(sel_ref, win_ref, ksa_ref, kwa_ref, vwd_ref):
    i = pl.program_id(1)
    tm = sel_ref.shape[1]
    lane = _iota((tm, LANES), 1)
    pos = i * tm + _iota((tm, LANES), 0)
    onehot = jnp.where(_div(pos, SEL_LEN) == lane, 1.0, 0.0)
    ks = sel_ref[0][:, 0:LANES]
    w = win_ref[0]
    kw = w[:, 0:LANES]
    vw = w[:, LANES:2 * LANES]
    rks = pltpu.roll(ks, HD, 1)
    rkw = pltpu.roll(kw, HD, 1)
    rvw = pltpu.roll(vw, HD, 1)
    for g in range(NSA_KV):
        left = jnp.where(lane < HD, ks if g == 0 else rks, jnp.where(lane < HD + 3, 1.0, 0.0))
        ksa_ref[0, g] = jnp.concatenate([left, onehot], axis=1).astype(BF16)
        kwa_ref[0, g] = jnp.where(lane < HD, kw if g == 0 else rkw, 0.0).astype(BF16)
        vd = jnp.where(lane < HD, vw, rvw) if g == 0 else jnp.where(lane < HD, rvw, vw)
        vwd_ref[0, g] = vd.astype(BF16)


def _nsa_prep(nkv, wkv):
    B, S, _ = nkv.shape
    tm = min(512, S)
    spec128 = pl.BlockSpec((1, NSA_KV, tm, LANES), lambda b, i: (b, 0, i, 0))
    return pl.pallas_call(
        _nsa_prep_kernel,
        grid=(B, S // tm),
        in_specs=[pl.BlockSpec((1, tm, 2 * LANES), lambda b, i: (b, i, 1)),
                  pl.BlockSpec((1, tm, 2 * LANES), lambda b, i: (b, i, 0))],
        out_specs=[pl.BlockSpec((1, NSA_KV, tm, 2 * LANES), lambda b, i: (b, 0, i, 0)), spec128, spec128],
        out_shape=[jax.ShapeDtypeStruct((B, NSA_KV, S, 2 * LANES), BF16),
                   jax.ShapeDtypeStruct((B, NSA_KV, S, LANES), BF16),
                   jax.ShapeDtypeStruct((B, NSA_KV, S, LANES), BF16)],
        compiler_params=_cparams(("arbitrary", "arbitrary")),
        name="nsa_prep",
    )(nkv, wkv)


def _overlap(i_blk, j_blk):
    start = i_blk * CMP_STRIDE
    return (start < j_blk * SEL_LEN + SEL_LEN) & (start + CMP_LEN > j_blk * SEL_LEN)


NEAR_BACK = 16


def _nsa_cmp_kernel(rb_ref, q_ref, kcw_ref, oc_ref, mq_ref, fc_ref):
    b = pl.program_id(0)
    i = pl.program_id(1)
    tq = q_ref.shape[1]
    n = kcw_ref.shape[2] - CMP_PAD_FRONT - CMP_PAD_BACK
    n_sel = n * CMP_STRIDE // SEL_LEN
    k_eff = min(N_SELECT, n_sel)

    @pl.when((b == 0) & (i == 0))
    def _():
        dist = _iota((tq, LANES), 0) + (NEAR_BACK * CMP_STRIDE - (CMP_LEN - 1)) - CMP_STRIDE * _iota((tq, LANES), 1)
        bk = _bucket(jnp.clip(dist, 0, MAX_DISTANCE - 1))
        for h in range(H_NSA):
            fc_ref[h] = jnp.where(dist >= 0, _bias_lookup(bk, lambda bb, h=h: rb_ref[bb, h]), NEG)

    qs = i * tq
    i0 = qs // CMP_STRIDE - NEAR_BACK
    q = q_ref[0] * SCALE
    lanehead = _div(_iota((tq, NSA_W), 1), HD)
    kfar = kcw_ref[0, 0, CMP_PAD_FRONT:CMP_PAD_FRONT + n, :]
    vfar = kcw_ref[0, 1, CMP_PAD_FRONT:CMP_PAD_FRONT + n, :]
    st = pl.multiple_of(qs // CMP_STRIDE, 16)
    knear = kcw_ref[0, 0, pl.ds(st, LANES), :]
    vnear = kcw_ref[0, 1, pl.ds(st, LANES), :]
    farmask = _iota((tq, n), 1) < i0
    nearmask = (_iota((tq, LANES), 1) + i0) >= 0
    oc = jnp.zeros((tq, NSA_W), F32)
    ps_far = [jnp.zeros((tq, n), F32) for _ in range(NSA_KV)]
    ps_near = [jnp.zeros((tq, LANES), F32) for _ in range(NSA_KV)]
    for h in range(H_NSA):
        g = h // NSA_GH
        qm = jnp.where(lanehead == h, q, 0.0).astype(BF16)
        sf = jnp.where(farmask, _dot_nt(qm, kfar) + rb_ref[N_BUCKETS - 1, h], NEG)
        sn = jnp.where(nearmask, _dot_nt(qm, knear) + fc_ref[h], NEG)
        m = jnp.maximum(jnp.max(sf, axis=1, keepdims=True), jnp.max(sn, axis=1, keepdims=True))
        pf = jnp.where(sf > 0.5 * NEG, jnp.exp(sf - m), 0.0)
        pn = jnp.where(sn > 0.5 * NEG, jnp.exp(sn - m), 0.0)
        l = jnp.sum(pf, axis=1, keepdims=True) + jnp.sum(pn, axis=1, keepdims=True)
        inv = 1.0 / jnp.maximum(l, TINY)
        pf = pf * inv
        pn = pn * inv
        o = _dot(pf.astype(BF16), vfar) + _dot(pn.astype(BF16), vnear)
        oc = jnp.where(lanehead == h, o, oc)
        ps_far[g] = ps_far[g] + pf
        ps_near[g] = ps_near[g] + pn
    oc_ref[0] = oc

    mov_far = jnp.where(_overlap(_iota((n, LANES), 0), _iota((n, LANES), 1)), 1.0, 0.0).astype(BF16)
    mov_near = jnp.where(_overlap(_iota((LANES, LANES), 0) + i0, _iota((LANES, LANES), 1)), 1.0, 0.0).astype(BF16)
    qpos = qs + _iota((tq, LANES), 0)
    jb = _iota((tq, LANES), 1)
    qblk = _div(qpos, SEL_LEN)
    valid = jb * SEL_LEN <= qpos
    forced = (jb == 0) | (jb == qblk) | (jb == qblk - 1)
    for g in range(NSA_KV):
        p_slc = _dot2_l(ps_far[g], mov_far) + _dot2_l(ps_near[g], mov_near)
        score = jnp.where(forced, FORCE_SCORE, jnp.where(valid, p_slc, -1.0))
        sel = _topk_mask(score, k_eff)
        mq_ref[0, g] = jnp.where((sel > 0.5) & (score >= 0.0), 0.0, NEG).astype(BF16)


def _nsa_cmp(rb, qn, kcw):
    B, S, _ = qn.shape
    tq = min(256, S)
    npad = kcw.shape[2]
    return pl.pallas_call(
        _nsa_cmp_kernel,
        grid=(B, S // tq),
        in_specs=[pl.BlockSpec(memory_space=pltpu.SMEM),
                  pl.BlockSpec((1, tq, NSA_W), lambda b, i: (b, i, 0)),
                  pl.BlockSpec((1, 2, npad, NSA_W), lambda b, i: (b, 0, 0, 0))],
        out_specs=[pl.BlockSpec((1, tq, NSA_W), lambda b, i: (b, i, 0)),
                   pl.BlockSpec((1, NSA_KV, tq, LANES), lambda b, i: (b, 0, i, 0))],
        out_shape=[jax.ShapeDtypeStruct((B, S, NSA_W), F32),
                   jax.ShapeDtypeStruct((B, NSA_KV, S, LANES), BF16)],
        scratch_shapes=[pltpu.VMEM((H_NSA, tq, LANES), F32)],
        compiler_params=_cparams(("arbitrary", "arbitrary")),
        name="nsa_cmp",
    )(rb, qn, kcw)


def _stack_heads(q, g_rows, extra, qs_ref):
    tq = q.shape[0]
    lane = _iota((tq, LANES), 1)
    for hh in range(NSA_GH):
        a = (hh // 2) * LANES
        t = q[:, a:a + LANES]
        if hh % 2:
            t = pltpu.roll(t, HD, 1)
        left = jnp.where(lane < HD, t * SCALE, g_rows[hh:hh + 1, :]).astype(BF16)
        if extra is None:
            qs_ref[hh * tq:(hh + 1) * tq, :] = left
        else:
            qs_ref[hh * tq:(hh + 1) * tq, :] = jnp.concatenate([left, extra], axis=1)


def _unstack_heads(a, tq):
    lane = _iota((tq, LANES), 1)
    p0 = jnp.where(lane < HD, a[0:tq], a[tq:2 * tq])
    p1 = jnp.where(lane < HD, a[2 * tq:3 * tq], a[3 * tq:4 * tq])
    return jnp.concatenate([p0, p1], axis=1)


SEL_BACK = 128


def _nsa_sel_kernel(rb_ref, q_ref, mq_ref, brow_ref, ksa_ref, vt_ref, o_ref, dn_ref, qs_ref, m_ref, l_ref, acc_ref):
    b = pl.program_id(0)
    g = pl.program_id(1)
    i = pl.program_id(2)
    tq = q_ref.shape[1]
    wn = tq + SEL_BACK

    @pl.when((b == 0) & (g == 0) & (i == 0))
    def _():
        dist = _iota((wn, tq), 1) + SEL_BACK - _iota((wn, tq), 0)
        bk = _bucket(jnp.clip(dist, 0, MAX_DISTANCE - 1))
        for h in range(H_NSA):
            far = rb_ref[N_BUCKETS - 1, h]
            val = _bias_lookup(bk, lambda bb, h=h: rb_ref[bb, h]) - far
            dn_ref[h // NSA_GH, :, (h % NSA_GH) * tq:(h % NSA_GH + 1) * tq] = jnp.where(dist >= 0, val, NEG)

    qs = i * tq
    _stack_heads(q_ref[0], brow_ref[0], mq_ref[0, 0], qs_ref)
    _softmax_init(m_ref, l_ref, acc_ref)
    hw = (NSA_GH // 2) * tq

    def tile(start, size, bias):
        k = ksa_ref[0, 0, pl.ds(start, size), :]
        vt = vt_ref[0, 0, :, pl.ds(start, size)]
        for half in range(2):
            s = _dot_nt(k, qs_ref[half * hw:(half + 1) * hw, :])
            if bias is not None:
                s = s + bias(half)
            _softmax_update_t(s, vt, m_ref.at[half], l_ref.at[half], acc_ref.at[half])

    @pl.when(i == 0)
    def _():
        tile(0, tq, lambda half: dn_ref[g, SEL_BACK:, half * hw:(half + 1) * hw])

    @pl.when(i > 0)
    def _():
        tile(pl.multiple_of(qs - SEL_BACK, LANES), wn, lambda half: dn_ref[g, :, half * hw:(half + 1) * hw])
        tile(pl.multiple_of(qs - tq, LANES), tq - SEL_BACK, None)

    def far_tile(kt, c):
        tile(pl.multiple_of(kt * tq, tq), tq, None)
        return c

    lax.fori_loop(0, jnp.maximum(i - 1, 0), far_tile, 0)
    parts = []
    for half in range(2):
        a = acc_ref[half] / jnp.maximum(l_ref[half], TINY)
        parts += [a[:, 0:tq], a[:, tq:2 * tq]]
    o_ref[0] = jnp.concatenate(parts, axis=0).T


def _nsa_sel(rb, qn, maskq, brow, ksa, vst):
    B, S, _ = qn.shape
    tq = min(512, S)
    G = NSA_KV
    return pl.pallas_call(
        _nsa_sel_kernel,
        grid=(B, G, S // tq),
        in_specs=[pl.BlockSpec(memory_space=pltpu.SMEM),
                  pl.BlockSpec((1, tq, 2 * LANES), lambda b, g, i: (b, i, g)),
                  pl.BlockSpec((1, 1, tq, LANES), lambda b, g, i: (b, g, i, 0)),
                  pl.BlockSpec((1, NSA_GH, LANES), lambda b, g, i: (g, 0, 0)),
                  pl.BlockSpec((1, 1, S, 2 * LANES), lambda b, g, i: (b, g, 0, 0)),
                  pl.BlockSpec((1, 1, HD, S), lambda b, g, i: (b, g, 0, 0))],
        out_specs=pl.BlockSpec((1, tq, 2 * LANES), lambda b, g, i: (b, i, g)),
        out_shape=jax.ShapeDtypeStruct((B, S, NSA_W), F32),
        scratch_shapes=[pltpu.VMEM((G, tq + SEL_BACK, NSA_GH * tq), F32),
                        pltpu.VMEM((NSA_GH * tq, 2 * LANES), BF16),
                        pltpu.VMEM((2, 1, NSA_GH // 2 * tq), F32), pltpu.VMEM((2, 1, NSA_GH // 2 * tq), F32),
                        pltpu.VMEM((2, HD, NSA_GH // 2 * tq), F32)],
        compiler_params=_cparams(("arbitrary", "arbitrary", "arbitrary")),
        name="nsa_sel",
    )(rb, qn, maskq, brow, ksa, vst)


def _nsa_win_kernel(rb_ref, q_ref, kwa_ref, vwd_ref, o_ref, dw_ref, qs_ref):
    b = pl.program_id(0)
    g = pl.program_id(1)
    i = pl.program_id(2)
    tq = q_ref.shape[1]
    wk = tq + WINDOW

    @pl.when((b == 0) & (g == 0) & (i == 0))
    def _():
        dist = _iota((tq, wk), 0) + WINDOW - _iota((tq, wk), 1)
        bk = _bucket(jnp.clip(dist, 0, MAX_DISTANCE - 1))
        ok = (dist >= 0) & (dist < WINDOW)
        for h in range(H_NSA):
            val = _bias_lookup(bk, lambda bb, h=h: rb_ref[bb, h])
            dw_ref[h // NSA_GH, (h % NSA_GH) * tq:(h % NSA_GH + 1) * tq, :] = jnp.where(ok, val, NEG)

    qs = pl.multiple_of(i * tq, tq)
    _stack_heads(q_ref[0], jnp.zeros((NSA_GH, LANES), F32), None, qs_ref)
    s = _dot_nt(qs_ref[...], kwa_ref[0, 0, pl.ds(qs, wk), :]) + dw_ref[g]
    s = jnp.where(_iota(s.shape, 1) + qs >= WINDOW, s, NEG)
    m = jnp.max(s, axis=1, keepdims=True)
    p = jnp.exp(s - m)
    l = jnp.sum(p, axis=1, keepdims=True)
    a = _dot(p.astype(BF16), vwd_ref[0, 0, pl.ds(qs, wk), :]) / jnp.maximum(l, TINY)
    o_ref[0] = _unstack_heads(a, tq)


def _nsa_win(rb, qn, kwa_p, vwd_p):
    B, S, _ = qn.shape
    tq = min(256, S)
    G = NSA_KV
    sp = kwa_p.shape[2]
    return pl.pallas_call(
        _nsa_win_kernel,
        grid=(B, G, S // tq),
        in_specs=[pl.BlockSpec(memory_space=pltpu.SMEM),
                  pl.BlockSpec((1, tq, 2 * LANES), lambda b, g, i: (b, i, g)),
                  pl.BlockSpec((1, 1, sp, LANES), lambda b, g, i: (b, g, 0, 0)),
                  pl.BlockSpec((1, 1, sp, LANES), lambda b, g, i: (b, g, 0, 0))],
        out_specs=pl.BlockSpec((1, tq, 2 * LANES), lambda b, g, i: (b, i, g)),
        out_shape=jax.ShapeDtypeStruct((B, S, NSA_W), F32),
        scratch_shapes=[pltpu.VMEM((G, NSA_GH * tq, tq + WINDOW), F32),
                        pltpu.VMEM((NSA_GH * tq, LANES), BF16)],
        compiler_params=_cparams(("arbitrary", "arbitrary", "arbitrary")),
        name="nsa_win",
    )(rb, qn, kwa_p, vwd_p)


def _merge_kernel(x_ref, of_ref, oc_ref, os_ref, ow_ref, misc_ref, g_ref, w_ref, out_ref):
    tm = x_ref.shape[0]
    hi, lo = _split2(misc_ref[...])
    er = _iota((LANES, NSA_W), 0)
    ec = _iota((LANES, NSA_W), 1)
    onsa = jnp.zeros((tm, NSA_W), F32)
    for k, o_ref in enumerate((oc_ref, os_ref, ow_ref)):
        e = jnp.where(er == MISC_GATE + k * H_NSA + _div(ec, HD), 1.0, 0.0).astype(BF16)
        onsa = onsa + (_dot(hi, e) + _dot(lo, e)) * o_ref[...]
    g = g_ref[...]
    a = _rms(of_ref[...], g[:, :FOX_W]).astype(BF16)
    c = _rms(onsa, g[:, FOX_W:]).astype(BF16)
    out_ref[...] = x_ref[...] + _dot(a, w_ref[0:FOX_W, :]) + _dot(c, w_ref[FOX_W:, :])


def _merge(x2, ofox, oc, os_, ow, misc, g, w):
    T, D = x2.shape
    tm = min(512, T)
    row = lambda wd: pl.BlockSpec((tm, wd), lambda i: (i, 0))
    return pl.pallas_call(
        _merge_kernel,
        grid=(T // tm,),
        in_specs=[row(D), row(FOX_W), row(NSA_W), row(NSA_W), row(NSA_W), row(LANES),
                  pl.BlockSpec((1, D), lambda i: (0, 0)),
                  pl.BlockSpec((D, D), lambda i: (0, 0))],
        out_specs=row(D),
        out_shape=jax.ShapeDtypeStruct((T, D), F32),
        compiler_params=_cparams(("arbitrary",)),
        name="merge",
    )(x2, ofox, oc, os_, ow, misc, g, w)


FF_CHUNK = 256


def _ffn_kernel(*refs, seq_len, short_len, final):
    if seq_len is None:
        (x_ref, g2_ref, wg_ref, wu_ref, cw_ref, cb_ref, wd_ref, gf_ref, hm1_ref, hm2_ref,
         out_ref, gt_ref, h_sc, acc_sc) = refs
    else:
        (x_ref, g2_ref, wg_ref, wu_ref, cw_ref, cb_ref, wd_ref, gf_ref,
         out_ref, gt_ref, h_sc, acc_sc, carry_sc) = refs
    i = pl.program_id(0)
    j = pl.program_id(1)
    tm = x_ref.shape[0]

    @pl.when(j == 0)
    def _():
        h_sc[...] = _rms(x_ref[...], g2_ref[...]).astype(BF16)
        acc_sc[...] = jnp.zeros_like(acc_sc)

    h = h_sc[...]
    gch = _dot(h, wg_ref[...])
    u = _dot(h, wu_ref[...])
    r1 = pltpu.roll(gch, 1, 0)
    r2 = pltpu.roll(gch, 2, 0)
    row = _iota(gch.shape, 0)
    if seq_len is None:
        t = _mod(row, short_len)
        m1 = jnp.where(t == 0, hm1_ref[...], r1)
        m2 = jnp.where(t < 2, hm2_ref[...], r2)
        gt_ref[...] = gch
    else:
        first = (i % (seq_len // tm)) == 0
        c = jnp.where(first, 0.0, carry_sc[j])
        m1 = jnp.where(row == 0, c[1:2, :], r1)
        m2 = jnp.where(row == 0, c[0:1, :], jnp.where(row == 1, c[1:2, :], r2))
        carry_sc[j, 0:2, :] = gch[tm - 2:tm, :]
        gt_ref[0] = gch[tm - 8:tm, :]
    cw = cw_ref[...]
    gc = cb_ref[...] + cw[0:1, :] * m2 + cw[1:2, :] * m1 + cw[2:3, :] * gch
    act = gc * (1.0 / (1.0 + jnp.exp(-gc)))
    acc_sc[...] += _dot((act * u).astype(BF16), wd_ref[...])

    @pl.when(j == pl.num_programs(1) - 1)
    def _():
        y = x_ref[...] + acc_sc[...]
        if final:
            y = _rms(y, gf_ref[...])
        out_ref[...] = y


def _ffn(x2, g2, wgu, cw, cb, wd, gf, *, seq_len, final, hist=None, short_len=None):
    T, D = x2.shape
    dff = wd.shape[0]
    fc = FF_CHUNK
    nff = dff // fc
    tm = min(1024, T) if seq_len is not None else T
    nt = T // tm
    in_specs = [pl.BlockSpec((tm, D), lambda i, j: (i, 0)),
                pl.BlockSpec((1, D), lambda i, j: (0, 0)),
                pl.BlockSpec((D, fc), lambda i, j: (0, j)),
                pl.BlockSpec((D, fc), lambda i, j: (0, nff + j)),
                pl.BlockSpec((CONV_W, fc), lambda i, j: (0, j)),
                pl.BlockSpec((1, fc), lambda i, j: (0, j)),
                pl.BlockSpec((fc, D), lambda i, j: (j, 0)),
                pl.BlockSpec((1, D), lambda i, j: (0, 0))]
    args = [x2, g2, wgu, wgu, cw, cb, wd, gf]
    scratch = [pltpu.VMEM((tm, D), BF16), pltpu.VMEM((tm, D), F32)]
    if seq_len is None:
        in_specs += [pl.BlockSpec((tm, fc), lambda i, j: (i, j)), pl.BlockSpec((tm, fc), lambda i, j: (i, j))]
        args += list(hist)
        gt_spec = pl.BlockSpec((tm, fc), lambda i, j: (i, j))
        gt_shape = jax.ShapeDtypeStruct((T, dff), F32)
    else:
        scratch.append(pltpu.VMEM((nff, 8, fc), F32))
        gt_spec = pl.BlockSpec((1, 8, fc), lambda i, j: (i, 0, j))
        gt_shape = jax.ShapeDtypeStruct((nt, 8, dff), F32)
    return pl.pallas_call(
        functools.partial(_ffn_kernel, seq_len=seq_len, short_len=short_len, final=final),
        grid=(nt, nff),
        in_specs=in_specs,
        out_specs=[pl.BlockSpec((tm, D), lambda i, j: (i, 0)), gt_spec],
        out_shape=[jax.ShapeDtypeStruct((T, D), F32), gt_shape],
        scratch_shapes=scratch,
        compiler_params=_cparams(("arbitrary", "arbitrary")),
        name="ffn",
    )(*args)


def _rows_th(q):
    tq = q.shape[0]
    rows = jnp.concatenate([jnp.broadcast_to(q[t:t + 1, :], (8, q.shape[1])) for t in range(tq)], axis=0)
    keep = _div(_iota(rows.shape, 1), HD) == _mod(_iota(rows.shape, 0), 8)
    return jnp.where(keep, rows * SCALE, 0.0).astype(BF16)


def _diag_rows(o_ref, o32):
    keep = _div(_iota(o32.shape, 1), HD) == _mod(_iota(o32.shape, 0), 8)
    od = jnp.where(keep, o32, 0.0)
    for t in range(o32.shape[0] // 8):
        o_ref[0, t:t + 1, :] = jnp.sum(od[t * 8:(t + 1) * 8, :], axis=0, keepdims=True)


def _fox_sample_kernel(pt_tab, *refs):
    P = PAGES_PER_STEP
    q_ref, kvn_ref, lfn_ref = refs[0:3]
    kv_refs = refs[3:3 + P]
    lf_refs = refs[3 + P:3 + 2 * P]
    o_ref, q_sc, m_ref, l_ref, acc_ref, carry_ref, new_sc = refs[3 + 2 * P:]
    s = pl.program_id(1)
    tq = q_ref.shape[1]
    nr = tq * 8

    @pl.when(s == 0)
    def _():
        q_sc[...] = _rows_th(q_ref[0])
        _softmax_init(m_ref, l_ref, acc_ref)
        carry_ref[...] = jnp.zeros_like(carry_ref)
        new_sc[...] = jnp.zeros_like(new_sc)
        new_sc[0:tq, :] = kvn_ref[0]

    qq = q_sc[...]
    triu = (_iota((PAGE, PAGE), 0) <= _iota((PAGE, PAGE), 1)).astype(BF16)

    def attend(kvs, lfs, extra_mask):
        off = carry_ref[...]
        sc = []
        for kv, lf in zip(kvs, lfs):
            cum = _dot3_l(lf, triu) + off
            off = jnp.broadcast_to(cum[:, PAGE - 1:PAGE], cum.shape)
            sc.append(_dot_nt(qq, kv[:, :FOX_W].astype(BF16)) - jnp.concatenate([cum] * tq, axis=0))
        carry_ref[...] = off
        sc = jnp.concatenate(sc, axis=1)
        if extra_mask is not None:
            sc = jnp.where(extra_mask, sc, NEG)
        m_old = m_ref[...]
        m_new = jnp.maximum(m_old, jnp.max(sc, axis=1, keepdims=True))
        alpha = jnp.exp(m_old - m_new)
        p = jnp.exp(sc - m_new)
        l_ref[...] = alpha * l_ref[...] + jnp.sum(p, axis=1, keepdims=True)
        acc = alpha * acc_ref[...]
        for k, kv in enumerate(kvs):
            acc = acc + _dot(p[:, k * PAGE:(k + 1) * PAGE].astype(BF16), kv[:, FOX_W:].astype(BF16))
        acc_ref[...] = acc
        m_ref[...] = m_new

    attend([r[...] for r in kv_refs], [r[...] for r in lf_refs], None)

    @pl.when(s == pl.num_programs(1) - 1)
    def _():
        key = _iota((nr, PAGE), 1)
        ok = (key < tq) & (key <= _div(_iota((nr, PAGE), 0), 8))
        attend([new_sc[...]], [lfn_ref[0]], ok)
        _diag_rows(o_ref, acc_ref[...] / jnp.maximum(l_ref[...], TINY))


def _fox_sample(qf, fkv_new, lfn_t, cache_kv, cache_lft, page_table, layer):
    nb, npages = page_table.shape
    tq = qf.shape[1]
    P = PAGES_PER_STEP
    kvw = 2 * FOX_W
    seq = lambda b, s, pt: (b, 0, 0)

    def kv_spec(k):
        return pl.BlockSpec((None, None, PAGE, kvw), lambda b, s, pt: (layer, pt[b, s * P + k], 0, 0))

    def lf_spec(k):
        return pl.BlockSpec((None, None, H_FOX, PAGE), lambda b, s, pt: (layer, pt[b, s * P + k], 0, 0))

    return pl.pallas_call(
        _fox_sample_kernel,
        grid_spec=pltpu.PrefetchScalarGridSpec(
            num_scalar_prefetch=1,
            grid=(nb, npages // P),
            in_specs=[pl.BlockSpec((1, tq, FOX_W), seq), pl.BlockSpec((1, tq, kvw), seq),
                      pl.BlockSpec((1, H_FOX, PAGE), seq)]
            + [kv_spec(k) for k in range(P)] + [lf_spec(k) for k in range(P)],
            out_specs=pl.BlockSpec((1, tq, FOX_W), seq),
            scratch_shapes=[pltpu.VMEM((tq * 8, FOX_W), BF16),
                            pltpu.VMEM((tq * 8, 1), F32), pltpu.VMEM((tq * 8, 1), F32),
                            pltpu.VMEM((tq * 8, FOX_W), F32),
                            pltpu.VMEM((H_FOX, PAGE), F32),
                            pltpu.VMEM((PAGE, kvw), F32)]),
        out_shape=jax.ShapeDtypeStruct((nb, tq, FOX_W), F32),
        compiler_params=_cparams(("arbitrary", "arbitrary")),
        name="fox_sample",
    )(page_table, qf, fkv_new, lfn_t, *([cache_kv] * P), *([cache_lft] * P))


def _place_wide():
    r = _iota((LANES, NSA_W), 0)
    c = _iota((LANES, NSA_W), 1)
    return jnp.where(r == _div(c, NSA_GH * HD) * HD + _mod(c, HD), 1.0, 0.0).astype(BF16)


def _rb_col(rbt_ref):
    return lambda bb: rbt_ref[:, bb:bb + 1]


def _ns_attend_kernel(q_ref, kcw_ref, win_ref, wn_ref, rbt_ref, oc_ref, ow_ref, mb_ref, new_sc, *, past):
    tq = q_ref.shape[1]
    nr = tq * 8
    n = kcw_ref.shape[2] - CMP_PAD_FRONT - CMP_PAD_BACK
    n_selp = past // SEL_LEN
    wb = win_ref.shape[0]
    qq = _rows_th(q_ref[0])
    kc = kcw_ref[0, 0, CMP_PAD_FRONT:CMP_PAD_FRONT + n, :]
    vc = kcw_ref[0, 1, CMP_PAD_FRONT:CMP_PAD_FRONT + n, :]
    trow = _div(_iota((nr, n), 0), 8)
    dist = past + trow - CMP_STRIDE * _iota((nr, n), 1) - (CMP_LEN - 1)
    bias = _bias_lookup(_bucket(jnp.clip(dist, 0, MAX_DISTANCE - 1)), _rb_col(rbt_ref))
    ok = dist >= 0
    s = jnp.where(ok, _dot_nt(qq, kc) + bias, NEG)
    m = jnp.max(s, axis=1, keepdims=True)
    p = jnp.where(ok, jnp.exp(s - m), 0.0)
    p = p / jnp.maximum(jnp.sum(p, axis=1, keepdims=True), TINY)
    _diag_rows(oc_ref, _dot(p.astype(BF16), vc))
    mov = jnp.where(_overlap(_iota((n, LANES), 0), _iota((n, LANES), 1)), 1.0, 0.0).astype(BF16)
    x = _dot2_l(p, mov)
    z = x + pltpu.roll(x, nr - 1, 0) + pltpu.roll(x, nr - 2, 0) + pltpu.roll(x, nr - 3, 0)
    z0 = jnp.where(_mod(_iota(z.shape, 0), NSA_GH) == 0, z, 0.0)
    p_slc = z0 + pltpu.roll(z0, 1, 0) + pltpu.roll(z0, 2, 0) + pltpu.roll(z0, 3, 0)
    jb = _iota((nr, LANES), 1)
    forced = (jb == 0) | (jb == n_selp - 1)
    score = jnp.where(forced, FORCE_SCORE, jnp.where(jb < n_selp, p_slc, -1.0))
    k_past = min(N_SELECT, n_selp + 1) - 1
    sel = _topk_mask(score, k_past)
    mb_ref[0] = jnp.where((sel > 0.5) & (score >= 0.0), 0.0, NEG).astype(BF16)
    place = _place_wide()
    w = win_ref[...]
    kw = _dot(w[:, 0:LANES].astype(BF16), place).astype(BF16)
    vw = _dot(w[:, LANES:2 * LANES].astype(BF16), place).astype(BF16)
    new_sc[...] = jnp.zeros_like(new_sc)
    new_sc[0:tq, :] = wn_ref[0]
    wnew = new_sc[...]
    kn = _dot(wnew[:, 0:LANES].astype(BF16), place).astype(BF16)
    vn = _dot(wnew[:, LANES:2 * LANES].astype(BF16), place).astype(BF16)
    dw = _div(_iota((nr, wb), 0), 8) + wb - _iota((nr, wb), 1)
    okw = (dw >= 0) & (dw < WINDOW)
    sw = _dot_nt(qq, kw) + _bias_lookup(_bucket(jnp.clip(dw, 0, MAX_DISTANCE - 1)), _rb_col(rbt_ref))
    sw = jnp.where(okw, sw, NEG)
    dn = _div(_iota((nr, PAGE), 0), 8) - _iota((nr, PAGE), 1)
    okn = (dn >= 0) & (_iota((nr, PAGE), 1) < tq)
    sn = _dot_nt(qq, kn) + _bias_lookup(_bucket(jnp.clip(dn, 0, MAX_DISTANCE - 1)), _rb_col(rbt_ref))
    sn = jnp.where(okn, sn, NEG)
    mw = jnp.maximum(jnp.max(sw, axis=1, keepdims=True), jnp.max(sn, axis=1, keepdims=True))
    pw = jnp.where(okw, jnp.exp(sw - mw), 0.0)
    pn = jnp.where(okn, jnp.exp(sn - mw), 0.0)
    lw = jnp.maximum(jnp.sum(pw, axis=1, keepdims=True) + jnp.sum(pn, axis=1, keepdims=True), TINY)
    _diag_rows(ow_ref, (_dot(pw.astype(BF16), vw) + _dot(pn.astype(BF16), vn)) / lw)


def _ns_attend(qn, kcw, state_win, wkv_new, rbt, layer, past):
    nb, tq, _ = qn.shape
    npad = kcw.shape[2]
    wb = state_win.shape[2]
    seq = lambda b: (b, 0, 0)
    return pl.pallas_call(
        functools.partial(_ns_attend_kernel, past=past),
        grid=(nb,),
        in_specs=[pl.BlockSpec((1, tq, NSA_W), seq),
                  pl.BlockSpec((1, 2, npad, NSA_W), lambda b: (b, 0, 0, 0)),
                  pl.BlockSpec((None, None, wb, 2 * LANES), lambda b: (layer, b, 0, 0)),
                  pl.BlockSpec((1, tq, 2 * LANES), seq),
                  pl.BlockSpec((tq * 8, N_BUCKETS), lambda b: (0, 0))],
        out_specs=[pl.BlockSpec((1, tq, NSA_W), seq), pl.BlockSpec((1, tq, NSA_W), seq),
                   pl.BlockSpec((1, tq * 8, LANES), seq)],
        out_shape=[jax.ShapeDtypeStruct((nb, tq, NSA_W), F32), jax.ShapeDtypeStruct((nb, tq, NSA_W), F32),
                   jax.ShapeDtypeStruct((nb, tq * 8, LANES), BF16)],
        scratch_shapes=[pltpu.VMEM((PAGE, 2 * LANES), F32)],
        compiler_params=_cparams(("arbitrary",)),
        name="ns_attend",
    )(qn, kcw, state_win, wkv_new, rbt)


def _ns_select_kernel(pt_tab, *refs, past):
    P = PAGES_PER_STEP
    q_ref, mb_ref, nn_ref, rbt_ref = refs[0:4]
    pg_refs = refs[4:4 + P]
    o_ref, q_sc, m_ref, l_ref, acc_ref, new_sc = refs[4 + P:]
    s = pl.program_id(1)
    tq = q_ref.shape[1]
    nr = tq * 8

    @pl.when(s == 0)
    def _():
        r = _iota((NSA_W, LANES), 0)
        c = _iota((NSA_W, LANES), 1)
        fold = jnp.where(c == _div(r, NSA_GH * HD) * HD + _mod(r, HD), 1.0, 0.0).astype(BF16)
        q_sc[...] = _dot(_rows_th(q_ref[0]), fold).astype(BF16)
        _softmax_init(m_ref, l_ref, acc_ref)
        new_sc[...] = jnp.zeros_like(new_sc)
        new_sc[0:tq, :] = nn_ref[0][:, 2 * LANES:]

    qq = q_sc[...]
    trow = _div(_iota((nr, PAGE), 0), 8)
    key = _iota((nr, PAGE), 1)

    def attend(tiles, biases):
        sc = jnp.concatenate([_dot_nt(qq, x[:, 0:LANES].astype(BF16)) for x in tiles], axis=1) + biases
        m_old = m_ref[...]
        m_new = jnp.maximum(m_old, jnp.max(sc, axis=1, keepdims=True))
        alpha = jnp.exp(m_old - m_new)
        p = jnp.exp(sc - m_new)
        l_ref[...] = alpha * l_ref[...] + jnp.sum(p, axis=1, keepdims=True)
        acc = alpha * acc_ref[...]
        for k, x in enumerate(tiles):
            acc = acc + _dot(p[:, k * PAGE:(k + 1) * PAGE].astype(BF16), x[:, LANES:2 * LANES].astype(BF16))
        acc_ref[...] = acc
        m_ref[...] = m_new

    blk = _iota((LANES, P * PAGE), 0)
    kcol = _iota((LANES, P * PAGE), 1)
    expand = jnp.where(blk == _div(s * (P * PAGE) + kcol, SEL_LEN), 1.0, 0.0).astype(BF16)
    bias = _dot(mb_ref[0], expand) + rbt_ref[:, N_BUCKETS - 1:N_BUCKETS]
    tiles = [r[...] for r in pg_refs]

    @pl.when(s < pl.num_programs(1) - 1)
    def _():
        attend(tiles, bias)

    @pl.when(s == pl.num_programs(1) - 1)
    def _():
        d_last = past + trow - (past - PAGE + key)
        b_last = _bias_lookup(_bucket(jnp.clip(d_last, 0, MAX_DISTANCE - 1)), _rb_col(rbt_ref))
        fix = jnp.concatenate([jnp.zeros((nr, (P - 1) * PAGE), F32),
                               b_last - rbt_ref[:, N_BUCKETS - 1:N_BUCKETS]], axis=1)
        attend(tiles, bias + fix)
        d_new = trow - key
        b_new = _bias_lookup(_bucket(jnp.clip(d_new, 0, MAX_DISTANCE - 1)), _rb_col(rbt_ref))
        attend([new_sc[...]], jnp.where((d_new >= 0) & (key < tq), b_new, NEG))
        a = acc_ref[...] / jnp.maximum(l_ref[...], TINY)
        place = _place_wide()
        _diag_rows(o_ref, _dot2_l(a, place))


def _ns_select(qn, maskb, nkv_new, rbt, cache_nsa, page_table, layer, past):
    nb, npages = page_table.shape
    tq = qn.shape[1]
    P = PAGES_PER_STEP
    seq = lambda b, s, pt: (b, 0, 0)

    def page_spec(k):
        return pl.BlockSpec((None, None, PAGE, 2 * LANES), lambda b, s, pt: (layer, pt[b, s * P + k], 0, 1))

    return pl.pallas_call(
        functools.partial(_ns_select_kernel, past=past),
        grid_spec=pltpu.PrefetchScalarGridSpec(
            num_scalar_prefetch=1,
            grid=(nb, npages // P),
            in_specs=[pl.BlockSpec((1, tq, NSA_W), seq), pl.BlockSpec((1, tq * 8, LANES), seq),
                      pl.BlockSpec((1, tq, 4 * LANES), seq),
                      pl.BlockSpec((tq * 8, N_BUCKETS), lambda b, s, pt: (0, 0))]
            + [page_spec(k) for k in range(P)],
            out_specs=pl.BlockSpec((1, tq, NSA_W), seq),
            scratch_shapes=[pltpu.VMEM((tq * 8, LANES), BF16),
                            pltpu.VMEM((tq * 8, 1), F32), pltpu.VMEM((tq * 8, 1), F32),
                            pltpu.VMEM((tq * 8, LANES), F32),
                            pltpu.VMEM((PAGE, 2 * LANES), F32)]),
        out_shape=jax.ShapeDtypeStruct((nb, tq, NSA_W), F32),
        compiler_params=_cparams(("arbitrary", "arbitrary")),
        name="ns_select",
    )(page_table, qn, maskb, nkv_new, rbt, *([cache_nsa] * P))


def _prep_w_in(w):
    d = w.shape[0]
    o_logf = 3 * FOX_W
    o_qn = o_logf + H_FOX
    o_kv = o_qn + NSA_W
    o_gate = o_kv + 6 * NSA_KV * HD
    misc = jnp.concatenate([w[:, o_logf:o_qn], w[:, o_gate:o_gate + 3 * H_NSA],
                            jnp.zeros((d, LANES - H_FOX - 3 * H_NSA), w.dtype)], axis=1)
    return jnp.concatenate([w[:, :o_logf], w[:, o_qn:o_kv], w[:, o_kv:o_gate], misc], axis=1).astype(BF16)


def _prep_cmp(pos, w1, b1, w2):
    eye = jnp.eye(NSA_KV, dtype=w1.dtype)
    w1r = w1.reshape(2, 2, CMP_STRIDE, HD, CMP_HID)
    wide = jnp.einsum("whjdc,ab->whjadbc", w1r, eye).reshape(2, 2, CMP_STRIDE * LANES, 2 * CMP_HID)
    posr = jnp.broadcast_to(pos.reshape(2, 2, CMP_STRIDE, 1, HD), (2, 2, CMP_STRIDE, NSA_KV, HD))
    posr = posr.reshape(2, 2, 1, CMP_STRIDE * LANES)
    group_of_head = (jnp.arange(H_NSA) // NSA_GH)[None, :] == jnp.arange(NSA_KV)[:, None]
    w2w = jnp.einsum("wcd,ah->wachd", w2, group_of_head.astype(w2.dtype)).reshape(2, 2 * CMP_HID, NSA_W)
    return {"pt": posr[:, 0], "pb": posr[:, 1],
            "w1t": wide[:, 0].astype(BF16), "w1b": wide[:, 1].astype(BF16),
            "b1": jnp.concatenate([b1, b1], axis=-1)[:, None, :],
            "w2w": w2w.astype(BF16)}


def _far_bias_rows(rel_bias):
    far = rel_bias[N_BUCKETS - 1]
    hi = far.astype(BF16).astype(F32)
    r = far - hi
    mid = r.astype(BF16).astype(F32)
    lo = (r - mid).astype(BF16).astype(F32)
    rows = jnp.zeros((H_NSA, LANES), F32)
    rows = rows.at[:, HD].set(hi).at[:, HD + 1].set(mid).at[:, HD + 2].set(lo)
    return rows.reshape(NSA_KV, NSA_GH, LANES)


def kernel(x_prompt, x_sample, cache_fox_kv, cache_fox_logf, cache_nsa_kv, state_win_kv, state_conv,
           page_table, norm1_g, w_in, b_forget, cmp_pos, cmp_w1, cmp_b1, cmp_w2, out_norm_g, w_out,
           norm2_g, w_gu, conv_w, conv_b, w_down, rel_bias, final_norm_g):
    B, S, D = x_prompt.shape
    nb, tq, _ = x_sample.shape
    depth = w_in.shape[0]
    n_pool = cache_fox_kv.shape[1]
    npages = page_table.shape[1]
    past = npages * PAGE
    dff = w_down.shape[1]
    wb = state_win_kv.shape[2]
    assert tq & (tq - 1) == 0 and tq >= CONV_W - 1
    assert S % 256 == 0 and S // SEL_LEN <= LANES and past // SEL_LEN <= LANES
    assert npages % PAGES_PER_STEP == 0 and wb == WINDOW and past >= WINDOW and tq * 8 <= LANES

    cache_kv = cache_fox_kv.astype(BF16).reshape(depth, n_pool, PAGE, 2 * FOX_W)
    cache_lft = jnp.swapaxes(cache_fox_logf, 2, 3)
    cache_nsa = cache_nsa_kv.reshape(depth, n_pool, PAGE, 4 * NSA_KV * HD)
    state_win = state_win_kv.reshape(depth, nb, wb, 2 * NSA_KV * HD)
    rbt = jnp.tile(rel_bias.T, (tq, 1))
    brow = _far_bias_rows(rel_bias)
    gf = final_norm_g.reshape(1, D)

    xp = x_prompt.reshape(B * S, D)
    xs = x_sample.reshape(nb * tq, D)
    outs = [[] for _ in range(10)]
    for l in range(depth):
        w_l = _prep_w_in(w_in[l])
        bf = jnp.zeros((1, LANES), F32).at[0, :H_FOX].set(b_forget[l])
        cw = _prep_cmp(cmp_pos[l], cmp_w1[l], cmp_b1[l], cmp_w2[l])
        g1 = norm1_g[l].reshape(1, D)
        g2 = norm2_g[l].reshape(1, D)
        go = out_norm_g[l].reshape(1, D)
        wo = w_out[l].astype(BF16)
        wgu = w_gu[l].astype(BF16)
        wd = w_down[l].astype(BF16)
        cb = conv_b[l].reshape(1, dff)
        final = l == depth - 1

        qf, fkv, qn, nkv, wkv, misc = _proj(xp, g1, w_l, bf)
        r3 = lambda a: a.reshape(B, S, a.shape[-1])
        qa, ka, vb = _fox_prep(r3(qf), r3(fkv), r3(misc))
        o_fox = _fox_flash(qa, ka, jnp.swapaxes(vb, 1, 2))
        kcw = _compress_prompt(r3(nkv), cw)
        ksa, kwa, vwd = _nsa_prep(r3(nkv), r3(wkv))
        vst = r3(nkv)[:, :, 3 * LANES:].astype(BF16).reshape(B, S, NSA_KV, HD).transpose(0, 2, 3, 1)
        o_c, maskq = _nsa_cmp(rel_bias, r3(qn), kcw)
        o_s = _nsa_sel(rel_bias, r3(qn), maskq, brow, ksa, vst)
        padw = ((0, 0), (0, 0), (WINDOW, 0), (0, 0))
        o_w = _nsa_win(rel_bias, r3(qn), jnp.pad(kwa, padw), jnp.pad(vwd, padw))
        f2 = lambda a: a.reshape(B * S, a.shape[-1])
        xp = _merge(xp, f2(o_fox), f2(o_c), f2(o_s), f2(o_w), misc, go, wo)
        xp, gtail = _ffn(xp, g2, wgu, conv_w[l], cb, wd, gf, seq_len=S, final=final)
        tiles_per_seq = gtail.shape[0] // B
        conv_p = gtail.reshape(B, tiles_per_seq, 8, dff)[:, -1, 8 - (CONV_W - 1):, :]
        outs[0].append(fkv.reshape(B, S, 2, H_FOX, HD))
        outs[2].append(misc[:, :H_FOX].reshape(B, S, H_FOX))
        outs[4].append(nkv.reshape(B, S, 4, NSA_KV, HD))
        outs[6].append(wkv.reshape(B, S, 2, NSA_KV, HD)[:, -min(WINDOW, S):])
        outs[8].append(conv_p)

        qf, fkv, qn, nkv, wkv, misc = _proj(xs, g1, w_l, bf)
        s3 = lambda a: a.reshape(nb, tq, a.shape[-1])
        lfn = jnp.swapaxes(s3(misc)[:, :, :H_FOX], 1, 2)
        lfn = jnp.pad(lfn, ((0, 0), (0, 0), (0, PAGE - tq)))
        o_fox = _fox_sample(s3(qf), s3(fkv), lfn, cache_kv, cache_lft, page_table, l)
        kcw = _compress_sample(cache_nsa, page_table, l, cw)
        o_c, o_w, maskb = _ns_attend(s3(qn), kcw, state_win, s3(wkv), rbt, l, past)
        o_s = _ns_select(s3(qn), maskb, s3(nkv), rbt, cache_nsa, page_table, l, past)
        s2 = lambda a: a.reshape(nb * tq, a.shape[-1])
        xs = _merge(xs, s2(o_fox), s2(o_c), s2(o_s), s2(o_w), misc, go, wo)
        hist = state_conv[l]
        zero = jnp.zeros((nb, 1, dff), F32)
        hm1 = jnp.concatenate([hist[:, 1:2], zero, zero, zero][:tq], axis=1).reshape(nb * tq, dff)
        hm2 = jnp.concatenate([hist[:, 0:1], hist[:, 1:2], zero, zero][:tq], axis=1).reshape(nb * tq, dff)
        xs, gfull = _ffn(xs, g2, wgu, conv_w[l], cb, wd, gf, seq_len=None, final=final, hist=(hm1, hm2),
                         short_len=tq)
        win_new = jnp.concatenate([state_win_kv[l], wkv.reshape(nb, tq, 2, NSA_KV, HD)], axis=1)[:, -wb:]
        outs[1].append(fkv.reshape(nb, tq, 2, H_FOX, HD))
        outs[3].append(misc[:, :H_FOX].reshape(nb, tq, H_FOX))
        outs[5].append(nkv.reshape(nb, tq, 4, NSA_KV, HD))
        outs[7].append(win_new)
        outs[9].append(gfull.reshape(nb, tq, dff)[:, -(CONV_W - 1):])

    st = [jnp.stack(o) for o in outs]
    return (xp.reshape(B, S, D), xs.reshape(nb, tq, D),
            st[0], st[1], st[2], st[3], st[4], st[5], st[6], st[7], st[8], st[9])
```

```python
import functools
import math

import numpy as np
import jax
import jax.numpy as jnp
from jax import lax
from jax.experimental import pallas as pl
from jax.experimental.pallas import tpu as pltpu

F32 = jnp.float32
BF16 = jnp.bfloat16
I32 = jnp.int32

HD = 64
H_FOX = 8
H_NSA = 8
NSA_KV = 2
NSA_GH = 4
FOX_W = H_FOX * HD
NSA_W = H_NSA * HD
CMP_LEN = 32
CMP_STRIDE = 16
CMP_HID = 256
SEL_LEN = 64
N_SELECT = 16
WINDOW = 512
N_BUCKETS = 32
MAX_DISTANCE = 128
CONV_W = 3
PAGE = 128
EPS = 1e-6
NEG = -1e30
TINY = 1e-30
FORCE_SCORE = 1e4
SCALE = HD ** -0.5
LOG2E = math.log2(math.e)
ONES_ROWS = 8

LANES = 128
VMEM_LIMIT = 56 * 1024 * 1024

C_QF, C_FKV, C_QN, C_NKV, C_WKV, C_MISC, C_END = 0, 512, 1536, 2048, 2560, 2816, 2944
MISC_LOGF = 0
MISC_GATE = 8

AUG0 = HD
AUG1 = HD + 3

PAGES_PER_STEP = 16


def _bucket_thresholds():
    exact = N_BUCKETS // 2
    n = np.arange(1, 4 * MAX_DISTANCE, dtype=np.float64)
    far = exact + (np.log(n / exact) / math.log(MAX_DISTANCE / exact) * (N_BUCKETS - exact)).astype(np.int64)
    b = np.where(n < exact, n, np.minimum(far, N_BUCKETS - 1)).astype(np.int64)
    return [int(n[b >= k].min()) for k in range(exact + 1, N_BUCKETS)]


_THR = _bucket_thresholds()


def _cparams(sem):
    return pltpu.CompilerParams(dimension_semantics=sem, vmem_limit_bytes=VMEM_LIMIT)


def _dot(a, b):
    return jnp.dot(a, b, preferred_element_type=F32)


def _dot_nt(a, b):
    return lax.dot_general(a, b, (((1,), (1,)), ((), ())), preferred_element_type=F32)


def _split2(x):
    hi = x.astype(BF16)
    lo = (x - hi.astype(F32)).astype(BF16)
    return hi, lo


def _split3(x):
    hi = x.astype(BF16)
    r = x - hi.astype(F32)
    mid = r.astype(BF16)
    lo = (r - mid.astype(F32)).astype(BF16)
    return hi, mid, lo


def _dot3_l(x, m):
    hi, mid, lo = _split3(x)
    return _dot(hi, m) + _dot(mid, m) + _dot(lo, m)


def _dot3_r(m, x):
    hi, mid, lo = _split3(x)
    return _dot(m, hi) + _dot(m, mid) + _dot(m, lo)


def _dot2_l(x, m):
    hi, lo = _split2(x)
    return _dot(hi, m) + _dot(lo, m)


def _iota(shape, dim):
    return lax.broadcasted_iota(I32, shape, dim)


def _div(x, k):
    return lax.shift_right_arithmetic(x, jnp.int32(k.bit_length() - 1))


def _mod(x, k):
    return x & (k - 1)


def _rms(x, g):
    r = lax.rsqrt(jnp.mean(x * x, axis=-1, keepdims=True) + EPS)
    return (x * r) * g


def _bucket(d):
    far = jnp.full(d.shape, N_BUCKETS // 2, I32)
    for thr in _THR:
        far = far + (d >= thr).astype(I32)
    return jnp.where(d < N_BUCKETS // 2, d, far)


def _bias_lookup(bucket, rb_get):
    acc = jnp.zeros(bucket.shape, F32)
    for b in range(N_BUCKETS):
        acc = jnp.where(bucket == b, rb_get(b), acc)
    return acc


def _topk_mask(score, k, axis=1):
    idx = _iota(score.shape, axis).astype(F32)

    def body(_, c):
        work, sel = c
        mx = jnp.max(work, axis=axis, keepdims=True)
        first = jnp.min(jnp.where(work == mx, idx, 1e9), axis=axis, keepdims=True)
        hit = idx == first
        return jnp.where(hit, -3.0, work), jnp.where(hit, 1.0, sel)

    _, sel = lax.fori_loop(0, k, body, (score, jnp.zeros(score.shape, F32)))
    return sel


def _proj_kernel(x_ref, g_ref, w_ref, bf_ref, qf_ref, fkv_ref, qn_ref, nkv_ref, wkv_ref, misc_ref):
    h = _rms(x_ref[...], g_ref[...]).astype(BF16)
    qf_ref[...] = _dot(h, w_ref[:, C_QF:C_FKV])
    fkv_ref[...] = _dot(h, w_ref[:, C_FKV:C_QN])
    qn_ref[...] = _dot(h, w_ref[:, C_QN:C_NKV])
    nkv_ref[...] = _dot(h, w_ref[:, C_NKV:C_WKV])
    wkv_ref[...] = _dot(h, w_ref[:, C_WKV:C_MISC])
    z = _dot(h, w_ref[:, C_MISC:C_END]) + bf_ref[...]
    lane = _iota(z.shape, 1)
    logsig = jnp.minimum(z, 0.0) - jnp.log(1.0 + jnp.exp(-jnp.abs(z)))
    sig = 1.0 / (1.0 + jnp.exp(-z))
    misc_ref[...] = jnp.where(lane < MISC_GATE, logsig, jnp.where(lane < MISC_GATE + 3 * H_NSA, sig, 0.0))


def _proj(x2, g, w, bf):
    T, D = x2.shape
    tm = min(512, T)
    widths = (C_FKV - C_QF, C_QN - C_FKV, C_NKV - C_QN, C_WKV - C_NKV, C_MISC - C_WKV, C_END - C_MISC)
    return pl.pallas_call(
        _proj_kernel,
        grid=(T // tm,),
        in_specs=[pl.BlockSpec((tm, D), lambda i: (i, 0)),
                  pl.BlockSpec((1, D), lambda i: (0, 0)),
                  pl.BlockSpec((D, C_END), lambda i: (0, 0)),
                  pl.BlockSpec((1, LANES), lambda i: (0, 0))],
        out_specs=[pl.BlockSpec((tm, wd), lambda i: (i, 0)) for wd in widths],
        out_shape=[jax.ShapeDtypeStruct((T, wd), F32) for wd in widths],
        compiler_params=_cparams(("arbitrary",)),
        name="proj",
    )(x2, g, w, bf)


def _fox_prep_kernel(qf_ref, k_ref, v_ref, misc_ref, qa_ref, ka_ref, vb_ref, carry_ref):
    i = pl.program_id(1)
    tm = qf_ref.shape[1]

    @pl.when(i == 0)
    def _():
        carry_ref[...] = jnp.zeros_like(carry_ref)

    lane = _iota((tm, LANES), 1)
    lf = jnp.where(lane < H_FOX, misc_ref[0], 0.0)
    tril = (_iota((tm, tm), 0) >= _iota((tm, tm), 1)).astype(BF16)
    cum = _dot3_r(tril, lf) + carry_ref[0:1, :]
    carry_ref[...] = jnp.broadcast_to(cum[tm - 1:tm, :], carry_ref.shape)
    cl2 = cum * LOG2E
    hi = cl2.astype(BF16).astype(F32)
    r = cl2 - hi
    mid = r.astype(BF16).astype(F32)
    lo = (r - mid).astype(BF16).astype(F32)
    cc = (hi + pltpu.roll(mid, H_FOX, 1) + pltpu.roll(lo, 2 * H_FOX, 1)).astype(BF16)
    er = _iota((LANES, H_FOX * LANES), 0)
    ec = _iota((LANES, H_FOX * LANES), 1)
    part = _div(er, H_FOX)
    head = _mod(er, H_FOX)
    inb = er < 3 * H_FOX
    eq = jnp.where(inb & (ec == head * LANES + AUG0 + part), 1.0, 0.0).astype(BF16)
    ek = jnp.where(inb & (ec == head * LANES + AUG1 + part), -1.0, 0.0).astype(BF16)
    cl = _mod(_iota((1, H_FOX * LANES), 1), LANES)
    ones_q = jnp.where((cl >= AUG1) & (cl < AUG1 + 3), 1.0, 0.0)
    ones_k = jnp.where((cl >= AUG0) & (cl < AUG0 + 3), 1.0, 0.0)
    cols_q = _dot(cc, eq) + ones_q
    cols_k = _dot(cc, ek) + ones_k
    q = qf_ref[0]
    k = k_ref[0]
    for h in range(H_FOX):
        a = (h // 2) * LANES
        qt = q[:, a:a + LANES]
        kt = k[:, a:a + LANES]
        if h % 2:
            qt = pltpu.roll(qt, HD, 1)
            kt = pltpu.roll(kt, HD, 1)
        qa_ref[0, h] = jnp.where(lane < HD, qt * (SCALE * LOG2E), cols_q[:, h * LANES:(h + 1) * LANES]).astype(BF16)
        ka_ref[0, h] = jnp.where(lane < HD, kt, cols_k[:, h * LANES:(h + 1) * LANES]).astype(BF16)
    vb_ref[0] = v_ref[0].astype(BF16)


def _fox_prep(qf, fkv, misc):
    B, S, _ = qf.shape
    tm = min(512, S)
    return pl.pallas_call(
        _fox_prep_kernel,
        grid=(B, S // tm),
        in_specs=[pl.BlockSpec((1, tm, FOX_W), lambda b, i: (b, i, 0)),
                  pl.BlockSpec((1, tm, FOX_W), lambda b, i: (b, i, 0)),
                  pl.BlockSpec((1, tm, FOX_W), lambda b, i: (b, i, 1)),
                  pl.BlockSpec((1, tm, LANES), lambda b, i: (b, i, 0))],
        out_specs=[pl.BlockSpec((1, H_FOX, tm, LANES), lambda b, i: (b, 0, i, 0)),
                   pl.BlockSpec((1, H_FOX, tm, LANES), lambda b, i: (b, 0, i, 0)),
                   pl.BlockSpec((1, tm, FOX_W), lambda b, i: (b, i, 0))],
        out_shape=[jax.ShapeDtypeStruct((B, H_FOX, S, LANES), BF16),
                   jax.ShapeDtypeStruct((B, H_FOX, S, LANES), BF16),
                   jax.ShapeDtypeStruct((B, S, FOX_W), BF16)],
        scratch_shapes=[pltpu.VMEM((8, LANES), F32)],
        compiler_params=_cparams(("arbitrary", "arbitrary")),
        name="fox_prep",
    )(qf, fkv, fkv, misc)


def _softmax_update(s, v, m_ref, l_ref, acc_ref):
    m_old = m_ref[...]
    m_new = jnp.maximum(m_old, jnp.max(s, axis=1, keepdims=True))
    alpha = jnp.exp(m_old - m_new)
    p = jnp.exp(s - m_new)
    l_ref[...] = alpha * l_ref[...] + jnp.sum(p, axis=1, keepdims=True)
    acc_ref[...] = alpha * acc_ref[...] + _dot(p.astype(BF16), v)
    m_ref[...] = m_new


def _softmax_init(m_ref, l_ref, acc_ref):
    m_ref[...] = jnp.full(m_ref.shape, NEG, F32)
    l_ref[...] = jnp.zeros(l_ref.shape, F32)
    acc_ref[...] = jnp.zeros(acc_ref.shape, F32)


def _softmax_update_t(s, vt, m_ref, acc_ref):
    m_old = m_ref[...]
    m_new = jnp.maximum(m_old, jnp.max(s, axis=0, keepdims=True))
    alpha = jnp.exp2(m_old - m_new)
    p = jnp.exp2(s - m_new).astype(BF16)
    acc_ref[...] = alpha * acc_ref[...] + _dot(vt, p)
    m_ref[...] = m_new


def _softmax_init_t(m_ref, acc_ref):
    m_ref[...] = jnp.full(m_ref.shape, NEG, F32)
    acc_ref[...] = jnp.zeros(acc_ref.shape, F32)


def _fox_flash_kernel(qa_ref, ka_ref, vt_ref, o_ref, m_ref, acc_ref):
    qi = pl.program_id(2)
    tq = qa_ref.shape[2]
    _softmax_init_t(m_ref, acc_ref)

    def tile(kt, diag):
        ks = pl.multiple_of(kt * tq, tq)
        vt = vt_ref[0, 0, :, pl.ds(ks, tq)]
        for hh in range(2):
            s = _dot_nt(ka_ref[0, hh, pl.ds(ks, tq), :], qa_ref[0, hh])
            if diag:
                s = jnp.where(_iota(s.shape, 0) <= _iota(s.shape, 1), s, NEG)
            _softmax_update_t(s, vt, m_ref.at[hh], acc_ref.at[hh])

    def full_tile(kt, c):
        tile(kt, False)
        return c

    lax.fori_loop(0, qi, full_tile, 0)
    tile(qi, True)
    halves = []
    for hh in range(2):
        a = acc_ref[hh]
        halves.append(a[hh * HD:(hh + 1) * HD, :] / jnp.maximum(a[2 * HD:2 * HD + 1, :], TINY))
    o_ref[0] = jnp.concatenate(halves, axis=0).T


def _fox_flash(qa, ka, vt):
    B, H, S, _ = qa.shape
    tq = min(512, S)
    rows = vt.shape[2]
    return pl.pallas_call(
        _fox_flash_kernel,
        grid=(B, H // 2, S // tq),
        in_specs=[pl.BlockSpec((1, 2, tq, LANES), lambda b, p, i: (b, p, i, 0)),
                  pl.BlockSpec((1, 2, S, LANES), lambda b, p, i: (b, p, 0, 0)),
                  pl.BlockSpec((1, 1, rows, S), lambda b, p, i: (b, p, 0, 0))],
        out_specs=pl.BlockSpec((1, tq, LANES), lambda b, p, i: (b, i, p)),
        out_shape=jax.ShapeDtypeStruct((B, S, FOX_W), F32),
        scratch_shapes=[pltpu.VMEM((2, 1, tq), F32), pltpu.VMEM((2, rows, tq), F32)],
        compiler_params=_cparams(("arbitrary", "arbitrary", "arbitrary")),
        name="fox_flash",
    )(qa, ka, vt)


def _gelu_tanh(x):
    return 0.5 * x * (1.0 + jnp.tanh(math.sqrt(2.0 / math.pi) * (x + 0.044715 * (x * x * x))))


def _compress_core(x, pt, pb, w1t, w1b, b1, w2w):
    n = x.shape[0]
    a = _dot((x + pt).astype(BF16), w1t)
    b = _dot((x + pb).astype(BF16), w1b)
    h = a + pltpu.roll(b, n - 1, 0) + b1
    return _dot(_gelu_tanh(h).astype(BF16), w2w)


CMP_PAD_FRONT = 16
CMP_PAD_BACK = 112


def _store_cmp(out_ref, idx, res, n):
    out_ref[idx] = jnp.zeros(out_ref.shape[len(idx):], BF16)
    out_ref[idx + (slice(CMP_PAD_FRONT, CMP_PAD_FRONT + n), slice(None))] = res.astype(BF16)


def _compress_prompt_kernel(x_ref, pt_ref, pb_ref, w1t_ref, w1b_ref, b1_ref, w2w_ref, out_ref):
    n = x_ref.shape[1] // CMP_STRIDE
    x = jnp.concatenate([x_ref[0, pl.ds(j, n, stride=CMP_STRIDE), :] for j in range(CMP_STRIDE)], axis=1)
    res = _compress_core(x, pt_ref[0], pb_ref[0], w1t_ref[0], w1b_ref[0], b1_ref[0], w2w_ref[0])
    _store_cmp(out_ref, (0, 0), res, n)


def _compress_prompt(nkv, cw):
    B, S, _ = nkv.shape
    n = S // CMP_STRIDE
    npad = n + CMP_PAD_FRONT + CMP_PAD_BACK
    kx = CMP_STRIDE * LANES
    return pl.pallas_call(
        _compress_prompt_kernel,
        grid=(B, 2),
        in_specs=[pl.BlockSpec((1, S, LANES), lambda b, w: (b, 0, w)),
                  pl.BlockSpec((1, 1, kx), lambda b, w: (w, 0, 0)),
                  pl.BlockSpec((1, 1, kx), lambda b, w: (w, 0, 0)),
                  pl.BlockSpec((1, kx, 2 * CMP_HID), lambda b, w: (w, 0, 0)),
                  pl.BlockSpec((1, kx, 2 * CMP_HID), lambda b, w: (w, 0, 0)),
                  pl.BlockSpec((1, 1, 2 * CMP_HID), lambda b, w: (w, 0, 0)),
                  pl.BlockSpec((1, 2 * CMP_HID, NSA_W), lambda b, w: (w, 0, 0))],
        out_specs=pl.BlockSpec((1, 1, npad, NSA_W), lambda b, w: (b, w, 0, 0)),
        out_shape=jax.ShapeDtypeStruct((B, 2, npad, NSA_W), BF16),
        compiler_params=_cparams(("arbitrary", "arbitrary")),
        name="compress_prompt",
    )(nkv, cw["pt"], cw["pb"], cw["w1t"], cw["w1b"], cw["b1"], cw["w2w"])


def _compress_sample_kernel(pt_tab, *refs):
    P = PAGES_PER_STEP
    pages = (refs[:P], refs[P:2 * P])
    pt_ref, pb_ref, w1t_ref, w1b_ref, b1_ref, w2w_ref, out_ref, x_sc = refs[2 * P:]
    s = pl.program_id(1)
    n = x_sc.shape[1]
    rows = P * (PAGE // CMP_STRIDE)
    r0 = pl.multiple_of(s * rows, rows)
    for w in range(2):
        for j in range(CMP_STRIDE):
            x_sc[w, pl.ds(r0, rows), j * LANES:(j + 1) * LANES] = jnp.concatenate(
                [pg[pl.ds(j, PAGE // CMP_STRIDE, stride=CMP_STRIDE), :] for pg in pages[w]], axis=0)

    @pl.when(s == pl.num_programs(1) - 1)
    def _():
        for w in range(2):
            res = _compress_core(x_sc[w], pt_ref[w], pb_ref[w], w1t_ref[w], w1b_ref[w], b1_ref[w], w2w_ref[w])
            _store_cmp(out_ref, (0, w), res, n)


def _compress_sample(cache_nsa, page_table, layer, cw):
    nb, npages = page_table.shape
    P = PAGES_PER_STEP
    n = npages * (PAGE // CMP_STRIDE)
    npad = n + CMP_PAD_FRONT + CMP_PAD_BACK
    kx = CMP_STRIDE * LANES

    def page_spec(k, w):
        return pl.BlockSpec((None, None, PAGE, LANES), lambda b, s, pt: (layer, pt[b, s * P + k], 0, w))

    const3 = lambda b, s, pt: (0, 0, 0)
    return pl.pallas_call(
        _compress_sample_kernel,
        grid_spec=pltpu.PrefetchScalarGridSpec(
            num_scalar_prefetch=1,
            grid=(nb, npages // P),
            in_specs=[page_spec(k, 0) for k in range(P)] + [page_spec(k, 1) for k in range(P)] + [
                pl.BlockSpec((2, 1, kx), const3), pl.BlockSpec((2, 1, kx), const3),
                pl.BlockSpec((2, kx, 2 * CMP_HID), const3), pl.BlockSpec((2, kx, 2 * CMP_HID), const3),
                pl.BlockSpec((2, 1, 2 * CMP_HID), const3), pl.BlockSpec((2, 2 * CMP_HID, NSA_W), const3)],
            out_specs=pl.BlockSpec((1, 2, npad, NSA_W), lambda b, s, pt: (b, 0, 0, 0)),
            scratch_shapes=[pltpu.VMEM((2, n, kx), F32)]),
        out_shape=jax.ShapeDtypeStruct((nb, 2, npad, NSA_W), BF16),
        compiler_params=_cparams(("arbitrary", "arbitrary")),
        name="compress_sample",
    )(page_table, *([cache_nsa] * (2 * P)), cw["pt"], cw["pb"], cw["w1t"], cw["w1b"], cw["b1"], cw["w2w"])


def _nsa_prep_kernel(sel_ref, win_ref, ksa_ref, kwa_ref, vwd_ref):
    i = pl.program_id(1)
    tm = sel_ref.shape[1]
    lane = _iota((tm, LANES), 1)
    pos = i * tm + _iota((tm, LANES), 0)
    onehot = jnp.where(_div(pos, SEL_LEN) == lane, 1.0, 0.0)
    ks = sel_ref[0][:, 0:LANES]
    w = win_ref[0]
    kw = w[:, 0:LANES]
    vw = w[:, LANES:2 * LANES]
    rks = pltpu.roll(ks, HD, 1)
    rkw = pltpu.roll(kw, HD, 1)
    rvw = pltpu.roll(vw, HD, 1)
    for g in range(NSA_KV):
        left = jnp.where(lane < HD, ks if g == 0 else rks, jnp.where(lane < HD + 3, 1.0, 0.0))
        ksa_ref[0, g] = jnp.concatenate([left, onehot], axis=1).astype(BF16)
        kwa_ref[0, g] = jnp.where(lane < HD, kw if g == 0 else rkw, 0.0).astype(BF16)
        vd = jnp.where(lane < HD, vw, rvw) if g == 0 else jnp.where(lane < HD, rvw, vw)
        vwd_ref[0, g] = vd.astype(BF16)


def _nsa_prep(nkv, wkv):
    B, S, _ = nkv.shape
    tm = min(512, S)
    spec128 = pl.BlockSpec((1, NSA_KV, tm, LANES), lambda b, i: (b, 0, i, 0))
    return pl.pallas_call(
        _nsa_prep_kernel,
        grid=(B, S // tm),
        in_specs=[pl.BlockSpec((1, tm, 2 * LANES), lambda b, i: (b, i, 1)),
                  pl.BlockSpec((1, tm, 2 * LANES), lambda b, i: (b, i, 0))],
        out_specs=[pl.BlockSpec((1, NSA_KV, tm, 2 * LANES), lambda b, i: (b, 0, i, 0)), spec128, spec128],
        out_shape=[jax.ShapeDtypeStruct((B, NSA_KV, S, 2 * LANES), BF16),
                   jax.ShapeDtypeStruct((B, NSA_KV, S, LANES), BF16),
                   jax.ShapeDtypeStruct((B, NSA_KV, S, LANES), BF16)],
        compiler_params=_cparams(("arbitrary", "arbitrary")),
        name="nsa_prep",
    )(nkv, wkv)


def _overlap(i_blk, j_blk):
    start = i_blk * CMP_STRIDE
    return (start < j_blk * SEL_LEN + SEL_LEN) & (start + CMP_LEN > j_blk * SEL_LEN)


NEAR_BACK = 16


def _nsa_cmp_kernel(rb_ref, q_ref, kcw_ref, oc_ref, mq_ref, fc_ref):
    b = pl.program_id(0)
    i = pl.program_id(1)
    tq = q_ref.shape[1]
    n = kcw_ref.shape[2] - CMP_PAD_FRONT - CMP_PAD_BACK
    n_sel = n * CMP_STRIDE // SEL_LEN
    k_eff = min(N_SELECT, n_sel)

    @pl.when((b == 0) & (i == 0))
    def _():
        dist = _iota((tq, LANES), 0) + (NEAR_BACK * CMP_STRIDE - (CMP_LEN - 1)) - CMP_STRIDE * _iota((tq, LANES), 1)
        bk = _bucket(jnp.clip(dist, 0, MAX_DISTANCE - 1))
        for h in range(H_NSA):
            fc_ref[h] = jnp.where(dist >= 0, _bias_lookup(bk, lambda bb, h=h: rb_ref[bb, h]), NEG)

    qs = i * tq
    i0 = qs // CMP_STRIDE - NEAR_BACK
    q = q_ref[0] * SCALE
    lanehead = _div(_iota((tq, NSA_W), 1), HD)
    kfar = kcw_ref[0, 0, CMP_PAD_FRONT:CMP_PAD_FRONT + n, :]
    vfar = kcw_ref[0, 1, CMP_PAD_FRONT:CMP_PAD_FRONT + n, :]
    st = pl.multiple_of(qs // CMP_STRIDE, 16)
    knear = kcw_ref[0, 0, pl.ds(st, LANES), :]
    vnear = kcw_ref[0, 1, pl.ds(st, LANES), :]
    farmask = _iota((tq, n), 1) < i0
    nearmask = (_iota((tq, LANES), 1) + i0) >= 0
    oc = jnp.zeros((tq, NSA_W), F32)
    ps_far = [jnp.zeros((tq, n), F32) for _ in range(NSA_KV)]
    ps_near = [jnp.zeros((tq, LANES), F32) for _ in range(NSA_KV)]
    for h in range(H_NSA):
        g = h // NSA_GH
        qm = jnp.where(lanehead == h, q, 0.0).astype(BF16)
        sf = jnp.where(farmask, _dot_nt(qm, kfar) + rb_ref[N_BUCKETS - 1, h], NEG)
        sn = jnp.where(nearmask, _dot_nt(qm, knear) + fc_ref[h], NEG)
        m = jnp.maximum(jnp.max(sf, axis=1, keepdims=True), jnp.max(sn, axis=1, keepdims=True))
        pf = jnp.where(sf > 0.5 * NEG, jnp.exp(sf - m), 0.0)
        pn = jnp.where(sn > 0.5 * NEG, jnp.exp(sn - m), 0.0)
        l = jnp.sum(pf, axis=1, keepdims=True) + jnp.sum(pn, axis=1, keepdims=True)
        inv = 1.0 / jnp.maximum(l, TINY)
        pf = pf * inv
        pn = pn * inv
        o = _dot(pf.astype(BF16), vfar) + _dot(pn.astype(BF16), vnear)
        oc = jnp.where(lanehead == h, o, oc)
        ps_far[g] = ps_far[g] + pf
        ps_near[g] = ps_near[g] + pn
    oc_ref[0] = oc

    mov_far = jnp.where(_overlap(_iota((n, LANES), 0), _iota((n, LANES), 1)), 1.0, 0.0).astype(BF16)
    mov_near = jnp.where(_overlap(_iota((LANES, LANES), 0) + i0, _iota((LANES, LANES), 1)), 1.0, 0.0).astype(BF16)
    qpos = qs + _iota((tq, LANES), 0)
    jb = _iota((tq, LANES), 1)
    qblk = _div(qpos, SEL_LEN)
    valid = jb * SEL_LEN <= qpos
    forced = (jb == 0) | (jb == qblk) | (jb == qblk - 1)
    scores = []
    for g in range(NSA_KV):
        p_slc = _dot2_l(ps_far[g], mov_far) + _dot2_l(ps_near[g], mov_near)
        scores.append(jnp.where(forced, FORCE_SCORE, jnp.where(valid, p_slc, -1.0)))
    st = jnp.concatenate([sc.T for sc in scores], axis=1)
    keep = jnp.where((_topk_mask(st, k_eff, axis=0) > 0.5) & (st >= 0.0), 0.0, NEG)
    for g in range(NSA_KV):
        mq_ref[0, g] = keep[:, g * tq:(g + 1) * tq].T.astype(BF16)


def _nsa_cmp(rb, qn, kcw):
    B, S, _ = qn.shape
    tq = min(256, S)
    npad = kcw.shape[2]
    return pl.pallas_call(
        _nsa_cmp_kernel,
        grid=(B, S // tq),
        in_specs=[pl.BlockSpec(memory_space=pltpu.SMEM),
                  pl.BlockSpec((1, tq, NSA_W), lambda b, i: (b, i, 0)),
                  pl.BlockSpec((1, 2, npad, NSA_W), lambda b, i: (b, 0, 0, 0))],
        out_specs=[pl.BlockSpec((1, tq, NSA_W), lambda b, i: (b, i, 0)),
                   pl.BlockSpec((1, NSA_KV, tq, LANES), lambda b, i: (b, 0, i, 0))],
        out_shape=[jax.ShapeDtypeStruct((B, S, NSA_W), F32),
                   jax.ShapeDtypeStruct((B, NSA_KV, S, LANES), BF16)],
        scratch_shapes=[pltpu.VMEM((H_NSA, tq, LANES), F32)],
        compiler_params=_cparams(("arbitrary", "arbitrary")),
        name="nsa_cmp",
    )(rb, qn, kcw)


def _stack_heads(q, g_rows, extra, qs_ref, scale=SCALE):
    tq = q.shape[0]
    lane = _iota((tq, LANES), 1)
    for hh in range(NSA_GH):
        a = (hh // 2) * LANES
        t = q[:, a:a + LANES]
        if hh % 2:
            t = pltpu.roll(t, HD, 1)
        left = jnp.where(lane < HD, t * scale, g_rows[hh:hh + 1, :]).astype(BF16)
        if extra is None:
            qs_ref[hh * tq:(hh + 1) * tq, :] = left
        else:
            qs_ref[hh * tq:(hh + 1) * tq, :] = jnp.concatenate([left, extra], axis=1)


def _unstack_heads(a, tq):
    lane = _iota((tq, LANES), 1)
    p0 = jnp.where(lane < HD, a[0:tq], a[tq:2 * tq])
    p1 = jnp.where(lane < HD, a[2 * tq:3 * tq], a[3 * tq:4 * tq])
    return jnp.concatenate([p0, p1], axis=1)


SEL_BACK = 128


def _nsa_sel_kernel(rb_ref, q_ref, mq_ref, brow_ref, ksa_ref, vt_ref, o_ref, dn_ref, qs_ref, m_ref, acc_ref):
    b = pl.program_id(0)
    g = pl.program_id(1)
    i = pl.program_id(2)
    tq = q_ref.shape[1]
    wn = tq + SEL_BACK

    @pl.when((b == 0) & (g == 0) & (i == 0))
    def _():
        dist = _iota((wn, tq), 1) + SEL_BACK - _iota((wn, tq), 0)
        bk = _bucket(jnp.clip(dist, 0, MAX_DISTANCE - 1))
        for h in range(H_NSA):
            far = rb_ref[N_BUCKETS - 1, h]
            val = (_bias_lookup(bk, lambda bb, h=h: rb_ref[bb, h]) - far) * LOG2E
            dn_ref[h // NSA_GH, :, (h % NSA_GH) * tq:(h % NSA_GH + 1) * tq] = jnp.where(dist >= 0, val, NEG)

    qs = i * tq
    _stack_heads(q_ref[0], brow_ref[0], mq_ref[0, 0], qs_ref, scale=SCALE * LOG2E)
    _softmax_init_t(m_ref, acc_ref)
    hw = (NSA_GH // 2) * tq

    def tile(start, size, bias):
        k = ksa_ref[0, 0, pl.ds(start, size), :]
        vt = vt_ref[0, 0, :, pl.ds(start, size)]
        for half in range(2):
            s = _dot_nt(k, qs_ref[half * hw:(half + 1) * hw, :])
            if bias is not None:
                s = s + bias(half)
            _softmax_update_t(s, vt, m_ref.at[half], acc_ref.at[half])

    @pl.when(i == 0)
    def _():
        tile(0, tq, lambda half: dn_ref[g, SEL_BACK:, half * hw:(half + 1) * hw])

    @pl.when(i > 0)
    def _():
        tile(pl.multiple_of(qs - SEL_BACK, LANES), wn, lambda half: dn_ref[g, :, half * hw:(half + 1) * hw])
        tile(pl.multiple_of(qs - tq, LANES), tq - SEL_BACK, None)

    def far_tile(kt, c):
        tile(pl.multiple_of(kt * tq, tq), tq, None)
        return c

    lax.fori_loop(0, jnp.maximum(i - 1, 0), far_tile, 0)
    parts = []
    for half in range(2):
        acc = acc_ref[half]
        a = acc[0:HD, :] / jnp.maximum(acc[HD:HD + 1, :], TINY)
        parts += [a[:, 0:tq], a[:, tq:2 * tq]]
    o_ref[0] = jnp.concatenate(parts, axis=0).T


def _nsa_sel(rb, qn, maskq, brow, ksa, vst):
    B, S, _ = qn.shape
    tq = min(512, S)
    G = NSA_KV
    return pl.pallas_call(
        _nsa_sel_kernel,
        grid=(B, G, S // tq),
        in_specs=[pl.BlockSpec(memory_space=pltpu.SMEM),
                  pl.BlockSpec((1, tq, 2 * LANES), lambda b, g, i: (b, i, g)),
                  pl.BlockSpec((1, 1, tq, LANES), lambda b, g, i: (b, g, i, 0)),
                  pl.BlockSpec((1, NSA_GH, LANES), lambda b, g, i: (g, 0, 0)),
                  pl.BlockSpec((1, 1, S, 2 * LANES), lambda b, g, i: (b, g, 0, 0)),
                  pl.BlockSpec((1, 1, HD + ONES_ROWS, S), lambda b, g, i: (b, g, 0, 0))],
        out_specs=pl.BlockSpec((1, tq, 2 * LANES), lambda b, g, i: (b, i, g)),
        out_shape=jax.ShapeDtypeStruct((B, S, NSA_W), F32),
        scratch_shapes=[pltpu.VMEM((G, tq + SEL_BACK, NSA_GH * tq), F32),
                        pltpu.VMEM((NSA_GH * tq, 2 * LANES), BF16),
                        pltpu.VMEM((2, 1, NSA_GH // 2 * tq), F32),
                        pltpu.VMEM((2, HD + ONES_ROWS, NSA_GH // 2 * tq), F32)],
        compiler_params=_cparams(("arbitrary", "arbitrary", "arbitrary")),
        name="nsa_sel",
    )(rb, qn, maskq, brow, ksa, vst)


def _nsa_win_kernel(rb_ref, q_ref, kwa_ref, vwd_ref, o_ref, dw_ref, qs_ref):
    b = pl.program_id(0)
    g = pl.program_id(1)
    i = pl.program_id(2)
    tq = q_ref.shape[1]
    wk = tq + WINDOW

    @pl.when((b == 0) & (g == 0) & (i == 0))
    def _():
        dist = _iota((tq, wk), 0) + WINDOW - _iota((tq, wk), 1)
        bk = _bucket(jnp.clip(dist, 0, MAX_DISTANCE - 1))
        ok = (dist >= 0) & (dist < WINDOW)
        for h in range(H_NSA):
            val = _bias_lookup(bk, lambda bb, h=h: rb_ref[bb, h])
            dw_ref[h // NSA_GH, (h % NSA_GH) * tq:(h % NSA_GH + 1) * tq, :] = jnp.where(ok, val, NEG)

    qs = pl.multiple_of(i * tq, tq)
    _stack_heads(q_ref[0], jnp.zeros((NSA_GH, LANES), F32), None, qs_ref)
    s = _dot_nt(qs_ref[...], kwa_ref[0, 0, pl.ds(qs, wk), :]) + dw_ref[g]
    s = jnp.where(_iota(s.shape, 1) + qs >= WINDOW, s, NEG)
    m = jnp.max(s, axis=1, keepdims=True)
    p = jnp.exp(s - m)
    l = jnp.sum(p, axis=1, keepdims=True)
    a = _dot(p.astype(BF16), vwd_ref[0, 0, pl.ds(qs, wk), :]) / jnp.maximum(l, TINY)
    o_ref[0] = _unstack_heads(a, tq)


def _nsa_win(rb, qn, kwa_p, vwd_p):
    B, S, _ = qn.shape
    tq = min(256, S)
    G = NSA_KV
    sp = kwa_p.shape[2]
    return pl.pallas_call(
        _nsa_win_kernel,
        grid=(B, G, S // tq),
        in_specs=[pl.BlockSpec(memory_space=pltpu.SMEM),
                  pl.BlockSpec((1, tq, 2 * LANES), lambda b, g, i: (b, i, g)),
                  pl.BlockSpec((1, 1, sp, LANES), lambda b, g, i: (b, g, 0, 0)),
                  pl.BlockSpec((1, 1, sp, LANES), lambda b, g, i: (b, g, 0, 0))],
        out_specs=pl.BlockSpec((1, tq, 2 * LANES), lambda b, g, i: (b, i, g)),
        out_shape=jax.ShapeDtypeStruct((B, S, NSA_W), F32),
        scratch_shapes=[pltpu.VMEM((G, NSA_GH * tq, tq + WINDOW), F32),
                        pltpu.VMEM((NSA_GH * tq, LANES), BF16)],
        compiler_params=_cparams(("arbitrary", "arbitrary", "arbitrary")),
        name="nsa_win",
    )(rb, qn, kwa_p, vwd_p)


def _merge_kernel(x_ref, of_ref, oc_ref, os_ref, ow_ref, misc_ref, g_ref, w_ref, out_ref):
    tm = x_ref.shape[0]
    hi, lo = _split2(misc_ref[...])
    er = _iota((LANES, NSA_W), 0)
    ec = _iota((LANES, NSA_W), 1)
    onsa = jnp.zeros((tm, NSA_W), F32)
    for k, o_ref in enumerate((oc_ref, os_ref, ow_ref)):
        e = jnp.where(er == MISC_GATE + k * H_NSA + _div(ec, HD), 1.0, 0.0).astype(BF16)
        onsa = onsa + (_dot(hi, e) + _dot(lo, e)) * o_ref[...]
    g = g_ref[...]
    a = _rms(of_ref[...], g[:, :FOX_W]).astype(BF16)
    c = _rms(onsa, g[:, FOX_W:]).astype(BF16)
    out_ref[...] = x_ref[...] + _dot(a, w_ref[0:FOX_W, :]) + _dot(c, w_ref[FOX_W:, :])


def _merge(x2, ofox, oc, os_, ow, misc, g, w):
    T, D = x2.shape
    tm = min(512, T)
    row = lambda wd: pl.BlockSpec((tm, wd), lambda i: (i, 0))
    return pl.pallas_call(
        _merge_kernel,
        grid=(T // tm,),
        in_specs=[row(D), row(FOX_W), row(NSA_W), row(NSA_W), row(NSA_W), row(LANES),
                  pl.BlockSpec((1, D), lambda i: (0, 0)),
                  pl.BlockSpec((D, D), lambda i: (0, 0))],
        out_specs=row(D),
        out_shape=jax.ShapeDtypeStruct((T, D), F32),
        compiler_params=_cparams(("arbitrary",)),
        name="merge",
    )(x2, ofox, oc, os_, ow, misc, g, w)


FF_CHUNK = 256


def _ffn_kernel(*refs, seq_len, short_len, final):
    if seq_len is None:
        (x_ref, g2_ref, wg_ref, wu_ref, cw_ref, cb_ref, wd_ref, gf_ref, hm1_ref, hm2_ref,
         out_ref, gt_ref, h_sc, acc_sc) = refs
    else:
        (x_ref, g2_ref, wg_ref, wu_ref, cw_ref, cb_ref, wd_ref, gf_ref,
         out_ref, gt_ref, h_sc, acc_sc, carry_sc) = refs
    i = pl.program_id(0)
    j = pl.program_id(1)
    tm = x_ref.shape[0]

    @pl.when(j == 0)
    def _():
        h_sc[...] = _rms(x_ref[...], g2_ref[...]).astype(BF16)
        acc_sc[...] = jnp.zeros_like(acc_sc)

    h = h_sc[...]
    gch = _dot(h, wg_ref[...])
    u = _dot(h, wu_ref[...])
    r1 = pltpu.roll(gch, 1, 0)
    r2 = pltpu.roll(gch, 2, 0)
    row = _iota(gch.shape, 0)
    if seq_len is None:
        t = _mod(row, short_len)
        m1 = jnp.where(t == 0, hm1_ref[...], r1)
        m2 = jnp.where(t < 2, hm2_ref[...], r2)
        gt_ref[...] = gch
    else:
        first = (i % (seq_len // tm)) == 0
        c = jnp.where(first, 0.0, carry_sc[j])
        m1 = jnp.where(row == 0, c[1:2, :], r1)
        m2 = jnp.where(row == 0, c[0:1, :], jnp.where(row == 1, c[1:2, :], r2))
        carry_sc[j, 0:2, :] = gch[tm - 2:tm, :]
        gt_ref[0] = gch[tm - 8:tm, :]
    cw = cw_ref[...]
    gc = cb_ref[...] + cw[0:1, :] * m2 + cw[1:2, :] * m1 + cw[2:3, :] * gch
    act = gc * (1.0 / (1.0 + jnp.exp(-gc)))
    acc_sc[...] += _dot((act * u).astype(BF16), wd_ref[...])

    @pl.when(j == pl.num_programs(1) - 1)
    def _():
        y = x_ref[...] + acc_sc[...]
        if final:
            y = _rms(y, gf_ref[...])
        out_ref[...] = y


def _ffn(x2, g2, wgu, cw, cb, wd, gf, *, seq_len, final, hist=None, short_len=None):
    T, D = x2.shape
    dff = wd.shape[0]
    fc = FF_CHUNK
    nff = dff // fc
    tm = min(1024, T) if seq_len is not None else T
    nt = T // tm
    in_specs = [pl.BlockSpec((tm, D), lambda i, j: (i, 0)),
                pl.BlockSpec((1, D), lambda i, j: (0, 0)),
                pl.BlockSpec((D, fc), lambda i, j: (0, j)),
                pl.BlockSpec((D, fc), lambda i, j: (0, nff + j)),
                pl.BlockSpec((CONV_W, fc), lambda i, j: (0, j)),
                pl.BlockSpec((1, fc), lambda i, j: (0, j)),
                pl.BlockSpec((fc, D), lambda i, j: (j, 0)),
                pl.BlockSpec((1, D), lambda i, j: (0, 0))]
    args = [x2, g2, wgu, wgu, cw, cb, wd, gf]
    scratch = [pltpu.VMEM((tm, D), BF16), pltpu.VMEM((tm, D), F32)]
    if seq_len is None:
        in_specs += [pl.BlockSpec((tm, fc), lambda i, j: (i, j)), pl.BlockSpec((tm, fc), lambda i, j: (i, j))]
        args += list(hist)
        gt_spec = pl.BlockSpec((tm, fc), lambda i, j: (i, j))
        gt_shape = jax.ShapeDtypeStruct((T, dff), F32)
    else:
        scratch.append(pltpu.VMEM((nff, 8, fc), F32))
        gt_spec = pl.BlockSpec((1, 8, fc), lambda i, j: (i, 0, j))
        gt_shape = jax.ShapeDtypeStruct((nt, 8, dff), F32)
    return pl.pallas_call(
        functools.partial(_ffn_kernel, seq_len=seq_len, short_len=short_len, final=final),
        grid=(nt, nff),
        in_specs=in_specs,
        out_specs=[pl.BlockSpec((tm, D), lambda i, j: (i, 0)), gt_spec],
        out_shape=[jax.ShapeDtypeStruct((T, D), F32), gt_shape],
        scratch_shapes=scratch,
        compiler_params=_cparams(("arbitrary", "arbitrary")),
        name="ffn",
    )(*args)


def _rows_th(q):
    tq = q.shape[0]
    rows = jnp.concatenate([jnp.broadcast_to(q[t:t + 1, :], (8, q.shape[1])) for t in range(tq)], axis=0)
    keep = _div(_iota(rows.shape, 1), HD) == _mod(_iota(rows.shape, 0), 8)
    return jnp.where(keep, rows * SCALE, 0.0).astype(BF16)


def _diag_rows(o_ref, o32):
    keep = _div(_iota(o32.shape, 1), HD) == _mod(_iota(o32.shape, 0), 8)
    od = jnp.where(keep, o32, 0.0)
    for t in range(o32.shape[0] // 8):
        o_ref[0, t:t + 1, :] = jnp.sum(od[t * 8:(t + 1) * 8, :], axis=0, keepdims=True)


FOX_PAGES = 8
FOX_CUM_PAGES = 16


def _fox_cum_kernel(pt_tab, *refs):
    P = FOX_CUM_PAGES
    lfn_ref = refs[0]
    lf_refs = refs[1:1 + P]
    cum_ref, cumn_ref, carry_ref = refs[1 + P:]
    s = pl.program_id(1)

    @pl.when(s == 0)
    def _():
        carry_ref[...] = jnp.zeros_like(carry_ref)

    c1 = _iota((LANES, LANES), 0)
    c2 = _iota((LANES, LANES), 1)
    same_head = _mod(c1, H_FOX) == _mod(c2, H_FOX)
    upper = jnp.where(same_head & (c1 <= c2), 1.0, 0.0).astype(BF16)
    last = jnp.where(c1 == LANES - H_FOX + _mod(c2, H_FOX), 1.0, 0.0).astype(BF16)
    lf = jnp.concatenate([r[...] for r in lf_refs], axis=0)
    inner = _dot3_l(lf, upper)
    tot = _dot3_l(inner, last)
    row = _mod(_iota(tot.shape, 0), 8)
    x = tot
    for sh in (1, 2, 4):
        x = x + jnp.where(row >= sh, pltpu.roll(x, sh, 0), 0.0)
    excl = x - tot
    off = carry_ref[0:1, :]
    for k in range(P):
        cum_ref[0, k] = inner[8 * k:8 * k + 8, :] + excl[8 * k:8 * k + 8, :] + off
        off = off + x[8 * k + 7:8 * k + 8, :]
    carry_ref[...] = jnp.broadcast_to(off, carry_ref.shape)

    @pl.when(s == pl.num_programs(1) - 1)
    def _():
        cumn_ref[0] = _dot3_l(lfn_ref[0], upper) + off


def _fox_cum(lfn_flat, cache_lf_flat, page_table, layer):
    nb, npages = page_table.shape
    P = FOX_CUM_PAGES
    seq = lambda b, s, pt: (b, 0, 0)

    def lf_spec(k):
        return pl.BlockSpec((None, None, 8, LANES), lambda b, s, pt: (layer, pt[b, s * P + k], 0, 0))

    return pl.pallas_call(
        _fox_cum_kernel,
        grid_spec=pltpu.PrefetchScalarGridSpec(
            num_scalar_prefetch=1,
            grid=(nb, npages // P),
            in_specs=[pl.BlockSpec((1, 8, LANES), seq)] + [lf_spec(k) for k in range(P)],
            out_specs=[pl.BlockSpec((1, P, 8, LANES), lambda b, s, pt: (b, s, 0, 0)),
                       pl.BlockSpec((1, 8, LANES), seq)],
            scratch_shapes=[pltpu.VMEM((8, LANES), F32)]),
        out_shape=[jax.ShapeDtypeStruct((nb, npages, 8, LANES), F32),
                   jax.ShapeDtypeStruct((nb, 8, LANES), F32)],
        compiler_params=_cparams(("arbitrary", "arbitrary")),
        name="fox_cum",
    )(page_table, lfn_flat, *([cache_lf_flat] * P))


def _fox_sample_kernel(pt_tab, *refs):
    P = FOX_PAGES
    q_ref, kn_ref, vn_ref, cumn_ref, cum_ref = refs[0:5]
    k_refs = refs[5:5 + P]
    v_refs = refs[5 + P:5 + 2 * P]
    o_ref, m_ref, l_ref, acc_ref, kn_sc, vn_sc = refs[5 + 2 * P:]
    s = pl.program_id(1)
    nr = q_ref.shape[1]
    pw = PAGE * H_FOX

    @pl.when(s == 0)
    def _():
        _softmax_init(m_ref, l_ref, acc_ref)
        kn_sc[...] = jnp.zeros_like(kn_sc)
        vn_sc[...] = jnp.zeros_like(vn_sc)
        kn_sc[0:nr, :] = kn_ref[0]
        vn_sc[0:nr, :] = vn_ref[0]

    qq = (q_ref[0] * SCALE).astype(BF16)

    def attend(sc, vals):
        m_old = m_ref[...]
        m_new = jnp.maximum(m_old, jnp.max(sc, axis=1, keepdims=True))
        alpha = jnp.exp(m_old - m_new)
        p = jnp.exp(sc - m_new)
        l_ref[...] = alpha * l_ref[...] + jnp.sum(p, axis=1, keepdims=True)
        acc = alpha * acc_ref[...]
        w = sc.shape[1] // len(vals)
        for k, v in enumerate(vals):
            acc = acc + _dot(p[:, k * w:(k + 1) * w].astype(BF16), v)
        acc_ref[...] = acc
        m_ref[...] = m_new

    keep = _mod(_iota((nr, pw), 0), H_FOX) == _mod(_iota((nr, pw), 1), H_FOX)
    sc = []
    for k in range(P):
        k2 = k_refs[k][...].reshape(pw, HD).astype(BF16)
        c = cum_ref[0, k]
        d_s = jnp.concatenate([c[r:r + 1, :] for r in range(8)], axis=1)
        sc.append(jnp.where(keep, _dot_nt(qq, k2) - d_s, NEG))
    attend(jnp.concatenate(sc, axis=1), [r[...].reshape(pw, HD).astype(BF16) for r in v_refs])

    @pl.when(s == pl.num_programs(1) - 1)
    def _():
        row = _iota((nr, LANES), 0)
        col = _iota((nr, LANES), 1)
        ok = (col < nr) & (_mod(row, H_FOX) == _mod(col, H_FOX)) & (_div(col, H_FOX) <= _div(row, H_FOX))
        a = _dot_nt(qq, kn_sc[...].astype(BF16)) - cumn_ref[0][0:1, :]
        attend(jnp.where(ok, a, NEG), [vn_sc[...].astype(BF16)])
        o_ref[0] = acc_ref[...] / jnp.maximum(l_ref[...], TINY)


def _fox_sample(q_rows, k_new, v_new, cum_new, cum, cache_fox_kv, page_table, layer):
    nb, npages = page_table.shape
    nr = q_rows.shape[1]
    P = FOX_PAGES
    seq = lambda b, s, pt: (b, 0, 0)

    def kv_spec(k, c):
        return pl.BlockSpec((None, None, PAGE, None, H_FOX, HD),
                            lambda b, s, pt: (layer, pt[b, s * P + k], 0, c, 0, 0))

    return pl.pallas_call(
        _fox_sample_kernel,
        grid_spec=pltpu.PrefetchScalarGridSpec(
            num_scalar_prefetch=1,
            grid=(nb, npages // P),
            in_specs=[pl.BlockSpec((1, nr, HD), seq), pl.BlockSpec((1, nr, HD), seq), pl.BlockSpec((1, nr, HD), seq),
                      pl.BlockSpec((1, 8, LANES), seq),
                      pl.BlockSpec((1, P, 8, LANES), lambda b, s, pt: (b, s, 0, 0))]
            + [kv_spec(k, 0) for k in range(P)] + [kv_spec(k, 1) for k in range(P)],
            out_specs=pl.BlockSpec((1, nr, HD), seq),
            scratch_shapes=[pltpu.VMEM((nr, 1), F32), pltpu.VMEM((nr, 1), F32), pltpu.VMEM((nr, HD), F32),
                            pltpu.VMEM((LANES, HD), F32), pltpu.VMEM((LANES, HD), F32)]),
        out_shape=jax.ShapeDtypeStruct((nb, nr, HD), F32),
        compiler_params=_cparams(("arbitrary", "arbitrary")),
        name="fox_sample",
    )(page_table, q_rows, k_new, v_new, cum_new, cum, *([cache_fox_kv] * (2 * P)))


def _place_wide():
    r = _iota((LANES, NSA_W), 0)
    c = _iota((LANES, NSA_W), 1)
    return jnp.where(r == _div(c, NSA_GH * HD) * HD + _mod(c, HD), 1.0, 0.0).astype(BF16)


def _rb_col(rbt_ref):
    return lambda bb: rbt_ref[:, bb:bb + 1]


def _ns_attend_kernel(q_ref, kcw_ref, win_ref, wn_ref, rbt_ref, oc_ref, ow_ref, mb_ref, new_sc, *, past):
    tq = q_ref.shape[1]
    nr = tq * 8
    n = kcw_ref.shape[2] - CMP_PAD_FRONT - CMP_PAD_BACK
    n_selp = past // SEL_LEN
    wb = win_ref.shape[0]
    qq = _rows_th(q_ref[0])
    kc = kcw_ref[0, 0, CMP_PAD_FRONT:CMP_PAD_FRONT + n, :]
    vc = kcw_ref[0, 1, CMP_PAD_FRONT:CMP_PAD_FRONT + n, :]
    trow = _div(_iota((nr, n), 0), 8)
    dist = past + trow - CMP_STRIDE * _iota((nr, n), 1) - (CMP_LEN - 1)
    bias = _bias_lookup(_bucket(jnp.clip(dist, 0, MAX_DISTANCE - 1)), _rb_col(rbt_ref))
    ok = dist >= 0
    s = jnp.where(ok, _dot_nt(qq, kc) + bias, NEG)
    m = jnp.max(s, axis=1, keepdims=True)
    p = jnp.where(ok, jnp.exp(s - m), 0.0)
    p = p / jnp.maximum(jnp.sum(p, axis=1, keepdims=True), TINY)
    _diag_rows(oc_ref, _dot(p.astype(BF16), vc))
    mov = jnp.where(_overlap(_iota((n, LANES), 0), _iota((n, LANES), 1)), 1.0, 0.0).astype(BF16)
    x = _dot2_l(p, mov)
    z = x + pltpu.roll(x, nr - 1, 0) + pltpu.roll(x, nr - 2, 0) + pltpu.roll(x, nr - 3, 0)
    z0 = jnp.where(_mod(_iota(z.shape, 0), NSA_GH) == 0, z, 0.0)
    p_slc = z0 + pltpu.roll(z0, 1, 0) + pltpu.roll(z0, 2, 0) + pltpu.roll(z0, 3, 0)
    jb = _iota((nr, LANES), 1)
    forced = (jb == 0) | (jb == n_selp - 1)
    score = jnp.where(forced, FORCE_SCORE, jnp.where(jb < n_selp, p_slc, -1.0))
    k_past = min(N_SELECT, n_selp + 1) - 1
    sel = _topk_mask(score, k_past)
    mb_ref[0] = jnp.where((sel > 0.5) & (score >= 0.0), 0.0, NEG).astype(BF16)
    place = _place_wide()
    w = win_ref[...]
    kw = _dot(w[:, 0:LANES].astype(BF16), place).astype(BF16)
    vw = _dot(w[:, LANES:2 * LANES].astype(BF16), place).astype(BF16)
    new_sc[...] = jnp.zeros_like(new_sc)
    new_sc[0:tq, :] = wn_ref[0]
    wnew = new_sc[...]
    kn = _dot(wnew[:, 0:LANES].astype(BF16), place).astype(BF16)
    vn = _dot(wnew[:, LANES:2 * LANES].astype(BF16), place).astype(BF16)
    dw = _div(_iota((nr, wb), 0), 8) + wb - _iota((nr, wb), 1)
    okw = (dw >= 0) & (dw < WINDOW)
    sw = _dot_nt(qq, kw) + _bias_lookup(_bucket(jnp.clip(dw, 0, MAX_DISTANCE - 1)), _rb_col(rbt_ref))
    sw = jnp.where(okw, sw, NEG)
    dn = _div(_iota((nr, PAGE), 0), 8) - _iota((nr, PAGE), 1)
    okn = (dn >= 0) & (_iota((nr, PAGE), 1) < tq)
    sn = _dot_nt(qq, kn) + _bias_lookup(_bucket(jnp.clip(dn, 0, MAX_DISTANCE - 1)), _rb_col(rbt_ref))
    sn = jnp.where(okn, sn, NEG)
    mw = jnp.maximum(jnp.max(sw, axis=1, keepdims=True), jnp.max(sn, axis=1, keepdims=True))
    pw = jnp.where(okw, jnp.exp(sw - mw), 0.0)
    pn = jnp.where(okn, jnp.exp(sn - mw), 0.0)
    lw = jnp.maximum(jnp.sum(pw, axis=1, keepdims=True) + jnp.sum(pn, axis=1, keepdims=True), TINY)
    _diag_rows(ow_ref, (_dot(pw.astype(BF16), vw) + _dot(pn.astype(BF16), vn)) / lw)


def _ns_attend(qn, kcw, state_win, wkv_new, rbt, layer, past):
    nb, tq, _ = qn.shape
    npad = kcw.shape[2]
    wb = state_win.shape[2]
    seq = lambda b: (b, 0, 0)
    return pl.pallas_call(
        functools.partial(_ns_attend_kernel, past=past),
        grid=(nb,),
        in_specs=[pl.BlockSpec((1, tq, NSA_W), seq),
                  pl.BlockSpec((1, 2, npad, NSA_W), lambda b: (b, 0, 0, 0)),
                  pl.BlockSpec((None, None, wb, 2 * LANES), lambda b: (layer, b, 0, 0)),
                  pl.BlockSpec((1, tq, 2 * LANES), seq),
                  pl.BlockSpec((tq * 8, N_BUCKETS), lambda b: (0, 0))],
        out_specs=[pl.BlockSpec((1, tq, NSA_W), seq), pl.BlockSpec((1, tq, NSA_W), seq),
                   pl.BlockSpec((1, tq * 8, LANES), seq)],
        out_shape=[jax.ShapeDtypeStruct((nb, tq, NSA_W), F32), jax.ShapeDtypeStruct((nb, tq, NSA_W), F32),
                   jax.ShapeDtypeStruct((nb, tq * 8, LANES), BF16)],
        scratch_shapes=[pltpu.VMEM((PAGE, 2 * LANES), F32)],
        compiler_params=_cparams(("arbitrary",)),
        name="ns_attend",
    )(qn, kcw, state_win, wkv_new, rbt)


def _ns_select_kernel(pt_tab, *refs, past):
    P = PAGES_PER_STEP
    q_ref, mb_ref, nn_ref, rbt_ref = refs[0:4]
    pg_refs = refs[4:4 + P]
    o_ref, q_sc, m_ref, l_ref, acc_ref, new_sc = refs[4 + P:]
    s = pl.program_id(1)
    tq = q_ref.shape[1]
    nr = tq * 8

    @pl.when(s == 0)
    def _():
        r = _iota((NSA_W, LANES), 0)
        c = _iota((NSA_W, LANES), 1)
        fold = jnp.where(c == _div(r, NSA_GH * HD) * HD + _mod(r, HD), 1.0, 0.0).astype(BF16)
        q_sc[...] = _dot(_rows_th(q_ref[0]), fold).astype(BF16)
        _softmax_init(m_ref, l_ref, acc_ref)
        new_sc[...] = jnp.zeros_like(new_sc)
        new_sc[0:tq, :] = nn_ref[0][:, 2 * LANES:]

    qq = q_sc[...]
    trow = _div(_iota((nr, PAGE), 0), 8)
    key = _iota((nr, PAGE), 1)

    def attend(tiles, biases):
        sc = jnp.concatenate([_dot_nt(qq, x[:, 0:LANES].astype(BF16)) for x in tiles], axis=1) + biases
        m_old = m_ref[...]
        m_new = jnp.maximum(m_old, jnp.max(sc, axis=1, keepdims=True))
        alpha = jnp.exp(m_old - m_new)
        p = jnp.exp(sc - m_new)
        l_ref[...] = alpha * l_ref[...] + jnp.sum(p, axis=1, keepdims=True)
        acc = alpha * acc_ref[...]
        for k, x in enumerate(tiles):
            acc = acc + _dot(p[:, k * PAGE:(k + 1) * PAGE].astype(BF16), x[:, LANES:2 * LANES].astype(BF16))
        acc_ref[...] = acc
        m_ref[...] = m_new

    blk = _iota((LANES, P * PAGE), 0)
    kcol = _iota((LANES, P * PAGE), 1)
    expand = jnp.where(blk == _div(s * (P * PAGE) + kcol, SEL_LEN), 1.0, 0.0).astype(BF16)
    bias = _dot(mb_ref[0], expand) + rbt_ref[:, N_BUCKETS - 1:N_BUCKETS]
    tiles = [r[...] for r in pg_refs]

    @pl.when(s < pl.num_programs(1) - 1)
    def _():
        attend(tiles, bias)

    @pl.when(s == pl.num_programs(1) - 1)
    def _():
        d_last = past + trow - (past - PAGE + key)
        b_last = _bias_lookup(_bucket(jnp.clip(d_last, 0, MAX_DISTANCE - 1)), _rb_col(rbt_ref))
        fix = jnp.concatenate([jnp.zeros((nr, (P - 1) * PAGE), F32),
                               b_last - rbt_ref[:, N_BUCKETS - 1:N_BUCKETS]], axis=1)
        attend(tiles, bias + fix)
        d_new = trow - key
        b_new = _bias_lookup(_bucket(jnp.clip(d_new, 0, MAX_DISTANCE - 1)), _rb_col(rbt_ref))
        attend([new_sc[...]], jnp.where((d_new >= 0) & (key < tq), b_new, NEG))
        a = acc_ref[...] / jnp.maximum(l_ref[...], TINY)
        place = _place_wide()
        _diag_rows(o_ref, _dot2_l(a, place))


def _ns_select(qn, maskb, nkv_new, rbt, cache_nsa, page_table, layer, past):
    nb, npages = page_table.shape
    tq = qn.shape[1]
    P = PAGES_PER_STEP
    seq = lambda b, s, pt: (b, 0, 0)

    def page_spec(k):
        return pl.BlockSpec((None, None, PAGE, 2 * LANES), lambda b, s, pt: (layer, pt[b, s * P + k], 0, 1))

    return pl.pallas_call(
        functools.partial(_ns_select_kernel, past=past),
        grid_spec=pltpu.PrefetchScalarGridSpec(
            num_scalar_prefetch=1,
            grid=(nb, npages // P),
            in_specs=[pl.BlockSpec((1, tq, NSA_W), seq), pl.BlockSpec((1, tq * 8, LANES), seq),
                      pl.BlockSpec((1, tq, 4 * LANES), seq),
                      pl.BlockSpec((tq * 8, N_BUCKETS), lambda b, s, pt: (0, 0))]
            + [page_spec(k) for k in range(P)],
            out_specs=pl.BlockSpec((1, tq, NSA_W), seq),
            scratch_shapes=[pltpu.VMEM((tq * 8, LANES), BF16),
                            pltpu.VMEM((tq * 8, 1), F32), pltpu.VMEM((tq * 8, 1), F32),
                            pltpu.VMEM((tq * 8, LANES), F32),
                            pltpu.VMEM((PAGE, 2 * LANES), F32)]),
        out_shape=jax.ShapeDtypeStruct((nb, tq, NSA_W), F32),
        compiler_params=_cparams(("arbitrary", "arbitrary")),
        name="ns_select",
    )(page_table, qn, maskb, nkv_new, rbt, *([cache_nsa] * P))


def _prep_w_in(w):
    d = w.shape[0]
    o_logf = 3 * FOX_W
    o_qn = o_logf + H_FOX
    o_kv = o_qn + NSA_W
    o_gate = o_kv + 6 * NSA_KV * HD
    misc = jnp.concatenate([w[:, o_logf:o_qn], w[:, o_gate:o_gate + 3 * H_NSA],
                            jnp.zeros((d, LANES - H_FOX - 3 * H_NSA), w.dtype)], axis=1)
    return jnp.concatenate([w[:, :o_logf], w[:, o_qn:o_kv], w[:, o_kv:o_gate], misc], axis=1).astype(BF16)


def _prep_cmp(pos, w1, b1, w2):
    eye = jnp.eye(NSA_KV, dtype=w1.dtype)
    w1r = w1.reshape(2, 2, CMP_STRIDE, HD, CMP_HID)
    wide = jnp.einsum("whjdc,ab->whjadbc", w1r, eye).reshape(2, 2, CMP_STRIDE * LANES, 2 * CMP_HID)
    posr = jnp.broadcast_to(pos.reshape(2, 2, CMP_STRIDE, 1, HD), (2, 2, CMP_STRIDE, NSA_KV, HD))
    posr = posr.reshape(2, 2, 1, CMP_STRIDE * LANES)
    group_of_head = (jnp.arange(H_NSA) // NSA_GH)[None, :] == jnp.arange(NSA_KV)[:, None]
    w2w = jnp.einsum("wcd,ah->wachd", w2, group_of_head.astype(w2.dtype)).reshape(2, 2 * CMP_HID, NSA_W)
    return {"pt": posr[:, 0], "pb": posr[:, 1],
            "w1t": wide[:, 0].astype(BF16), "w1b": wide[:, 1].astype(BF16),
            "b1": jnp.concatenate([b1, b1], axis=-1)[:, None, :],
            "w2w": w2w.astype(BF16)}


def _far_bias_rows(rel_bias):
    far = rel_bias[N_BUCKETS - 1]
    hi = far.astype(BF16).astype(F32)
    r = far - hi
    mid = r.astype(BF16).astype(F32)
    lo = (r - mid).astype(BF16).astype(F32)
    rows = jnp.zeros((H_NSA, LANES), F32)
    rows = rows.at[:, HD].set(hi).at[:, HD + 1].set(mid).at[:, HD + 2].set(lo)
    return rows.reshape(NSA_KV, NSA_GH, LANES)


def kernel(x_prompt, x_sample, cache_fox_kv, cache_fox_logf, cache_nsa_kv, state_win_kv, state_conv,
           page_table, norm1_g, w_in, b_forget, cmp_pos, cmp_w1, cmp_b1, cmp_w2, out_norm_g, w_out,
           norm2_g, w_gu, conv_w, conv_b, w_down, rel_bias, final_norm_g):
    B, S, D = x_prompt.shape
    nb, tq, _ = x_sample.shape
    depth = w_in.shape[0]
    n_pool = cache_fox_kv.shape[1]
    npages = page_table.shape[1]
    past = npages * PAGE
    dff = w_down.shape[1]
    wb = state_win_kv.shape[2]
    assert tq & (tq - 1) == 0 and tq >= CONV_W - 1
    assert S % 256 == 0 and S // SEL_LEN <= LANES and past // SEL_LEN <= LANES
    assert npages % PAGES_PER_STEP == 0 and wb == WINDOW and past >= WINDOW and tq * 8 <= LANES

    assert npages % FOX_PAGES == 0 and npages % FOX_CUM_PAGES == 0 and PAGE * H_FOX == 8 * LANES
    cache_lf_flat = cache_fox_logf.reshape(depth, n_pool, 8, LANES)
    cache_nsa = cache_nsa_kv.reshape(depth, n_pool, PAGE, 4 * NSA_KV * HD)
    state_win = state_win_kv.reshape(depth, nb, wb, 2 * NSA_KV * HD)
    rbt = jnp.tile(rel_bias.T, (tq, 1))
    brow = _far_bias_rows(rel_bias * LOG2E)
    gf = final_norm_g.reshape(1, D)

    xp = x_prompt.reshape(B * S, D)
    xs = x_sample.reshape(nb * tq, D)
    outs = [[] for _ in range(10)]
    for l in range(depth):
        w_l = _prep_w_in(w_in[l])
        bf = jnp.zeros((1, LANES), F32).at[0, :H_FOX].set(b_forget[l])
        cw = _prep_cmp(cmp_pos[l], cmp_w1[l], cmp_b1[l], cmp_w2[l])
        g1 = norm1_g[l].reshape(1, D)
        g2 = norm2_g[l].reshape(1, D)
        go = out_norm_g[l].reshape(1, D)
        wo = w_out[l].astype(BF16)
        wgu = w_gu[l].astype(BF16)
        wd = w_down[l].astype(BF16)
        cb = conv_b[l].reshape(1, dff)
        final = l == depth - 1

        qf, fkv, qn, nkv, wkv, misc = _proj(xp, g1, w_l, bf)
        r3 = lambda a: a.reshape(B, S, a.shape[-1])
        qa, ka, vb = _fox_prep(r3(qf), r3(fkv), r3(misc))
        vt = jnp.swapaxes(vb, 1, 2).reshape(B, H_FOX // 2, 2 * HD, S)
        vt = jnp.concatenate([vt, jnp.ones((B, H_FOX // 2, ONES_ROWS, S), BF16)], axis=2)
        o_fox = _fox_flash(qa, ka, vt)
        kcw = _compress_prompt(r3(nkv), cw)
        ksa, kwa, vwd = _nsa_prep(r3(nkv), r3(wkv))
        vst = r3(nkv)[:, :, 3 * LANES:].astype(BF16).reshape(B, S, NSA_KV, HD).transpose(0, 2, 3, 1)
        vst = jnp.concatenate([vst, jnp.ones((B, NSA_KV, ONES_ROWS, S), BF16)], axis=2)
        o_c, maskq = _nsa_cmp(rel_bias, r3(qn), kcw)
        o_s = _nsa_sel(rel_bias, r3(qn), maskq, brow, ksa, vst)
        padw = ((0, 0), (0, 0), (WINDOW, 0), (0, 0))
        o_w = _nsa_win(rel_bias, r3(qn), jnp.pad(kwa, padw), jnp.pad(vwd, padw))
        f2 = lambda a: a.reshape(B * S, a.shape[-1])
        xp = _merge(xp, f2(o_fox), f2(o_c), f2(o_s), f2(o_w), misc, go, wo)
        xp, gtail = _ffn(xp, g2, wgu, conv_w[l], cb, wd, gf, seq_len=S, final=final)
        tiles_per_seq = gtail.shape[0] // B
        conv_p = gtail.reshape(B, tiles_per_seq, 8, dff)[:, -1, 8 - (CONV_W - 1):, :]
        outs[0].append(fkv.reshape(B, S, 2, H_FOX, HD))
        outs[2].append(misc[:, :H_FOX].reshape(B, S, H_FOX))
        outs[4].append(nkv.reshape(B, S, 4, NSA_KV, HD))
        outs[6].append(wkv.reshape(B, S, 2, NSA_KV, HD)[:, -min(WINDOW, S):])
        outs[8].append(conv_p)

        qf, fkv, qn, nkv, wkv, misc = _proj(xs, g1, w_l, bf)
        s3 = lambda a: a.reshape(nb, tq, a.shape[-1])
        lfn = jnp.zeros((nb, 8, LANES), F32).at[:, 0, :tq * H_FOX].set(misc[:, :H_FOX].reshape(nb, tq * H_FOX))
        cum, cum_new = _fox_cum(lfn, cache_lf_flat, page_table, l)
        fkv5 = fkv.reshape(nb, tq, 2, H_FOX, HD)
        o_fox = _fox_sample(qf.reshape(nb, tq * H_FOX, HD), fkv5[:, :, 0].reshape(nb, tq * H_FOX, HD),
                            fkv5[:, :, 1].reshape(nb, tq * H_FOX, HD), cum_new, cum, cache_fox_kv, page_table, l)
        kcw = _compress_sample(cache_nsa, page_table, l, cw)
        o_c, o_w, maskb = _ns_attend(s3(qn), kcw, state_win, s3(wkv), rbt, l, past)
        o_s = _ns_select(s3(qn), maskb, s3(nkv), rbt, cache_nsa, page_table, l, past)
        s2 = lambda a: a.reshape(nb * tq, a.shape[-1])
        xs = _merge(xs, o_fox.reshape(nb * tq, FOX_W), s2(o_c), s2(o_s), s2(o_w), misc, go, wo)
        hist = state_conv[l]
        zero = jnp.zeros((nb, 1, dff), F32)
        hm1 = jnp.concatenate([hist[:, 1:2], zero, zero, zero][:tq], axis=1).reshape(nb * tq, dff)
        hm2 = jnp.concatenate([hist[:, 0:1], hist[:, 1:2], zero, zero][:tq], axis=1).reshape(nb * tq, dff)
        xs, gfull = _ffn(xs, g2, wgu, conv_w[l], cb, wd, gf, seq_len=None, final=final, hist=(hm1, hm2),
                         short_len=tq)
        win_new = jnp.concatenate([state_win_kv[l], wkv.reshape(nb, tq, 2, NSA_KV, HD)], axis=1)[:, -wb:]
        outs[1].append(fkv.reshape(nb, tq, 2, H_FOX, HD))
        outs[3].append(misc[:, :H_FOX].reshape(nb, tq, H_FOX))
        outs[5].append(nkv.reshape(nb, tq, 4, NSA_KV, HD))
        outs[7].append(win_new)
        outs[9].append(gfull.reshape(nb, tq, dff)[:, -(CONV_W - 1):])

    st = [jnp.stack(o) for o in outs]
    return (xp.reshape(B, S, D), xs.reshape(nb, tq, D),
            st[0], st[1], st[2], st[3], st[4], st[5], st[6], st[7], st[8], st[9])
```

```python
import functools
import math

import numpy as np
import jax
import jax.numpy as jnp
from jax import lax
from jax.experimental import pallas as pl
from jax.experimental.pallas import tpu as pltpu

F32 = jnp.float32
BF16 = jnp.bfloat16
I32 = jnp.int32

HD = 64
H_FOX = 8
H_NSA = 8
NSA_KV = 2
NSA_GH = 4
FOX_W = H_FOX * HD
NSA_W = H_NSA * HD
CMP_LEN = 32
CMP_STRIDE = 16
CMP_HID = 256
SEL_LEN = 64
N_SELECT = 16
WINDOW = 512
N_BUCKETS = 32
MAX_DISTANCE = 128
CONV_W = 3
PAGE = 128
EPS = 1e-6
NEG = -1e30
TINY = 1e-30
FORCE_SCORE = 1e4
SCALE = HD ** -0.5
LOG2E = math.log2(math.e)
ONES_ROWS = 8

LANES = 128
VMEM_LIMIT = 56 * 1024 * 1024

C_QF, C_FKV, C_QN, C_NKV, C_WKV, C_MISC, C_END = 0, 512, 1536, 2048, 2560, 2816, 2944
MISC_LOGF = 0
MISC_GATE = 8

AUG0 = HD
AUG1 = HD + 3

PAGES_PER_STEP = 16


def _bucket_thresholds():
    exact = N_BUCKETS // 2
    n = np.arange(1, 4 * MAX_DISTANCE, dtype=np.float64)
    far = exact + (np.log(n / exact) / math.log(MAX_DISTANCE / exact) * (N_BUCKETS - exact)).astype(np.int64)
    b = np.where(n < exact, n, np.minimum(far, N_BUCKETS - 1)).astype(np.int64)
    return [int(n[b >= k].min()) for k in range(exact + 1, N_BUCKETS)]


_THR = _bucket_thresholds()


def _cparams(sem):
    return pltpu.CompilerParams(dimension_semantics=sem, vmem_limit_bytes=VMEM_LIMIT)


def _dot(a, b):
    return jnp.dot(a, b, preferred_element_type=F32)


def _dot_nt(a, b):
    return lax.dot_general(a, b, (((1,), (1,)), ((), ())), preferred_element_type=F32)


def _split2(x):
    hi = x.astype(BF16)
    lo = (x - hi.astype(F32)).astype(BF16)
    return hi, lo


def _split3(x):
    hi = x.astype(BF16)
    r = x - hi.astype(F32)
    mid = r.astype(BF16)
    lo = (r - mid.astype(F32)).astype(BF16)
    return hi, mid, lo


def _dot3_l(x, m):
    hi, mid, lo = _split3(x)
    return _dot(hi, m) + _dot(mid, m) + _dot(lo, m)


def _dot3_r(m, x):
    hi, mid, lo = _split3(x)
    return _dot(m, hi) + _dot(m, mid) + _dot(m, lo)


def _dot2_l(x, m):
    hi, lo = _split2(x)
    return _dot(hi, m) + _dot(lo, m)


def _iota(shape, dim):
    return lax.broadcasted_iota(I32, shape, dim)


def _div(x, k):
    return lax.shift_right_arithmetic(x, jnp.int32(k.bit_length() - 1))


def _mod(x, k):
    return x & (k - 1)


def _rms(x, g):
    r = lax.rsqrt(jnp.mean(x * x, axis=-1, keepdims=True) + EPS)
    return (x * r) * g


def _bucket(d):
    far = jnp.full(d.shape, N_BUCKETS // 2, I32)
    for thr in _THR:
        far = far + (d >= thr).astype(I32)
    return jnp.where(d < N_BUCKETS // 2, d, far)


def _bias_lookup(bucket, rb_get):
    acc = jnp.zeros(bucket.shape, F32)
    for b in range(N_BUCKETS):
        acc = jnp.where(bucket == b, rb_get(b), acc)
    return acc


def _topk_mask(score, k, axis=1):
    idx = _iota(score.shape, axis).astype(F32)

    def body(_, c):
        work, sel = c
        mx = jnp.max(work, axis=axis, keepdims=True)
        first = jnp.min(jnp.where(work == mx, idx, 1e9), axis=axis, keepdims=True)
        hit = idx == first
        return jnp.where(hit, -3.0, work), jnp.where(hit, 1.0, sel)

    _, sel = lax.fori_loop(0, k, body, (score, jnp.zeros(score.shape, F32)))
    return sel


def _proj_kernel(x_ref, g_ref, w_ref, bf_ref, qf_ref, fkv_ref, qn_ref, nkv_ref, wkv_ref, misc_ref):
    h = _rms(x_ref[...], g_ref[...]).astype(BF16)
    qf_ref[...] = _dot(h, w_ref[:, C_QF:C_FKV])
    fkv_ref[...] = _dot(h, w_ref[:, C_FKV:C_QN])
    qn_ref[...] = _dot(h, w_ref[:, C_QN:C_NKV])
    nkv_ref[...] = _dot(h, w_ref[:, C_NKV:C_WKV])
    wkv_ref[...] = _dot(h, w_ref[:, C_WKV:C_MISC])
    z = _dot(h, w_ref[:, C_MISC:C_END]) + bf_ref[...]
    lane = _iota(z.shape, 1)
    logsig = jnp.minimum(z, 0.0) - jnp.log(1.0 + jnp.exp(-jnp.abs(z)))
    sig = 1.0 / (1.0 + jnp.exp(-z))
    misc_ref[...] = jnp.where(lane < MISC_GATE, logsig, jnp.where(lane < MISC_GATE + 3 * H_NSA, sig, 0.0))


def _proj(x2, g, w, bf):
    T, D = x2.shape
    tm = min(512, T)
    widths = (C_FKV - C_QF, C_QN - C_FKV, C_NKV - C_QN, C_WKV - C_NKV, C_MISC - C_WKV, C_END - C_MISC)
    return pl.pallas_call(
        _proj_kernel,
        grid=(T // tm,),
        in_specs=[pl.BlockSpec((tm, D), lambda i: (i, 0)),
                  pl.BlockSpec((1, D), lambda i: (0, 0)),
                  pl.BlockSpec((D, C_END), lambda i: (0, 0)),
                  pl.BlockSpec((1, LANES), lambda i: (0, 0))],
        out_specs=[pl.BlockSpec((tm, wd), lambda i: (i, 0)) for wd in widths],
        out_shape=[jax.ShapeDtypeStruct((T, wd), F32) for wd in widths],
        compiler_params=_cparams(("arbitrary",)),
        name="proj",
    )(x2, g, w, bf)


def _fox_prep_kernel(qf_ref, k_ref, v_ref, misc_ref, qa_ref, ka_ref, vb_ref, carry_ref):
    i = pl.program_id(1)
    tm = qf_ref.shape[1]

    @pl.when(i == 0)
    def _():
        carry_ref[...] = jnp.zeros_like(carry_ref)

    lane = _iota((tm, LANES), 1)
    lf = jnp.where(lane < H_FOX, misc_ref[0], 0.0)
    tril = (_iota((tm, tm), 0) >= _iota((tm, tm), 1)).astype(BF16)
    cum = _dot3_r(tril, lf) + carry_ref[0:1, :]
    carry_ref[...] = jnp.broadcast_to(cum[tm - 1:tm, :], carry_ref.shape)
    cl2 = cum * LOG2E
    hi = cl2.astype(BF16).astype(F32)
    r = cl2 - hi
    mid = r.astype(BF16).astype(F32)
    lo = (r - mid).astype(BF16).astype(F32)
    cc = (hi + pltpu.roll(mid, H_FOX, 1) + pltpu.roll(lo, 2 * H_FOX, 1)).astype(BF16)
    er = _iota((LANES, H_FOX * LANES), 0)
    ec = _iota((LANES, H_FOX * LANES), 1)
    part = _div(er, H_FOX)
    head = _mod(er, H_FOX)
    inb = er < 3 * H_FOX
    eq = jnp.where(inb & (ec == head * LANES + AUG0 + part), 1.0, 0.0).astype(BF16)
    ek = jnp.where(inb & (ec == head * LANES + AUG1 + part), -1.0, 0.0).astype(BF16)
    cl = _mod(_iota((1, H_FOX * LANES), 1), LANES)
    ones_q = jnp.where((cl >= AUG1) & (cl < AUG1 + 3), 1.0, 0.0)
    ones_k = jnp.where((cl >= AUG0) & (cl < AUG0 + 3), 1.0, 0.0)
    cols_q = _dot(cc, eq) + ones_q
    cols_k = _dot(cc, ek) + ones_k
    q = qf_ref[0]
    k = k_ref[0]
    for h in range(H_FOX):
        a = (h // 2) * LANES
        qt = q[:, a:a + LANES]
        kt = k[:, a:a + LANES]
        if h % 2:
            qt = pltpu.roll(qt, HD, 1)
            kt = pltpu.roll(kt, HD, 1)
        qa_ref[0, h] = jnp.where(lane < HD, qt * (SCALE * LOG2E), cols_q[:, h * LANES:(h + 1) * LANES]).astype(BF16)
        ka_ref[0, h] = jnp.where(lane < HD, kt, cols_k[:, h * LANES:(h + 1) * LANES]).astype(BF16)
    vb_ref[0] = v_ref[0].astype(BF16)


def _fox_prep(qf, fkv, misc):
    B, S, _ = qf.shape
    tm = min(512, S)
    return pl.pallas_call(
        _fox_prep_kernel,
        grid=(B, S // tm),
        in_specs=[pl.BlockSpec((1, tm, FOX_W), lambda b, i: (b, i, 0)),
                  pl.BlockSpec((1, tm, FOX_W), lambda b, i: (b, i, 0)),
                  pl.BlockSpec((1, tm, FOX_W), lambda b, i: (b, i, 1)),
                  pl.BlockSpec((1, tm, LANES), lambda b, i: (b, i, 0))],
        out_specs=[pl.BlockSpec((1, H_FOX, tm, LANES), lambda b, i: (b, 0, i, 0)),
                   pl.BlockSpec((1, H_FOX, tm, LANES), lambda b, i: (b, 0, i, 0)),
                   pl.BlockSpec((1, tm, FOX_W), lambda b, i: (b, i, 0))],
        out_shape=[jax.ShapeDtypeStruct((B, H_FOX, S, LANES), BF16),
                   jax.ShapeDtypeStruct((B, H_FOX, S, LANES), BF16),
                   jax.ShapeDtypeStruct((B, S, FOX_W), BF16)],
        scratch_shapes=[pltpu.VMEM((8, LANES), F32)],
        compiler_params=_cparams(("arbitrary", "arbitrary")),
        name="fox_prep",
    )(qf, fkv, fkv, misc)


def _softmax_update(s, v, m_ref, l_ref, acc_ref):
    m_old = m_ref[...]
    m_new = jnp.maximum(m_old, jnp.max(s, axis=1, keepdims=True))
    alpha = jnp.exp(m_old - m_new)
    p = jnp.exp(s - m_new)
    l_ref[...] = alpha * l_ref[...] + jnp.sum(p, axis=1, keepdims=True)
    acc_ref[...] = alpha * acc_ref[...] + _dot(p.astype(BF16), v)
    m_ref[...] = m_new


def _softmax_init(m_ref, l_ref, acc_ref):
    m_ref[...] = jnp.full(m_ref.shape, NEG, F32)
    l_ref[...] = jnp.zeros(l_ref.shape, F32)
    acc_ref[...] = jnp.zeros(acc_ref.shape, F32)


def _softmax_update_t(s, vt, m_ref, acc_ref):
    m_old = m_ref[...]
    m_new = jnp.maximum(m_old, jnp.max(s, axis=0, keepdims=True))
    alpha = jnp.exp2(m_old - m_new)
    p = jnp.exp2(s - m_new).astype(BF16)
    acc_ref[...] = alpha * acc_ref[...] + _dot(vt, p)
    m_ref[...] = m_new


def _softmax_init_t(m_ref, acc_ref):
    m_ref[...] = jnp.full(m_ref.shape, NEG, F32)
    acc_ref[...] = jnp.zeros(acc_ref.shape, F32)


def _fox_flash_kernel(qa_ref, ka_ref, vt_ref, o_ref, m_ref, acc_ref):
    qi = pl.program_id(2)
    tq = qa_ref.shape[2]
    _softmax_init_t(m_ref, acc_ref)

    def tile(kt, diag):
        ks = pl.multiple_of(kt * tq, tq)
        vt = vt_ref[0, 0, :, pl.ds(ks, tq)]
        for hh in range(2):
            s = _dot_nt(ka_ref[0, hh, pl.ds(ks, tq), :], qa_ref[0, hh])
            if diag:
                s = jnp.where(_iota(s.shape, 0) <= _iota(s.shape, 1), s, NEG)
            _softmax_update_t(s, vt, m_ref.at[hh], acc_ref.at[hh])

    def full_tile(kt, c):
        tile(kt, False)
        return c

    lax.fori_loop(0, qi, full_tile, 0)
    tile(qi, True)
    halves = []
    for hh in range(2):
        a = acc_ref[hh]
        halves.append(a[hh * HD:(hh + 1) * HD, :] / jnp.maximum(a[2 * HD:2 * HD + 1, :], TINY))
    o_ref[0] = jnp.concatenate(halves, axis=0).T


def _fox_flash(qa, ka, vt):
    B, H, S, _ = qa.shape
    tq = min(512, S)
    rows = vt.shape[2]
    return pl.pallas_call(
        _fox_flash_kernel,
        grid=(B, H // 2, S // tq),
        in_specs=[pl.BlockSpec((1, 2, tq, LANES), lambda b, p, i: (b, p, i, 0)),
                  pl.BlockSpec((1, 2, S, LANES), lambda b, p, i: (b, p, 0, 0)),
                  pl.BlockSpec((1, 1, rows, S), lambda b, p, i: (b, p, 0, 0))],
        out_specs=pl.BlockSpec((1, tq, LANES), lambda b, p, i: (b, i, p)),
        out_shape=jax.ShapeDtypeStruct((B, S, FOX_W), F32),
        scratch_shapes=[pltpu.VMEM((2, 1, tq), F32), pltpu.VMEM((2, rows, tq), F32)],
        compiler_params=_cparams(("arbitrary", "arbitrary", "arbitrary")),
        name="fox_flash",
    )(qa, ka, vt)


def _gelu_tanh(x):
    return 0.5 * x * (1.0 + jnp.tanh(math.sqrt(2.0 / math.pi) * (x + 0.044715 * (x * x * x))))


def _compress_core(x, pt, pb, w1t, w1b, b1, w2w):
    n = x.shape[0]
    a = _dot((x + pt).astype(BF16), w1t)
    b = _dot((x + pb).astype(BF16), w1b)
    h = a + pltpu.roll(b, n - 1, 0) + b1
    return _dot(_gelu_tanh(h).astype(BF16), w2w)


CMP_PAD_FRONT = 16
CMP_PAD_BACK = 112


def _store_cmp(out_ref, idx, res, n):
    out_ref[idx] = jnp.zeros(out_ref.shape[len(idx):], BF16)
    out_ref[idx + (slice(CMP_PAD_FRONT, CMP_PAD_FRONT + n), slice(None))] = res.astype(BF16)


def _compress_prompt_kernel(x_ref, pt_ref, pb_ref, w1t_ref, w1b_ref, b1_ref, w2w_ref, out_ref):
    n = x_ref.shape[1] // CMP_STRIDE
    x = jnp.concatenate([x_ref[0, pl.ds(j, n, stride=CMP_STRIDE), :] for j in range(CMP_STRIDE)], axis=1)
    res = _compress_core(x, pt_ref[0], pb_ref[0], w1t_ref[0], w1b_ref[0], b1_ref[0], w2w_ref[0])
    _store_cmp(out_ref, (0, 0), res, n)


def _compress_prompt(nkv, cw):
    B, S, _ = nkv.shape
    n = S // CMP_STRIDE
    npad = n + CMP_PAD_FRONT + CMP_PAD_BACK
    kx = CMP_STRIDE * LANES
    return pl.pallas_call(
        _compress_prompt_kernel,
        grid=(B, 2),
        in_specs=[pl.BlockSpec((1, S, LANES), lambda b, w: (b, 0, w)),
                  pl.BlockSpec((1, 1, kx), lambda b, w: (w, 0, 0)),
                  pl.BlockSpec((1, 1, kx), lambda b, w: (w, 0, 0)),
                  pl.BlockSpec((1, kx, 2 * CMP_HID), lambda b, w: (w, 0, 0)),
                  pl.BlockSpec((1, kx, 2 * CMP_HID), lambda b, w: (w, 0, 0)),
                  pl.BlockSpec((1, 1, 2 * CMP_HID), lambda b, w: (w, 0, 0)),
                  pl.BlockSpec((1, 2 * CMP_HID, NSA_W), lambda b, w: (w, 0, 0))],
        out_specs=pl.BlockSpec((1, 1, npad, NSA_W), lambda b, w: (b, w, 0, 0)),
        out_shape=jax.ShapeDtypeStruct((B, 2, npad, NSA_W), BF16),
        compiler_params=_cparams(("arbitrary", "arbitrary")),
        name="compress_prompt",
    )(nkv, cw["pt"], cw["pb"], cw["w1t"], cw["w1b"], cw["b1"], cw["w2w"])


def _compress_sample_kernel(pt_tab, *refs):
    P = PAGES_PER_STEP
    pages = (refs[:P], refs[P:2 * P])
    pt_ref, pb_ref, w1t_ref, w1b_ref, b1_ref, w2w_ref, out_ref, x_sc, rows_sc = refs[2 * P:]
    s = pl.program_id(1)
    n = x_sc.shape[1]
    rows = P * (PAGE // CMP_STRIDE)
    r0 = pl.multiple_of(s * rows, rows)
    for w in range(2):
        for k, pg in enumerate(pages[w]):
            rows_sc[k * PAGE:(k + 1) * PAGE, :] = pg[...].reshape(LANES, PAGE).T
        for j in range(CMP_STRIDE):
            x_sc[w, pl.ds(r0, rows), j * LANES:(j + 1) * LANES] = rows_sc[pl.ds(j, rows, stride=CMP_STRIDE), :]

    @pl.when(s == pl.num_programs(1) - 1)
    def _():
        for w in range(2):
            res = _compress_core(x_sc[w], pt_ref[w], pb_ref[w], w1t_ref[w], w1b_ref[w], b1_ref[w], w2w_ref[w])
            _store_cmp(out_ref, (0, w), res, n)


def _compress_sample(cache_nsa, page_table, layer, cw):
    nb, npages = page_table.shape
    P = PAGES_PER_STEP
    n = npages * (PAGE // CMP_STRIDE)
    npad = n + CMP_PAD_FRONT + CMP_PAD_BACK
    kx = CMP_STRIDE * LANES

    def page_spec(k, w):
        return pl.BlockSpec((None, None, None, NSA_KV, HD, PAGE),
                            lambda b, s, pt: (layer, pt[b, s * P + k], w, 0, 0, 0))

    const3 = lambda b, s, pt: (0, 0, 0)
    return pl.pallas_call(
        _compress_sample_kernel,
        grid_spec=pltpu.PrefetchScalarGridSpec(
            num_scalar_prefetch=1,
            grid=(nb, npages // P),
            in_specs=[page_spec(k, 0) for k in range(P)] + [page_spec(k, 1) for k in range(P)] + [
                pl.BlockSpec((2, 1, kx), const3), pl.BlockSpec((2, 1, kx), const3),
                pl.BlockSpec((2, kx, 2 * CMP_HID), const3), pl.BlockSpec((2, kx, 2 * CMP_HID), const3),
                pl.BlockSpec((2, 1, 2 * CMP_HID), const3), pl.BlockSpec((2, 2 * CMP_HID, NSA_W), const3)],
            out_specs=pl.BlockSpec((1, 2, npad, NSA_W), lambda b, s, pt: (b, 0, 0, 0)),
            scratch_shapes=[pltpu.VMEM((2, n, kx), F32), pltpu.VMEM((P * PAGE, LANES), F32)]),
        out_shape=jax.ShapeDtypeStruct((nb, 2, npad, NSA_W), BF16),
        compiler_params=_cparams(("arbitrary", "arbitrary")),
        name="compress_sample",
    )(page_table, *([cache_nsa] * (2 * P)), cw["pt"], cw["pb"], cw["w1t"], cw["w1b"], cw["b1"], cw["w2w"])


def _nsa_prep_kernel(sel_ref, win_ref, ksa_ref, kwa_ref, vwd_ref):
    i = pl.program_id(1)
    tm = sel_ref.shape[1]
    lane = _iota((tm, LANES), 1)
    pos = i * tm + _iota((tm, LANES), 0)
    onehot = jnp.where(_div(pos, SEL_LEN) == lane, 1.0, 0.0)
    ks = sel_ref[0][:, 0:LANES]
    w = win_ref[0]
    kw = w[:, 0:LANES]
    vw = w[:, LANES:2 * LANES]
    rks = pltpu.roll(ks, HD, 1)
    rkw = pltpu.roll(kw, HD, 1)
    rvw = pltpu.roll(vw, HD, 1)
    for g in range(NSA_KV):
        left = jnp.where(lane < HD, ks if g == 0 else rks, jnp.where(lane < HD + 3, 1.0, 0.0))
        ksa_ref[0, g] = jnp.concatenate([left, onehot], axis=1).astype(BF16)
        kwa_ref[0, g] = jnp.where(lane < HD, kw if g == 0 else rkw, 0.0).astype(BF16)
        vd = jnp.where(lane < HD, vw, rvw) if g == 0 else jnp.where(lane < HD, rvw, vw)
        vwd_ref[0, g] = vd.astype(BF16)


def _nsa_prep(nkv, wkv):
    B, S, _ = nkv.shape
    tm = min(512, S)
    spec128 = pl.BlockSpec((1, NSA_KV, tm, LANES), lambda b, i: (b, 0, i, 0))
    return pl.pallas_call(
        _nsa_prep_kernel,
        grid=(B, S // tm),
        in_specs=[pl.BlockSpec((1, tm, 2 * LANES), lambda b, i: (b, i, 1)),
                  pl.BlockSpec((1, tm, 2 * LANES), lambda b, i: (b, i, 0))],
        out_specs=[pl.BlockSpec((1, NSA_KV, tm, 2 * LANES), lambda b, i: (b, 0, i, 0)), spec128, spec128],
        out_shape=[jax.ShapeDtypeStruct((B, NSA_KV, S, 2 * LANES), BF16),
                   jax.ShapeDtypeStruct((B, NSA_KV, S, LANES), BF16),
                   jax.ShapeDtypeStruct((B, NSA_KV, S, LANES), BF16)],
        compiler_params=_cparams(("arbitrary", "arbitrary")),
        name="nsa_prep",
    )(nkv, wkv)


def _overlap(i_blk, j_blk):
    start = i_blk * CMP_STRIDE
    return (start < j_blk * SEL_LEN + SEL_LEN) & (start + CMP_LEN > j_blk * SEL_LEN)


NEAR_BACK = 16


def _nsa_cmp_kernel(rb_ref, q_ref, kcw_ref, oc_ref, mq_ref, fc_ref):
    b = pl.program_id(0)
    i = pl.program_id(1)
    tq = q_ref.shape[1]
    n = kcw_ref.shape[2] - CMP_PAD_FRONT - CMP_PAD_BACK
    n_sel = n * CMP_STRIDE // SEL_LEN
    k_eff = min(N_SELECT, n_sel)

    @pl.when((b == 0) & (i == 0))
    def _():
        dist = _iota((tq, LANES), 0) + (NEAR_BACK * CMP_STRIDE - (CMP_LEN - 1)) - CMP_STRIDE * _iota((tq, LANES), 1)
        bk = _bucket(jnp.clip(dist, 0, MAX_DISTANCE - 1))
        for h in range(H_NSA):
            fc_ref[h] = jnp.where(dist >= 0, _bias_lookup(bk, lambda bb, h=h: rb_ref[bb, h]), NEG)

    qs = i * tq
    i0 = qs // CMP_STRIDE - NEAR_BACK
    q = q_ref[0] * SCALE
    lanehead = _div(_iota((tq, NSA_W), 1), HD)
    kfar = kcw_ref[0, 0, CMP_PAD_FRONT:CMP_PAD_FRONT + n, :]
    vfar = kcw_ref[0, 1, CMP_PAD_FRONT:CMP_PAD_FRONT + n, :]
    st = pl.multiple_of(qs // CMP_STRIDE, 16)
    knear = kcw_ref[0, 0, pl.ds(st, LANES), :]
    vnear = kcw_ref[0, 1, pl.ds(st, LANES), :]
    farmask = _iota((tq, n), 1) < i0
    nearmask = (_iota((tq, LANES), 1) + i0) >= 0
    oc = jnp.zeros((tq, NSA_W), F32)
    ps_far = [jnp.zeros((tq, n), F32) for _ in range(NSA_KV)]
    ps_near = [jnp.zeros((tq, LANES), F32) for _ in range(NSA_KV)]
    for h in range(H_NSA):
        g = h // NSA_GH
        qm = jnp.where(lanehead == h, q, 0.0).astype(BF16)
        sf = jnp.where(farmask, _dot_nt(qm, kfar) + rb_ref[N_BUCKETS - 1, h], NEG)
        sn = jnp.where(nearmask, _dot_nt(qm, knear) + fc_ref[h], NEG)
        m = jnp.maximum(jnp.max(sf, axis=1, keepdims=True), jnp.max(sn, axis=1, keepdims=True))
        pf = jnp.where(sf > 0.5 * NEG, jnp.exp(sf - m), 0.0)
        pn = jnp.where(sn > 0.5 * NEG, jnp.exp(sn - m), 0.0)
        l = jnp.sum(pf, axis=1, keepdims=True) + jnp.sum(pn, axis=1, keepdims=True)
        inv = 1.0 / jnp.maximum(l, TINY)
        pf = pf * inv
        pn = pn * inv
        o = _dot(pf.astype(BF16), vfar) + _dot(pn.astype(BF16), vnear)
        oc = jnp.where(lanehead == h, o, oc)
        ps_far[g] = ps_far[g] + pf
        ps_near[g] = ps_near[g] + pn
    oc_ref[0] = oc

    mov_far = jnp.where(_overlap(_iota((n, LANES), 0), _iota((n, LANES), 1)), 1.0, 0.0).astype(BF16)
    mov_near = jnp.where(_overlap(_iota((LANES, LANES), 0) + i0, _iota((LANES, LANES), 1)), 1.0, 0.0).astype(BF16)
    qpos = qs + _iota((tq, LANES), 0)
    jb = _iota((tq, LANES), 1)
    qblk = _div(qpos, SEL_LEN)
    valid = jb * SEL_LEN <= qpos
    forced = (jb == 0) | (jb == qblk) | (jb == qblk - 1)
    scores = []
    for g in range(NSA_KV):
        p_slc = _dot2_l(ps_far[g], mov_far) + _dot2_l(ps_near[g], mov_near)
        scores.append(jnp.where(forced, FORCE_SCORE, jnp.where(valid, p_slc, -1.0)))
    st = jnp.concatenate([sc.T for sc in scores], axis=1)
    keep = jnp.where((_topk_mask(st, k_eff, axis=0) > 0.5) & (st >= 0.0), 0.0, NEG)
    for g in range(NSA_KV):
        mq_ref[0, g] = keep[:, g * tq:(g + 1) * tq].T.astype(BF16)


def _nsa_cmp(rb, qn, kcw):
    B, S, _ = qn.shape
    tq = min(256, S)
    npad = kcw.shape[2]
    return pl.pallas_call(
        _nsa_cmp_kernel,
        grid=(B, S // tq),
        in_specs=[pl.BlockSpec(memory_space=pltpu.SMEM),
                  pl.BlockSpec((1, tq, NSA_W), lambda b, i: (b, i, 0)),
                  pl.BlockSpec((1, 2, npad, NSA_W), lambda b, i: (b, 0, 0, 0))],
        out_specs=[pl.BlockSpec((1, tq, NSA_W), lambda b, i: (b, i, 0)),
                   pl.BlockSpec((1, NSA_KV, tq, LANES), lambda b, i: (b, 0, i, 0))],
        out_shape=[jax.ShapeDtypeStruct((B, S, NSA_W), F32),
                   jax.ShapeDtypeStruct((B, NSA_KV, S, LANES), BF16)],
        scratch_shapes=[pltpu.VMEM((H_NSA, tq, LANES), F32)],
        compiler_params=_cparams(("arbitrary", "arbitrary")),
        name="nsa_cmp",
    )(rb, qn, kcw)


def _stack_heads(q, g_rows, extra, qs_ref, scale=SCALE):
    tq = q.shape[0]
    lane = _iota((tq, LANES), 1)
    for hh in range(NSA_GH):
        a = (hh // 2) * LANES
        t = q[:, a:a + LANES]
        if hh % 2:
            t = pltpu.roll(t, HD, 1)
        left = jnp.where(lane < HD, t * scale, g_rows[hh:hh + 1, :]).astype(BF16)
        if extra is None:
            qs_ref[hh * tq:(hh + 1) * tq, :] = left
        else:
            qs_ref[hh * tq:(hh + 1) * tq, :] = jnp.concatenate([left, extra], axis=1)


def _unstack_heads(a, tq):
    lane = _iota((tq, LANES), 1)
    p0 = jnp.where(lane < HD, a[0:tq], a[tq:2 * tq])
    p1 = jnp.where(lane < HD, a[2 * tq:3 * tq], a[3 * tq:4 * tq])
    return jnp.concatenate([p0, p1], axis=1)


SEL_BACK = 128


def _nsa_sel_kernel(rb_ref, q_ref, mq_ref, brow_ref, ksa_ref, vt_ref, o_ref, dn_ref, qs_ref, m_ref, acc_ref):
    b = pl.program_id(0)
    g = pl.program_id(1)
    i = pl.program_id(2)
    tq = q_ref.shape[1]
    wn = tq + SEL_BACK

    @pl.when((b == 0) & (g == 0) & (i == 0))
    def _():
        dist = _iota((wn, tq), 1) + SEL_BACK - _iota((wn, tq), 0)
        bk = _bucket(jnp.clip(dist, 0, MAX_DISTANCE - 1))
        for h in range(H_NSA):
            far = rb_ref[N_BUCKETS - 1, h]
            val = (_bias_lookup(bk, lambda bb, h=h: rb_ref[bb, h]) - far) * LOG2E
            dn_ref[h // NSA_GH, :, (h % NSA_GH) * tq:(h % NSA_GH + 1) * tq] = jnp.where(dist >= 0, val, NEG)

    qs = i * tq
    _stack_heads(q_ref[0], brow_ref[0], mq_ref[0, 0], qs_ref, scale=SCALE * LOG2E)
    _softmax_init_t(m_ref, acc_ref)
    hw = (NSA_GH // 2) * tq

    def tile(start, size, bias):
        k = ksa_ref[0, 0, pl.ds(start, size), :]
        vt = vt_ref[0, 0, :, pl.ds(start, size)]
        for half in range(2):
            s = _dot_nt(k, qs_ref[half * hw:(half + 1) * hw, :])
            if bias is not None:
                s = s + bias(half)
            _softmax_update_t(s, vt, m_ref.at[half], acc_ref.at[half])

    @pl.when(i == 0)
    def _():
        tile(0, tq, lambda half: dn_ref[g, SEL_BACK:, half * hw:(half + 1) * hw])

    @pl.when(i > 0)
    def _():
        tile(pl.multiple_of(qs - SEL_BACK, LANES), wn, lambda half: dn_ref[g, :, half * hw:(half + 1) * hw])
        tile(pl.multiple_of(qs - tq, LANES), tq - SEL_BACK, None)

    def far_tile(kt, c):
        tile(pl.multiple_of(kt * tq, tq), tq, None)
        return c

    lax.fori_loop(0, jnp.maximum(i - 1, 0), far_tile, 0)
    parts = []
    for half in range(2):
        acc = acc_ref[half]
        a = acc[0:HD, :] / jnp.maximum(acc[HD:HD + 1, :], TINY)
        parts += [a[:, 0:tq], a[:, tq:2 * tq]]
    o_ref[0] = jnp.concatenate(parts, axis=0).T


def _nsa_sel(rb, qn, maskq, brow, ksa, vst):
    B, S, _ = qn.shape
    tq = min(512, S)
    G = NSA_KV
    return pl.pallas_call(
        _nsa_sel_kernel,
        grid=(B, G, S // tq),
        in_specs=[pl.BlockSpec(memory_space=pltpu.SMEM),
                  pl.BlockSpec((1, tq, 2 * LANES), lambda b, g, i: (b, i, g)),
                  pl.BlockSpec((1, 1, tq, LANES), lambda b, g, i: (b, g, i, 0)),
                  pl.BlockSpec((1, NSA_GH, LANES), lambda b, g, i: (g, 0, 0)),
                  pl.BlockSpec((1, 1, S, 2 * LANES), lambda b, g, i: (b, g, 0, 0)),
                  pl.BlockSpec((1, 1, HD + ONES_ROWS, S), lambda b, g, i: (b, g, 0, 0))],
        out_specs=pl.BlockSpec((1, tq, 2 * LANES), lambda b, g, i: (b, i, g)),
        out_shape=jax.ShapeDtypeStruct((B, S, NSA_W), F32),
        scratch_shapes=[pltpu.VMEM((G, tq + SEL_BACK, NSA_GH * tq), F32),
                        pltpu.VMEM((NSA_GH * tq, 2 * LANES), BF16),
                        pltpu.VMEM((2, 1, NSA_GH // 2 * tq), F32),
                        pltpu.VMEM((2, HD + ONES_ROWS, NSA_GH // 2 * tq), F32)],
        compiler_params=_cparams(("arbitrary", "arbitrary", "arbitrary")),
        name="nsa_sel",
    )(rb, qn, maskq, brow, ksa, vst)


def _nsa_win_kernel(rb_ref, q_ref, kwa_ref, vwd_ref, o_ref, dw_ref, qs_ref):
    b = pl.program_id(0)
    g = pl.program_id(1)
    i = pl.program_id(2)
    tq = q_ref.shape[1]
    wk = tq + WINDOW

    @pl.when((b == 0) & (g == 0) & (i == 0))
    def _():
        dist = _iota((tq, wk), 0) + WINDOW - _iota((tq, wk), 1)
        bk = _bucket(jnp.clip(dist, 0, MAX_DISTANCE - 1))
        ok = (dist >= 0) & (dist < WINDOW)
        for h in range(H_NSA):
            val = _bias_lookup(bk, lambda bb, h=h: rb_ref[bb, h])
            dw_ref[h // NSA_GH, (h % NSA_GH) * tq:(h % NSA_GH + 1) * tq, :] = jnp.where(ok, val, NEG)

    qs = pl.multiple_of(i * tq, tq)
    _stack_heads(q_ref[0], jnp.zeros((NSA_GH, LANES), F32), None, qs_ref)
    s = _dot_nt(qs_ref[...], kwa_ref[0, 0, pl.ds(qs, wk), :]) + dw_ref[g]
    s = jnp.where(_iota(s.shape, 1) + qs >= WINDOW, s, NEG)
    m = jnp.max(s, axis=1, keepdims=True)
    p = jnp.exp(s - m)
    l = jnp.sum(p, axis=1, keepdims=True)
    a = _dot(p.astype(BF16), vwd_ref[0, 0, pl.ds(qs, wk), :]) / jnp.maximum(l, TINY)
    o_ref[0] = _unstack_heads(a, tq)


def _nsa_win(rb, qn, kwa_p, vwd_p):
    B, S, _ = qn.shape
    tq = min(256, S)
    G = NSA_KV
    sp = kwa_p.shape[2]
    return pl.pallas_call(
        _nsa_win_kernel,
        grid=(B, G, S // tq),
        in_specs=[pl.BlockSpec(memory_space=pltpu.SMEM),
                  pl.BlockSpec((1, tq, 2 * LANES), lambda b, g, i: (b, i, g)),
                  pl.BlockSpec((1, 1, sp, LANES), lambda b, g, i: (b, g, 0, 0)),
                  pl.BlockSpec((1, 1, sp, LANES), lambda b, g, i: (b, g, 0, 0))],
        out_specs=pl.BlockSpec((1, tq, 2 * LANES), lambda b, g, i: (b, i, g)),
        out_shape=jax.ShapeDtypeStruct((B, S, NSA_W), F32),
        scratch_shapes=[pltpu.VMEM((G, NSA_GH * tq, tq + WINDOW), F32),
                        pltpu.VMEM((NSA_GH * tq, LANES), BF16)],
        compiler_params=_cparams(("arbitrary", "arbitrary", "arbitrary")),
        name="nsa_win",
    )(rb, qn, kwa_p, vwd_p)


def _merge_kernel(x_ref, of_ref, oc_ref, os_ref, ow_ref, misc_ref, g_ref, w_ref, out_ref):
    tm = x_ref.shape[0]
    hi, lo = _split2(misc_ref[...])
    er = _iota((LANES, NSA_W), 0)
    ec = _iota((LANES, NSA_W), 1)
    onsa = jnp.zeros((tm, NSA_W), F32)
    for k, o_ref in enumerate((oc_ref, os_ref, ow_ref)):
        e = jnp.where(er == MISC_GATE + k * H_NSA + _div(ec, HD), 1.0, 0.0).astype(BF16)
        onsa = onsa + (_dot(hi, e) + _dot(lo, e)) * o_ref[...]
    g = g_ref[...]
    a = _rms(of_ref[...], g[:, :FOX_W]).astype(BF16)
    c = _rms(onsa, g[:, FOX_W:]).astype(BF16)
    out_ref[...] = x_ref[...] + _dot(a, w_ref[0:FOX_W, :]) + _dot(c, w_ref[FOX_W:, :])


def _merge(x2, ofox, oc, os_, ow, misc, g, w):
    T, D = x2.shape
    tm = min(512, T)
    row = lambda wd: pl.BlockSpec((tm, wd), lambda i: (i, 0))
    return pl.pallas_call(
        _merge_kernel,
        grid=(T // tm,),
        in_specs=[row(D), row(FOX_W), row(NSA_W), row(NSA_W), row(NSA_W), row(LANES),
                  pl.BlockSpec((1, D), lambda i: (0, 0)),
                  pl.BlockSpec((D, D), lambda i: (0, 0))],
        out_specs=row(D),
        out_shape=jax.ShapeDtypeStruct((T, D), F32),
        compiler_params=_cparams(("arbitrary",)),
        name="merge",
    )(x2, ofox, oc, os_, ow, misc, g, w)


FF_CHUNK = 256


def _ffn_kernel(*refs, seq_len, short_len, final):
    if seq_len is None:
        (x_ref, g2_ref, wg_ref, wu_ref, cw_ref, cb_ref, wd_ref, gf_ref, hm1_ref, hm2_ref,
         out_ref, gt_ref, h_sc, acc_sc) = refs
    else:
        (x_ref, g2_ref, wg_ref, wu_ref, cw_ref, cb_ref, wd_ref, gf_ref,
         out_ref, gt_ref, h_sc, acc_sc, carry_sc) = refs
    i = pl.program_id(0)
    j = pl.program_id(1)
    tm = x_ref.shape[0]

    @pl.when(j == 0)
    def _():
        h_sc[...] = _rms(x_ref[...], g2_ref[...]).astype(BF16)
        acc_sc[...] = jnp.zeros_like(acc_sc)

    h = h_sc[...]
    gch = _dot(h, wg_ref[...])
    u = _dot(h, wu_ref[...])
    r1 = pltpu.roll(gch, 1, 0)
    r2 = pltpu.roll(gch, 2, 0)
    row = _iota(gch.shape, 0)
    if seq_len is None:
        t = _mod(row, short_len)
        m1 = jnp.where(t == 0, hm1_ref[...], r1)
        m2 = jnp.where(t < 2, hm2_ref[...], r2)
        gt_ref[...] = gch
    else:
        first = (i % (seq_len // tm)) == 0
        c = jnp.where(first, 0.0, carry_sc[j])
        m1 = jnp.where(row == 0, c[1:2, :], r1)
        m2 = jnp.where(row == 0, c[0:1, :], jnp.where(row == 1, c[1:2, :], r2))
        carry_sc[j, 0:2, :] = gch[tm - 2:tm, :]
        gt_ref[0] = gch[tm - 8:tm, :]
    cw = cw_ref[...]
    gc = cb_ref[...] + cw[0:1, :] * m2 + cw[1:2, :] * m1 + cw[2:3, :] * gch
    act = gc * (1.0 / (1.0 + jnp.exp(-gc)))
    acc_sc[...] += _dot((act * u).astype(BF16), wd_ref[...])

    @pl.when(j == pl.num_programs(1) - 1)
    def _():
        y = x_ref[...] + acc_sc[...]
        if final:
            y = _rms(y, gf_ref[...])
        out_ref[...] = y


def _ffn(x2, g2, wgu, cw, cb, wd, gf, *, seq_len, final, hist=None, short_len=None):
    T, D = x2.shape
    dff = wd.shape[0]
    fc = FF_CHUNK
    nff = dff // fc
    tm = min(1024, T) if seq_len is not None else T
    nt = T // tm
    in_specs = [pl.BlockSpec((tm, D), lambda i, j: (i, 0)),
                pl.BlockSpec((1, D), lambda i, j: (0, 0)),
                pl.BlockSpec((D, fc), lambda i, j: (0, j)),
                pl.BlockSpec((D, fc), lambda i, j: (0, nff + j)),
                pl.BlockSpec((CONV_W, fc), lambda i, j: (0, j)),
                pl.BlockSpec((1, fc), lambda i, j: (0, j)),
                pl.BlockSpec((fc, D), lambda i, j: (j, 0)),
                pl.BlockSpec((1, D), lambda i, j: (0, 0))]
    args = [x2, g2, wgu, wgu, cw, cb, wd, gf]
    scratch = [pltpu.VMEM((tm, D), BF16), pltpu.VMEM((tm, D), F32)]
    if seq_len is None:
        in_specs += [pl.BlockSpec((tm, fc), lambda i, j: (i, j)), pl.BlockSpec((tm, fc), lambda i, j: (i, j))]
        args += list(hist)
        gt_spec = pl.BlockSpec((tm, fc), lambda i, j: (i, j))
        gt_shape = jax.ShapeDtypeStruct((T, dff), F32)
    else:
        scratch.append(pltpu.VMEM((nff, 8, fc), F32))
        gt_spec = pl.BlockSpec((1, 8, fc), lambda i, j: (i, 0, j))
        gt_shape = jax.ShapeDtypeStruct((nt, 8, dff), F32)
    return pl.pallas_call(
        functools.partial(_ffn_kernel, seq_len=seq_len, short_len=short_len, final=final),
        grid=(nt, nff),
        in_specs=in_specs,
        out_specs=[pl.BlockSpec((tm, D), lambda i, j: (i, 0)), gt_spec],
        out_shape=[jax.ShapeDtypeStruct((T, D), F32), gt_shape],
        scratch_shapes=scratch,
        compiler_params=_cparams(("arbitrary", "arbitrary")),
        name="ffn",
    )(*args)


def _rows_th(q):
    tq = q.shape[0]
    rows = jnp.concatenate([jnp.broadcast_to(q[t:t + 1, :], (8, q.shape[1])) for t in range(tq)], axis=0)
    keep = _div(_iota(rows.shape, 1), HD) == _mod(_iota(rows.shape, 0), 8)
    return jnp.where(keep, rows * SCALE, 0.0).astype(BF16)


def _diag_rows(o_ref, o32):
    keep = _div(_iota(o32.shape, 1), HD) == _mod(_iota(o32.shape, 0), 8)
    od = jnp.where(keep, o32, 0.0)
    for t in range(o32.shape[0] // 8):
        o_ref[0, t:t + 1, :] = jnp.sum(od[t * 8:(t + 1) * 8, :], axis=0, keepdims=True)


FOX_PAGES = 16


def _fox_sample_kernel(pt_tab, *refs):
    P = FOX_PAGES
    q_ref, kvn_ref, lfn_ref = refs[0:3]
    k_refs = refs[3:3 + P]
    v_refs = refs[3 + P:3 + 2 * P]
    lf_refs = refs[3 + 2 * P:3 + 3 * P]
    o_ref, q_sc, m_ref, l_ref, acc_ref, carry_ref, new_sc = refs[3 + 3 * P:]
    s = pl.program_id(1)
    tq = q_ref.shape[1]
    nr = tq * 8

    @pl.when(s == 0)
    def _():
        q_sc[...] = _rows_th(q_ref[0])
        _softmax_init(m_ref, l_ref, acc_ref)
        carry_ref[...] = jnp.zeros_like(carry_ref)
        new_sc[...] = jnp.zeros_like(new_sc)
        new_sc[0:tq, :] = kvn_ref[0]

    qq = q_sc[...]
    triu = (_iota((PAGE, PAGE), 0) <= _iota((PAGE, PAGE), 1)).astype(BF16)

    def attend(score_fns, value_fns, lfs, extra_mask):
        off = carry_ref[...]
        sc = []
        for fn, lf in zip(score_fns, lfs):
            cum = _dot3_l(lf, triu) + off
            off = jnp.broadcast_to(cum[:, PAGE - 1:PAGE], cum.shape)
            sc.append(fn() - jnp.concatenate([cum] * tq, axis=0))
        carry_ref[...] = off
        sc = jnp.concatenate(sc, axis=1)
        if extra_mask is not None:
            sc = jnp.where(extra_mask, sc, NEG)
        m_old = m_ref[...]
        m_new = jnp.maximum(m_old, jnp.max(sc, axis=1, keepdims=True))
        alpha = jnp.exp(m_old - m_new)
        p = jnp.exp(sc - m_new)
        l_ref[...] = alpha * l_ref[...] + jnp.sum(p, axis=1, keepdims=True)
        acc = alpha * acc_ref[...]
        for k, fn in enumerate(value_fns):
            acc = acc + fn(p[:, k * PAGE:(k + 1) * PAGE].astype(BF16))
        acc_ref[...] = acc
        m_ref[...] = m_new

    attend([lambda r=r: _dot(qq, r[...].reshape(FOX_W, PAGE).astype(BF16)) for r in k_refs],
           [lambda p, r=r: _dot_nt(p, r[...].reshape(FOX_W, PAGE).astype(BF16)) for r in v_refs],
           [r[...] for r in lf_refs], None)

    @pl.when(s == pl.num_programs(1) - 1)
    def _():
        key = _iota((nr, PAGE), 1)
        ok = (key < tq) & (key <= _div(_iota((nr, PAGE), 0), 8))
        new = new_sc[...]
        attend([lambda: _dot_nt(qq, new[:, :FOX_W].astype(BF16))],
               [lambda p: _dot(p, new[:, FOX_W:].astype(BF16))], [lfn_ref[0]], ok)
        _diag_rows(o_ref, acc_ref[...] / jnp.maximum(l_ref[...], TINY))


def _fox_sample(qf, fkv_new, lfn_t, cache_kt, cache_lft, page_table, layer):
    nb, npages = page_table.shape
    tq = qf.shape[1]
    P = FOX_PAGES
    kvw = 2 * FOX_W
    seq = lambda b, s, pt: (b, 0, 0)

    def kv_spec(k, c):
        return pl.BlockSpec((None, None, None, H_FOX, HD, PAGE),
                            lambda b, s, pt: (layer, pt[b, s * P + k], c, 0, 0, 0))

    def lf_spec(k):
        return pl.BlockSpec((None, None, H_FOX, PAGE), lambda b, s, pt: (layer, pt[b, s * P + k], 0, 0))

    return pl.pallas_call(
        _fox_sample_kernel,
        grid_spec=pltpu.PrefetchScalarGridSpec(
            num_scalar_prefetch=1,
            grid=(nb, npages // P),
            in_specs=[pl.BlockSpec((1, tq, FOX_W), seq), pl.BlockSpec((1, tq, kvw), seq),
                      pl.BlockSpec((1, H_FOX, PAGE), seq)]
            + [kv_spec(k, 0) for k in range(P)] + [kv_spec(k, 1) for k in range(P)]
            + [lf_spec(k) for k in range(P)],
            out_specs=pl.BlockSpec((1, tq, FOX_W), seq),
            scratch_shapes=[pltpu.VMEM((tq * 8, FOX_W), BF16),
                            pltpu.VMEM((tq * 8, 1), F32), pltpu.VMEM((tq * 8, 1), F32),
                            pltpu.VMEM((tq * 8, FOX_W), F32),
                            pltpu.VMEM((H_FOX, PAGE), F32),
                            pltpu.VMEM((PAGE, kvw), F32)]),
        out_shape=jax.ShapeDtypeStruct((nb, tq, FOX_W), F32),
        compiler_params=_cparams(("arbitrary", "arbitrary")),
        name="fox_sample",
    )(page_table, qf, fkv_new, lfn_t, *([cache_kt] * (2 * P)), *([cache_lft] * P))


def _place_wide():
    r = _iota((LANES, NSA_W), 0)
    c = _iota((LANES, NSA_W), 1)
    return jnp.where(r == _div(c, NSA_GH * HD) * HD + _mod(c, HD), 1.0, 0.0).astype(BF16)


def _rb_col(rbt_ref):
    return lambda bb: rbt_ref[:, bb:bb + 1]


def _fold_heads():
    r = _iota((NSA_W, LANES), 0)
    c = _iota((NSA_W, LANES), 1)
    return jnp.where(c == _div(r, NSA_GH * HD) * HD + _mod(r, HD), 1.0, 0.0).astype(BF16)


def _ns_attend_kernel(q_ref, kcw_ref, kwt_ref, vwt_ref, wn_ref, rbt_ref, oc_ref, ow_ref, mb_ref, new_sc, *, past):
    tq = q_ref.shape[1]
    nr = tq * 8
    n = kcw_ref.shape[2] - CMP_PAD_FRONT - CMP_PAD_BACK
    n_selp = past // SEL_LEN
    wb = kwt_ref.shape[2]
    qq = _rows_th(q_ref[0])
    kc = kcw_ref[0, 0, CMP_PAD_FRONT:CMP_PAD_FRONT + n, :]
    vc = kcw_ref[0, 1, CMP_PAD_FRONT:CMP_PAD_FRONT + n, :]
    trow = _div(_iota((nr, n), 0), 8)
    dist = past + trow - CMP_STRIDE * _iota((nr, n), 1) - (CMP_LEN - 1)
    bias = _bias_lookup(_bucket(jnp.clip(dist, 0, MAX_DISTANCE - 1)), _rb_col(rbt_ref))
    ok = dist >= 0
    s = jnp.where(ok, _dot_nt(qq, kc) + bias, NEG)
    m = jnp.max(s, axis=1, keepdims=True)
    p = jnp.where(ok, jnp.exp(s - m), 0.0)
    p = p / jnp.maximum(jnp.sum(p, axis=1, keepdims=True), TINY)
    _diag_rows(oc_ref, _dot(p.astype(BF16), vc))
    mov = jnp.where(_overlap(_iota((n, LANES), 0), _iota((n, LANES), 1)), 1.0, 0.0).astype(BF16)
    x = _dot2_l(p, mov)
    z = x + pltpu.roll(x, nr - 1, 0) + pltpu.roll(x, nr - 2, 0) + pltpu.roll(x, nr - 3, 0)
    z0 = jnp.where(_mod(_iota(z.shape, 0), NSA_GH) == 0, z, 0.0)
    p_slc = z0 + pltpu.roll(z0, 1, 0) + pltpu.roll(z0, 2, 0) + pltpu.roll(z0, 3, 0)
    jb = _iota((nr, LANES), 1)
    forced = (jb == 0) | (jb == n_selp - 1)
    score = jnp.where(forced, FORCE_SCORE, jnp.where(jb < n_selp, p_slc, -1.0))
    k_past = min(N_SELECT, n_selp + 1) - 1
    sel = _topk_mask(score, k_past)
    mb_ref[0] = jnp.where((sel > 0.5) & (score >= 0.0), 0.0, NEG).astype(BF16)
    qf = _dot(qq, _fold_heads()).astype(BF16)
    kwt = kwt_ref[...].reshape(LANES, wb).astype(BF16)
    vwt = vwt_ref[...].reshape(LANES, wb).astype(BF16)
    new_sc[...] = jnp.zeros_like(new_sc)
    new_sc[0:tq, :] = wn_ref[0]
    wnew = new_sc[...]
    dw = _div(_iota((nr, wb), 0), 8) + wb - _iota((nr, wb), 1)
    okw = (dw >= 0) & (dw < WINDOW)
    sw = _dot(qf, kwt) + _bias_lookup(_bucket(jnp.clip(dw, 0, MAX_DISTANCE - 1)), _rb_col(rbt_ref))
    sw = jnp.where(okw, sw, NEG)
    dn = _div(_iota((nr, PAGE), 0), 8) - _iota((nr, PAGE), 1)
    okn = (dn >= 0) & (_iota((nr, PAGE), 1) < tq)
    sn = _dot_nt(qf, wnew[:, 0:LANES].astype(BF16))
    sn = sn + _bias_lookup(_bucket(jnp.clip(dn, 0, MAX_DISTANCE - 1)), _rb_col(rbt_ref))
    sn = jnp.where(okn, sn, NEG)
    mw = jnp.maximum(jnp.max(sw, axis=1, keepdims=True), jnp.max(sn, axis=1, keepdims=True))
    pw = jnp.where(okw, jnp.exp(sw - mw), 0.0)
    pn = jnp.where(okn, jnp.exp(sn - mw), 0.0)
    lw = jnp.maximum(jnp.sum(pw, axis=1, keepdims=True) + jnp.sum(pn, axis=1, keepdims=True), TINY)
    a = (_dot_nt(pw.astype(BF16), vwt) + _dot(pn.astype(BF16), wnew[:, LANES:2 * LANES].astype(BF16))) / lw
    _diag_rows(ow_ref, _dot2_l(a, _place_wide()))


def _ns_attend(qn, kcw, win_t, wkv_new, rbt, layer, past):
    nb, tq, _ = qn.shape
    npad = kcw.shape[2]
    wb = win_t.shape[5]
    seq = lambda b: (b, 0, 0)

    def win_spec(c):
        return pl.BlockSpec((None, None, None, NSA_KV, HD, wb), lambda b: (layer, b, c, 0, 0, 0))

    return pl.pallas_call(
        functools.partial(_ns_attend_kernel, past=past),
        grid=(nb,),
        in_specs=[pl.BlockSpec((1, tq, NSA_W), seq),
                  pl.BlockSpec((1, 2, npad, NSA_W), lambda b: (b, 0, 0, 0)),
                  win_spec(0), win_spec(1),
                  pl.BlockSpec((1, tq, 2 * LANES), seq),
                  pl.BlockSpec((tq * 8, N_BUCKETS), lambda b: (0, 0))],
        out_specs=[pl.BlockSpec((1, tq, NSA_W), seq), pl.BlockSpec((1, tq, NSA_W), seq),
                   pl.BlockSpec((1, tq * 8, LANES), seq)],
        out_shape=[jax.ShapeDtypeStruct((nb, tq, NSA_W), F32), jax.ShapeDtypeStruct((nb, tq, NSA_W), F32),
                   jax.ShapeDtypeStruct((nb, tq * 8, LANES), BF16)],
        scratch_shapes=[pltpu.VMEM((PAGE, 2 * LANES), F32)],
        compiler_params=_cparams(("arbitrary",)),
        name="ns_attend",
    )(qn, kcw, win_t, win_t, wkv_new, rbt)


def _ns_select_kernel(pt_tab, *refs, past):
    P = PAGES_PER_STEP
    q_ref, mb_ref, nn_ref, rbt_ref = refs[0:4]
    k_refs = refs[4:4 + P]
    v_refs = refs[4 + P:4 + 2 * P]
    o_ref, q_sc, m_ref, l_ref, acc_ref, new_sc = refs[4 + 2 * P:]
    s = pl.program_id(1)
    tq = q_ref.shape[1]
    nr = tq * 8

    @pl.when(s == 0)
    def _():
        q_sc[...] = _dot(_rows_th(q_ref[0]), _fold_heads()).astype(BF16)
        _softmax_init(m_ref, l_ref, acc_ref)
        new_sc[...] = jnp.zeros_like(new_sc)
        new_sc[0:tq, :] = nn_ref[0][:, 2 * LANES:]

    qq = q_sc[...]
    trow = _div(_iota((nr, PAGE), 0), 8)
    key = _iota((nr, PAGE), 1)

    def attend(score_fns, value_fns, biases):
        sc = jnp.concatenate([fn() for fn in score_fns], axis=1) + biases
        m_old = m_ref[...]
        m_new = jnp.maximum(m_old, jnp.max(sc, axis=1, keepdims=True))
        alpha = jnp.exp(m_old - m_new)
        p = jnp.exp(sc - m_new)
        l_ref[...] = alpha * l_ref[...] + jnp.sum(p, axis=1, keepdims=True)
        acc = alpha * acc_ref[...]
        for k, fn in enumerate(value_fns):
            acc = acc + fn(p[:, k * PAGE:(k + 1) * PAGE].astype(BF16))
        acc_ref[...] = acc
        m_ref[...] = m_new

    blk = _iota((LANES, P * PAGE), 0)
    kcol = _iota((LANES, P * PAGE), 1)
    expand = jnp.where(blk == _div(s * (P * PAGE) + kcol, SEL_LEN), 1.0, 0.0).astype(BF16)
    bias = _dot(mb_ref[0], expand) + rbt_ref[:, N_BUCKETS - 1:N_BUCKETS]
    tiles = ([lambda r=r: _dot(qq, r[...].reshape(LANES, PAGE).astype(BF16)) for r in k_refs],
             [lambda p, r=r: _dot_nt(p, r[...].reshape(LANES, PAGE).astype(BF16)) for r in v_refs])

    @pl.when(s < pl.num_programs(1) - 1)
    def _():
        attend(*tiles, bias)

    @pl.when(s == pl.num_programs(1) - 1)
    def _():
        d_last = past + trow - (past - PAGE + key)
        b_last = _bias_lookup(_bucket(jnp.clip(d_last, 0, MAX_DISTANCE - 1)), _rb_col(rbt_ref))
        fix = jnp.concatenate([jnp.zeros((nr, (P - 1) * PAGE), F32),
                               b_last - rbt_ref[:, N_BUCKETS - 1:N_BUCKETS]], axis=1)
        attend(*tiles, bias + fix)
        d_new = trow - key
        b_new = _bias_lookup(_bucket(jnp.clip(d_new, 0, MAX_DISTANCE - 1)), _rb_col(rbt_ref))
        new = new_sc[...]
        attend([lambda: _dot_nt(qq, new[:, 0:LANES].astype(BF16))],
               [lambda p: _dot(p, new[:, LANES:2 * LANES].astype(BF16))],
               jnp.where((d_new >= 0) & (key < tq), b_new, NEG))
        a = acc_ref[...] / jnp.maximum(l_ref[...], TINY)
        place = _place_wide()
        _diag_rows(o_ref, _dot2_l(a, place))


def _ns_select(qn, maskb, nkv_new, rbt, cache_nsa, page_table, layer, past):
    nb, npages = page_table.shape
    tq = qn.shape[1]
    P = PAGES_PER_STEP
    seq = lambda b, s, pt: (b, 0, 0)

    def page_spec(k, c):
        return pl.BlockSpec((None, None, None, NSA_KV, HD, PAGE),
                            lambda b, s, pt: (layer, pt[b, s * P + k], c, 0, 0, 0))

    return pl.pallas_call(
        functools.partial(_ns_select_kernel, past=past),
        grid_spec=pltpu.PrefetchScalarGridSpec(
            num_scalar_prefetch=1,
            grid=(nb, npages // P),
            in_specs=[pl.BlockSpec((1, tq, NSA_W), seq), pl.BlockSpec((1, tq * 8, LANES), seq),
                      pl.BlockSpec((1, tq, 4 * LANES), seq),
                      pl.BlockSpec((tq * 8, N_BUCKETS), lambda b, s, pt: (0, 0))]
            + [page_spec(k, 2) for k in range(P)] + [page_spec(k, 3) for k in range(P)],
            out_specs=pl.BlockSpec((1, tq, NSA_W), seq),
            scratch_shapes=[pltpu.VMEM((tq * 8, LANES), BF16),
                            pltpu.VMEM((tq * 8, 1), F32), pltpu.VMEM((tq * 8, 1), F32),
                            pltpu.VMEM((tq * 8, LANES), F32),
                            pltpu.VMEM((PAGE, 2 * LANES), F32)]),
        out_shape=jax.ShapeDtypeStruct((nb, tq, NSA_W), F32),
        compiler_params=_cparams(("arbitrary", "arbitrary")),
        name="ns_select",
    )(page_table, qn, maskb, nkv_new, rbt, *([cache_nsa] * (2 * P)))


def _prep_w_in(w):
    d = w.shape[0]
    o_logf = 3 * FOX_W
    o_qn = o_logf + H_FOX
    o_kv = o_qn + NSA_W
    o_gate = o_kv + 6 * NSA_KV * HD
    misc = jnp.concatenate([w[:, o_logf:o_qn], w[:, o_gate:o_gate + 3 * H_NSA],
                            jnp.zeros((d, LANES - H_FOX - 3 * H_NSA), w.dtype)], axis=1)
    return jnp.concatenate([w[:, :o_logf], w[:, o_qn:o_kv], w[:, o_kv:o_gate], misc], axis=1).astype(BF16)


def _prep_cmp(pos, w1, b1, w2):
    eye = jnp.eye(NSA_KV, dtype=w1.dtype)
    w1r = w1.reshape(2, 2, CMP_STRIDE, HD, CMP_HID)
    wide = jnp.einsum("whjdc,ab->whjadbc", w1r, eye).reshape(2, 2, CMP_STRIDE * LANES, 2 * CMP_HID)
    posr = jnp.broadcast_to(pos.reshape(2, 2, CMP_STRIDE, 1, HD), (2, 2, CMP_STRIDE, NSA_KV, HD))
    posr = posr.reshape(2, 2, 1, CMP_STRIDE * LANES)
    group_of_head = (jnp.arange(H_NSA) // NSA_GH)[None, :] == jnp.arange(NSA_KV)[:, None]
    w2w = jnp.einsum("wcd,ah->wachd", w2, group_of_head.astype(w2.dtype)).reshape(2, 2 * CMP_HID, NSA_W)
    return {"pt": posr[:, 0], "pb": posr[:, 1],
            "w1t": wide[:, 0].astype(BF16), "w1b": wide[:, 1].astype(BF16),
            "b1": jnp.concatenate([b1, b1], axis=-1)[:, None, :],
            "w2w": w2w.astype(BF16)}


def _far_bias_rows(rel_bias):
    far = rel_bias[N_BUCKETS - 1]
    hi = far.astype(BF16).astype(F32)
    r = far - hi
    mid = r.astype(BF16).astype(F32)
    lo = (r - mid).astype(BF16).astype(F32)
    rows = jnp.zeros((H_NSA, LANES), F32)
    rows = rows.at[:, HD].set(hi).at[:, HD + 1].set(mid).at[:, HD + 2].set(lo)
    return rows.reshape(NSA_KV, NSA_GH, LANES)


def kernel(x_prompt, x_sample, cache_fox_kv, cache_fox_logf, cache_nsa_kv, state_win_kv, state_conv,
           page_table, norm1_g, w_in, b_forget, cmp_pos, cmp_w1, cmp_b1, cmp_w2, out_norm_g, w_out,
           norm2_g, w_gu, conv_w, conv_b, w_down, rel_bias, final_norm_g):
    B, S, D = x_prompt.shape
    nb, tq, _ = x_sample.shape
    depth = w_in.shape[0]
    n_pool = cache_fox_kv.shape[1]
    npages = page_table.shape[1]
    past = npages * PAGE
    dff = w_down.shape[1]
    wb = state_win_kv.shape[2]
    assert tq & (tq - 1) == 0 and tq >= CONV_W - 1
    assert S % 256 == 0 and S // SEL_LEN <= LANES and past // SEL_LEN <= LANES
    assert npages % PAGES_PER_STEP == 0 and wb == WINDOW and past >= WINDOW and tq * 8 <= LANES

    assert npages % FOX_PAGES == 0 and PAGE == LANES
    cache_kt = jnp.transpose(cache_fox_kv, (0, 1, 3, 4, 5, 2))
    cache_lft = jnp.swapaxes(cache_fox_logf, 2, 3)
    cache_nsa = jnp.transpose(cache_nsa_kv, (0, 1, 3, 4, 5, 2))
    win_t = jnp.transpose(state_win_kv, (0, 1, 3, 4, 5, 2))
    rbt = jnp.tile(rel_bias.T, (tq, 1))
    brow = _far_bias_rows(rel_bias * LOG2E)
    gf = final_norm_g.reshape(1, D)

    xp = x_prompt.reshape(B * S, D)
    xs = x_sample.reshape(nb * tq, D)
    outs = [[] for _ in range(10)]
    for l in range(depth):
        w_l = _prep_w_in(w_in[l])
        bf = jnp.zeros((1, LANES), F32).at[0, :H_FOX].set(b_forget[l])
        cw = _prep_cmp(cmp_pos[l], cmp_w1[l], cmp_b1[l], cmp_w2[l])
        g1 = norm1_g[l].reshape(1, D)
        g2 = norm2_g[l].reshape(1, D)
        go = out_norm_g[l].reshape(1, D)
        wo = w_out[l].astype(BF16)
        wgu = w_gu[l].astype(BF16)
        wd = w_down[l].astype(BF16)
        cb = conv_b[l].reshape(1, dff)
        final = l == depth - 1

        qf, fkv, qn, nkv, wkv, misc = _proj(xp, g1, w_l, bf)
        r3 = lambda a: a.reshape(B, S, a.shape[-1])
        qa, ka, vb = _fox_prep(r3(qf), r3(fkv), r3(misc))
        vt = jnp.swapaxes(vb, 1, 2).reshape(B, H_FOX // 2, 2 * HD, S)
        vt = jnp.concatenate([vt, jnp.ones((B, H_FOX // 2, ONES_ROWS, S), BF16)], axis=2)
        o_fox = _fox_flash(qa, ka, vt)
        kcw = _compress_prompt(r3(nkv), cw)
        ksa, kwa, vwd = _nsa_prep(r3(nkv), r3(wkv))
        vst = r3(nkv)[:, :, 3 * LANES:].astype(BF16).reshape(B, S, NSA_KV, HD).transpose(0, 2, 3, 1)
        vst = jnp.concatenate([vst, jnp.ones((B, NSA_KV, ONES_ROWS, S), BF16)], axis=2)
        o_c, maskq = _nsa_cmp(rel_bias, r3(qn), kcw)
        o_s = _nsa_sel(rel_bias, r3(qn), maskq, brow, ksa, vst)
        padw = ((0, 0), (0, 0), (WINDOW, 0), (0, 0))
        o_w = _nsa_win(rel_bias, r3(qn), jnp.pad(kwa, padw), jnp.pad(vwd, padw))
        f2 = lambda a: a.reshape(B * S, a.shape[-1])
        xp = _merge(xp, f2(o_fox), f2(o_c), f2(o_s), f2(o_w), misc, go, wo)
        xp, gtail = _ffn(xp, g2, wgu, conv_w[l], cb, wd, gf, seq_len=S, final=final)
        tiles_per_seq = gtail.shape[0] // B
        conv_p = gtail.reshape(B, tiles_per_seq, 8, dff)[:, -1, 8 - (CONV_W - 1):, :]
        outs[0].append(fkv.reshape(B, S, 2, H_FOX, HD))
        outs[2].append(misc[:, :H_FOX].reshape(B, S, H_FOX))
        outs[4].append(nkv.reshape(B, S, 4, NSA_KV, HD))
        outs[6].append(wkv.reshape(B, S, 2, NSA_KV, HD)[:, -min(WINDOW, S):])
        outs[8].append(conv_p)

        qf, fkv, qn, nkv, wkv, misc = _proj(xs, g1, w_l, bf)
        s3 = lambda a: a.reshape(nb, tq, a.shape[-1])
        lfn = jnp.swapaxes(s3(misc)[:, :, :H_FOX], 1, 2)
        lfn = jnp.pad(lfn, ((0, 0), (0, 0), (0, PAGE - tq)))
        o_fox = _fox_sample(s3(qf), s3(fkv), lfn, cache_kt, cache_lft, page_table, l)
        kcw = _compress_sample(cache_nsa, page_table, l, cw)
        o_c, o_w, maskb = _ns_attend(s3(qn), kcw, win_t, s3(wkv), rbt, l, past)
        o_s = _ns_select(s3(qn), maskb, s3(nkv), rbt, cache_nsa, page_table, l, past)
        s2 = lambda a: a.reshape(nb * tq, a.shape[-1])
        xs = _merge(xs, s2(o_fox), s2(o_c), s2(o_s), s2(o_w), misc, go, wo)
        hist = state_conv[l]
        zero = jnp.zeros((nb, 1, dff), F32)
        hm1 = jnp.concatenate([hist[:, 1:2], zero, zero, zero][:tq], axis=1).reshape(nb * tq, dff)
        hm2 = jnp.concatenate([hist[:, 0:1], hist[:, 1:2], zero, zero][:tq], axis=1).reshape(nb * tq, dff)
        xs, gfull = _ffn(xs, g2, wgu, conv_w[l], cb, wd, gf, seq_len=None, final=final, hist=(hm1, hm2),
                         short_len=tq)
        win_new = jnp.concatenate([state_win_kv[l], wkv.reshape(nb, tq, 2, NSA_KV, HD)], axis=1)[:, -wb:]
        outs[1].append(fkv.reshape(nb, tq, 2, H_FOX, HD))
        outs[3].append(misc[:, :H_FOX].reshape(nb, tq, H_FOX))
        outs[5].append(nkv.reshape(nb, tq, 4, NSA_KV, HD))
        outs[7].append(win_new)
        outs[9].append(gfull.reshape(nb, tq, dff)[:, -(CONV_W - 1):])

    st = [jnp.stack(o) for o in outs]
    return (xp.reshape(B, S, D), xs.reshape(nb, tq, D),
            st[0], st[1], st[2], st[3], st[4], st[5], st[6], st[7], st[8], st[9])
```

```python
import functools
import math

import numpy as np
import jax
import jax.numpy as jnp
from jax import lax
from jax.experimental import pallas as pl
from jax.experimental.pallas import tpu as pltpu

F32 = jnp.float32
BF16 = jnp.bfloat16
I32 = jnp.int32

HD = 64
H_FOX = 8
H_NSA = 8
NSA_KV = 2
NSA_GH = 4
FOX_W = H_FOX * HD
NSA_W = H_NSA * HD
CMP_LEN = 32
CMP_STRIDE = 16
CMP_HID = 256
SEL_LEN = 64
N_SELECT = 16
WINDOW = 512
N_BUCKETS = 32
MAX_DISTANCE = 128
CONV_W = 3
PAGE = 128
EPS = 1e-6
NEG = -1e30
TINY = 1e-30
FORCE_SCORE = 1e4
SCALE = HD ** -0.5
LOG2E = math.log2(math.e)
ONES_ROWS = 8
LANES = 128
VMEM_LIMIT = 56 * 1024 * 1024

C_QF, C_FKV, C_QN, C_NKV, C_WKV, C_MISC, C_END = 0, 512, 1536, 2048, 2560, 2816, 2944
MISC_LOGF = 0
MISC_GATE = 8

AUG0 = HD
AUG1 = HD + 3

PAGES_PER_STEP = 32


def _bucket_thresholds():
    exact = N_BUCKETS // 2
    n = np.arange(1, 4 * MAX_DISTANCE, dtype=np.float64)
    far = exact + (np.log(n / exact) / math.log(MAX_DISTANCE / exact) * (N_BUCKETS - exact)).astype(np.int64)
    b = np.where(n < exact, n, np.minimum(far, N_BUCKETS - 1)).astype(np.int64)
    return [int(n[b >= k].min()) for k in range(exact + 1, N_BUCKETS)]


_THR = _bucket_thresholds()


def _cparams(sem):
    return pltpu.CompilerParams(dimension_semantics=sem, vmem_limit_bytes=VMEM_LIMIT)


def _dot(a, b):
    return jnp.dot(a, b, preferred_element_type=F32)


def _dot_nt(a, b):
    return lax.dot_general(a, b, (((1,), (1,)), ((), ())), preferred_element_type=F32)


def _split2(x):
    hi = x.astype(BF16)
    lo = (x - hi.astype(F32)).astype(BF16)
    return hi, lo


def _split3(x):
    hi = x.astype(BF16)
    r = x - hi.astype(F32)
    mid = r.astype(BF16)
    lo = (r - mid.astype(F32)).astype(BF16)
    return hi, mid, lo


def _dot3_l(x, m):
    hi, mid, lo = _split3(x)
    return _dot(hi, m) + _dot(mid, m) + _dot(lo, m)


def _dot3_r(m, x):
    hi, mid, lo = _split3(x)
    return _dot(m, hi) + _dot(m, mid) + _dot(m, lo)


def _dot2_l(x, m):
    hi, lo = _split2(x)
    return _dot(hi, m) + _dot(lo, m)


def _iota(shape, dim):
    return lax.broadcasted_iota(I32, shape, dim)


def _div(x, k):
    return lax.shift_right_arithmetic(x, jnp.int32(k.bit_length() - 1))


def _mod(x, k):
    return x & (k - 1)


def _rms(x, g):
    r = lax.rsqrt(jnp.mean(x * x, axis=-1, keepdims=True) + EPS)
    return (x * r) * g


def _bucket(d):
    far = jnp.full(d.shape, N_BUCKETS // 2, I32)
    for thr in _THR:
        far = far + (d >= thr).astype(I32)
    return jnp.where(d < N_BUCKETS // 2, d, far)


def _bias_lookup(bucket, rb_get):
    acc = jnp.zeros(bucket.shape, F32)
    for b in range(N_BUCKETS):
        acc = jnp.where(bucket == b, rb_get(b), acc)
    return acc


def _topk_mask(score, k, axis=1):
    idx = _iota(score.shape, axis).astype(F32)

    def body(_, c):
        work, sel = c
        mx = jnp.max(work, axis=axis, keepdims=True)
        first = jnp.min(jnp.where(work == mx, idx, 1e9), axis=axis, keepdims=True)
        hit = idx == first
        return jnp.where(hit, -3.0, work), jnp.where(hit, 1.0, sel)

    _, sel = lax.fori_loop(0, k, body, (score, jnp.zeros(score.shape, F32)))
    return sel


def _proj_kernel(x_ref, g_ref, w_ref, bf_ref, qf_ref, fkv_ref, qn_ref, nkv_ref, wkv_ref, misc_ref):
    h = _rms(x_ref[...], g_ref[...]).astype(BF16)
    qf_ref[...] = _dot(h, w_ref[:, C_QF:C_FKV])
    fkv_ref[...] = _dot(h, w_ref[:, C_FKV:C_QN])
    qn_ref[...] = _dot(h, w_ref[:, C_QN:C_NKV])
    nkv_ref[...] = _dot(h, w_ref[:, C_NKV:C_WKV])
    wkv_ref[...] = _dot(h, w_ref[:, C_WKV:C_MISC])
    z = _dot(h, w_ref[:, C_MISC:C_END]) + bf_ref[...]
    lane = _iota(z.shape, 1)
    logsig = jnp.minimum(z, 0.0) - jnp.log(1.0 + jnp.exp(-jnp.abs(z)))
    sig = 1.0 / (1.0 + jnp.exp(-z))
    misc_ref[...] = jnp.where(lane < MISC_GATE, logsig, jnp.where(lane < MISC_GATE + 3 * H_NSA, sig, 0.0))


def _proj(x2, g, w, bf):
    T, D = x2.shape
    tm = min(512, T)
    widths = (C_FKV - C_QF, C_QN - C_FKV, C_NKV - C_QN, C_WKV - C_NKV, C_MISC - C_WKV, C_END - C_MISC)
    return pl.pallas_call(
        _proj_kernel,
        grid=(T // tm,),
        in_specs=[pl.BlockSpec((tm, D), lambda i: (i, 0)),
                  pl.BlockSpec((1, D), lambda i: (0, 0)),
                  pl.BlockSpec((D, C_END), lambda i: (0, 0)),
                  pl.BlockSpec((1, LANES), lambda i: (0, 0))],
        out_specs=[pl.BlockSpec((tm, wd), lambda i: (i, 0)) for wd in widths],
        out_shape=[jax.ShapeDtypeStruct((T, wd), F32) for wd in widths],
        compiler_params=_cparams(("arbitrary",)),
        name="proj",
    )(x2, g, w, bf)


def _fox_prep_kernel(qf_ref, k_ref, v_ref, misc_ref, qa_ref, ka_ref, vb_ref, carry_ref):
    i = pl.program_id(1)
    tm = qf_ref.shape[1]

    @pl.when(i == 0)
    def _():
        carry_ref[...] = jnp.zeros_like(carry_ref)

    lane = _iota((tm, LANES), 1)
    lf = jnp.where(lane < H_FOX, misc_ref[0], 0.0)
    tril = (_iota((tm, tm), 0) >= _iota((tm, tm), 1)).astype(BF16)
    cum = _dot3_r(tril, lf) + carry_ref[0:1, :]
    carry_ref[...] = jnp.broadcast_to(cum[tm - 1:tm, :], carry_ref.shape)
    cl2 = cum * LOG2E
    hi = cl2.astype(BF16).astype(F32)
    r = cl2 - hi
    mid = r.astype(BF16).astype(F32)
    lo = (r - mid).astype(BF16).astype(F32)
    cc = (hi + pltpu.roll(mid, H_FOX, 1) + pltpu.roll(lo, 2 * H_FOX, 1)).astype(BF16)
    er = _iota((LANES, H_FOX * LANES), 0)
    ec = _iota((LANES, H_FOX * LANES), 1)
    part = _div(er, H_FOX)
    head = _mod(er, H_FOX)
    inb = er < 3 * H_FOX
    eq = jnp.where(inb & (ec == head * LANES + AUG0 + part), 1.0, 0.0).astype(BF16)
    ek = jnp.where(inb & (ec == head * LANES + AUG1 + part), -1.0, 0.0).astype(BF16)
    cl = _mod(_iota((1, H_FOX * LANES), 1), LANES)
    ones_q = jnp.where((cl >= AUG1) & (cl < AUG1 + 3), 1.0, 0.0)
    ones_k = jnp.where((cl >= AUG0) & (cl < AUG0 + 3), 1.0, 0.0)
    cols_q = _dot(cc, eq) + ones_q
    cols_k = _dot(cc, ek) + ones_k
    q = qf_ref[0]
    k = k_ref[0]
    for h in range(H_FOX):
        a = (h // 2) * LANES
        qt = q[:, a:a + LANES]
        kt = k[:, a:a + LANES]
        if h % 2:
            qt = pltpu.roll(qt, HD, 1)
            kt = pltpu.roll(kt, HD, 1)
        qa_ref[0, h] = jnp.where(lane < HD, qt * (SCALE * LOG2E), cols_q[:, h * LANES:(h + 1) * LANES]).astype(BF16)
        ka_ref[0, h] = jnp.where(lane < HD, kt, cols_k[:, h * LANES:(h + 1) * LANES]).astype(BF16)
    vb_ref[0] = v_ref[0].astype(BF16)


def _fox_prep(qf, fkv, misc):
    B, S, _ = qf.shape
    tm = min(512, S)
    return pl.pallas_call(
        _fox_prep_kernel,
        grid=(B, S // tm),
        in_specs=[pl.BlockSpec((1, tm, FOX_W), lambda b, i: (b, i, 0)),
                  pl.BlockSpec((1, tm, FOX_W), lambda b, i: (b, i, 0)),
                  pl.BlockSpec((1, tm, FOX_W), lambda b, i: (b, i, 1)),
                  pl.BlockSpec((1, tm, LANES), lambda b, i: (b, i, 0))],
        out_specs=[pl.BlockSpec((1, H_FOX, tm, LANES), lambda b, i: (b, 0, i, 0)),
                   pl.BlockSpec((1, H_FOX, tm, LANES), lambda b, i: (b, 0, i, 0)),
                   pl.BlockSpec((1, tm, FOX_W), lambda b, i: (b, i, 0))],
        out_shape=[jax.ShapeDtypeStruct((B, H_FOX, S, LANES), BF16),
                   jax.ShapeDtypeStruct((B, H_FOX, S, LANES), BF16),
                   jax.ShapeDtypeStruct((B, S, FOX_W), BF16)],
        scratch_shapes=[pltpu.VMEM((8, LANES), F32)],
        compiler_params=_cparams(("arbitrary", "arbitrary")),
        name="fox_prep",
    )(qf, fkv, fkv, misc)


def _softmax_update(s, v, m_ref, l_ref, acc_ref):
    m_old = m_ref[...]
    m_new = jnp.maximum(m_old, jnp.max(s, axis=1, keepdims=True))
    alpha = jnp.exp(m_old - m_new)
    p = jnp.exp(s - m_new)
    l_ref[...] = alpha * l_ref[...] + jnp.sum(p, axis=1, keepdims=True)
    acc_ref[...] = alpha * acc_ref[...] + _dot(p.astype(BF16), v)
    m_ref[...] = m_new


def _softmax_init(m_ref, l_ref, acc_ref):
    m_ref[...] = jnp.full(m_ref.shape, NEG, F32)
    l_ref[...] = jnp.zeros(l_ref.shape, F32)
    acc_ref[...] = jnp.zeros(acc_ref.shape, F32)


def _softmax_update_t(s, vt, m_ref, acc_ref):
    m_old = m_ref[...]
    m_new = jnp.maximum(m_old, jnp.max(s, axis=0, keepdims=True))
    alpha = jnp.exp2(m_old - m_new)
    p = jnp.exp2(s - m_new).astype(BF16)
    acc_ref[...] = alpha * acc_ref[...] + _dot(vt, p)
    m_ref[...] = m_new


def _softmax_init_t(m_ref, acc_ref):
    m_ref[...] = jnp.full(m_ref.shape, NEG, F32)
    acc_ref[...] = jnp.zeros(acc_ref.shape, F32)


def _fox_flash_kernel(qa_ref, ka_ref, vt_ref, o_ref, m0_ref, m1_ref, acc0_ref, acc1_ref):
    qi = pl.program_id(2)
    tq = qa_ref.shape[2]
    m_ref = (m0_ref, m1_ref)
    acc_ref = (acc0_ref, acc1_ref)
    for hh in range(2):
        _softmax_init_t(m_ref[hh], acc_ref[hh])

    def tile(kt, diag):
        ks = pl.multiple_of(kt * tq, tq)
        for hh in range(2):
            s = _dot_nt(ka_ref[0, hh, pl.ds(ks, tq), :], qa_ref[0, hh])
            if diag:
                s = jnp.where(_iota(s.shape, 0) <= _iota(s.shape, 1), s, NEG)
            _softmax_update_t(s, vt_ref[0, hh, :, pl.ds(ks, tq)], m_ref[hh], acc_ref[hh])

    def full_tile(kt, c):
        tile(kt, False)
        return c

    lax.fori_loop(0, qi, full_tile, 0)
    tile(qi, True)
    halves = []
    for hh in range(2):
        a = acc_ref[hh][...]
        halves.append(a[0:HD, :] / jnp.maximum(a[HD:HD + 1, :], TINY))
    o_ref[0] = jnp.concatenate(halves, axis=0).T


def _fox_flash(qa, ka, vt):
    B, H, S, _ = qa.shape
    tq = min(512, S)
    rows = vt.shape[2]
    return pl.pallas_call(
        _fox_flash_kernel,
        grid=(B, H // 2, S // tq),
        in_specs=[pl.BlockSpec((1, 2, tq, LANES), lambda b, p, i: (b, p, i, 0)),
                  pl.BlockSpec((1, 2, S, LANES), lambda b, p, i: (b, p, 0, 0)),
                  pl.BlockSpec((1, 2, rows, S), lambda b, p, i: (b, p, 0, 0))],
        out_specs=pl.BlockSpec((1, tq, LANES), lambda b, p, i: (b, i, p)),
        out_shape=jax.ShapeDtypeStruct((B, S, FOX_W), F32),
        scratch_shapes=[pltpu.VMEM((1, tq), F32), pltpu.VMEM((1, tq), F32),
                        pltpu.VMEM((rows, tq), F32), pltpu.VMEM((rows, tq), F32)],
        compiler_params=_cparams(("arbitrary", "arbitrary", "arbitrary")),
        name="fox_flash",
    )(qa, ka, vt)


def _gelu_tanh(x):
    return 0.5 * x * (1.0 + jnp.tanh(math.sqrt(2.0 / math.pi) * (x + 0.044715 * (x * x * x))))


def _compress_core(x, pt, pb, w1t, w1b, b1, w2w):
    n = x.shape[0]
    a = _dot((x + pt).astype(BF16), w1t)
    b = _dot((x + pb).astype(BF16), w1b)
    h = a + pltpu.roll(b, n - 1, 0) + b1
    return _dot(_gelu_tanh(h).astype(BF16), w2w)


CMP_PAD_FRONT = 16
CMP_PAD_BACK = 112


def _store_cmp(out_ref, idx, res, n):
    out_ref[idx] = jnp.zeros(out_ref.shape[len(idx):], BF16)
    out_ref[idx + (slice(CMP_PAD_FRONT, CMP_PAD_FRONT + n), slice(None))] = res.astype(BF16)


def _compress_prompt_kernel(x_ref, pt_ref, pb_ref, w1t_ref, w1b_ref, b1_ref, w2w_ref, out_ref):
    n = x_ref.shape[1] // CMP_STRIDE
    x = jnp.concatenate([x_ref[0, pl.ds(j, n, stride=CMP_STRIDE), :] for j in range(CMP_STRIDE)], axis=1)
    res = _compress_core(x, pt_ref[0], pb_ref[0], w1t_ref[0], w1b_ref[0], b1_ref[0], w2w_ref[0])
    _store_cmp(out_ref, (0, 0), res, n)


def _compress_prompt(nkv, cw):
    B, S, _ = nkv.shape
    n = S // CMP_STRIDE
    npad = n + CMP_PAD_FRONT + CMP_PAD_BACK
    kx = CMP_STRIDE * LANES
    return pl.pallas_call(
        _compress_prompt_kernel,
        grid=(B, 2),
        in_specs=[pl.BlockSpec((1, S, LANES), lambda b, w: (b, 0, w)),
                  pl.BlockSpec((1, 1, kx), lambda b, w: (w, 0, 0)),
                  pl.BlockSpec((1, 1, kx), lambda b, w: (w, 0, 0)),
                  pl.BlockSpec((1, kx, 2 * CMP_HID), lambda b, w: (w, 0, 0)),
                  pl.BlockSpec((1, kx, 2 * CMP_HID), lambda b, w: (w, 0, 0)),
                  pl.BlockSpec((1, 1, 2 * CMP_HID), lambda b, w: (w, 0, 0)),
                  pl.BlockSpec((1, 2 * CMP_HID, NSA_W), lambda b, w: (w, 0, 0))],
        out_specs=pl.BlockSpec((1, 1, npad, NSA_W), lambda b, w: (b, w, 0, 0)),
        out_shape=jax.ShapeDtypeStruct((B, 2, npad, NSA_W), BF16),
        compiler_params=_cparams(("arbitrary", "arbitrary")),
        name="compress_prompt",
    )(nkv, cw["pt"], cw["pb"], cw["w1t"], cw["w1b"], cw["b1"], cw["w2w"])


def _compress_sample_kernel(pt_tab, *refs):
    P = PAGES_PER_STEP
    pages = (refs[:P], refs[P:2 * P])
    pt_ref, pb_ref, w1t_ref, w1b_ref, b1_ref, w2w_ref, out_ref, x_sc, rows_sc = refs[2 * P:]
    s = pl.program_id(1)
    n = x_sc.shape[1]
    rows = P * (PAGE // CMP_STRIDE)
    r0 = pl.multiple_of(s * rows, rows)
    for w in range(2):
        for k, pg in enumerate(pages[w]):
            rows_sc[k * PAGE:(k + 1) * PAGE, :] = pg[...].reshape(LANES, PAGE).T
        for j in range(CMP_STRIDE):
            x_sc[w, pl.ds(r0, rows), j * LANES:(j + 1) * LANES] = rows_sc[pl.ds(j, rows, stride=CMP_STRIDE), :]

    @pl.when(s == pl.num_programs(1) - 1)
    def _():
        for w in range(2):
            res = _compress_core(x_sc[w], pt_ref[w], pb_ref[w], w1t_ref[w], w1b_ref[w], b1_ref[w], w2w_ref[w])
            _store_cmp(out_ref, (0, w), res, n)


def _compress_sample(cache_nsa, page_table, layer, cw):
    nb, npages = page_table.shape
    P = PAGES_PER_STEP
    n = npages * (PAGE // CMP_STRIDE)
    npad = n + CMP_PAD_FRONT + CMP_PAD_BACK
    kx = CMP_STRIDE * LANES

    def page_spec(k, w):
        return pl.BlockSpec((None, None, None, NSA_KV, HD, PAGE),
                            lambda b, s, pt: (layer, pt[b, s * P + k], w, 0, 0, 0))

    const3 = lambda b, s, pt: (0, 0, 0)
    return pl.pallas_call(
        _compress_sample_kernel,
        grid_spec=pltpu.PrefetchScalarGridSpec(
            num_scalar_prefetch=1,
            grid=(nb, npages // P),
            in_specs=[page_spec(k, 0) for k in range(P)] + [page_spec(k, 1) for k in range(P)] + [
                pl.BlockSpec((2, 1, kx), const3), pl.BlockSpec((2, 1, kx), const3),
                pl.BlockSpec((2, kx, 2 * CMP_HID), const3), pl.BlockSpec((2, kx, 2 * CMP_HID), const3),
                pl.BlockSpec((2, 1, 2 * CMP_HID), const3), pl.BlockSpec((2, 2 * CMP_HID, NSA_W), const3)],
            out_specs=pl.BlockSpec((1, 2, npad, NSA_W), lambda b, s, pt: (b, 0, 0, 0)),
            scratch_shapes=[pltpu.VMEM((2, n, kx), F32), pltpu.VMEM((P * PAGE, LANES), F32)]),
        out_shape=jax.ShapeDtypeStruct((nb, 2, npad, NSA_W), BF16),
        compiler_params=_cparams(("arbitrary", "arbitrary")),
        name="compress_sample",
    )(page_table, *([cache_nsa] * (2 * P)), cw["pt"], cw["pb"], cw["w1t"], cw["w1b"], cw["b1"], cw["w2w"])


def _nsa_prep_kernel(sel_ref, win_ref, ksa_ref, kwa_ref):
    i = pl.program_id(1)
    tm = sel_ref.shape[1]
    lane = _iota((tm, LANES), 1)
    pos = i * tm + _iota((tm, LANES), 0)
    onehot = jnp.where(_div(pos, SEL_LEN) == lane, 1.0, 0.0)
    ks = sel_ref[0][:, 0:LANES]
    kw = win_ref[0][:, 0:LANES]
    rks = pltpu.roll(ks, HD, 1)
    rkw = pltpu.roll(kw, HD, 1)
    for g in range(NSA_KV):
        left = jnp.where(lane < HD, ks if g == 0 else rks, jnp.where(lane < HD + 3, 1.0, 0.0))
        ksa_ref[0, g] = jnp.concatenate([left, onehot], axis=1).astype(BF16)
        kwa_ref[0, g] = jnp.where(lane < HD, kw if g == 0 else rkw, 0.0).astype(BF16)


def _nsa_prep(nkv, wkv):
    B, S, _ = nkv.shape
    tm = min(512, S)
    spec128 = pl.BlockSpec((1, NSA_KV, tm, LANES), lambda b, i: (b, 0, i, 0))
    return pl.pallas_call(
        _nsa_prep_kernel,
        grid=(B, S // tm),
        in_specs=[pl.BlockSpec((1, tm, 2 * LANES), lambda b, i: (b, i, 1)),
                  pl.BlockSpec((1, tm, 2 * LANES), lambda b, i: (b, i, 0))],
        out_specs=[pl.BlockSpec((1, NSA_KV, tm, 2 * LANES), lambda b, i: (b, 0, i, 0)), spec128],
        out_shape=[jax.ShapeDtypeStruct((B, NSA_KV, S, 2 * LANES), BF16),
                   jax.ShapeDtypeStruct((B, NSA_KV, S, LANES), BF16)],
        compiler_params=_cparams(("arbitrary", "arbitrary")),
        name="nsa_prep",
    )(nkv, wkv)


def _overlap(i_blk, j_blk):
    start = i_blk * CMP_STRIDE
    return (start < j_blk * SEL_LEN + SEL_LEN) & (start + CMP_LEN > j_blk * SEL_LEN)


NEAR_BACK = 16


def _nsa_cmp_kernel(rb_ref, q_ref, kcw_ref, oc_ref, mq_ref, fc_ref, qs_ref):
    b = pl.program_id(0)
    i = pl.program_id(1)
    tq = q_ref.shape[1]
    n = kcw_ref.shape[2] - CMP_PAD_FRONT - CMP_PAD_BACK
    n_sel = n * CMP_STRIDE // SEL_LEN
    k_eff = min(N_SELECT, n_sel)

    @pl.when((b == 0) & (i == 0))
    def _():
        dist = _iota((tq, LANES), 0) + (NEAR_BACK * CMP_STRIDE - (CMP_LEN - 1)) - CMP_STRIDE * _iota((tq, LANES), 1)
        bk = _bucket(jnp.clip(dist, 0, MAX_DISTANCE - 1))
        for h in range(H_NSA):
            fc_ref[h // NSA_GH, (h % NSA_GH) * tq:(h % NSA_GH + 1) * tq, :] = jnp.where(
                dist >= 0, _bias_lookup(bk, lambda bb, h=h: rb_ref[bb, h]), NEG)

    qs = i * tq
    i0 = qs // CMP_STRIDE - NEAR_BACK
    st = pl.multiple_of(qs // CMP_STRIDE, 16)
    mrows = NSA_GH * tq
    farmask = _iota((mrows, n), 1) < i0
    nearmask = (_iota((mrows, LANES), 1) + i0) >= 0
    rowhead = _div(_iota((mrows, 1), 0), tq)
    ps_far = []
    ps_near = []
    outs = []
    for g in range(NSA_KV):
        _stack_heads(q_ref[0][:, g * 2 * LANES:(g + 1) * 2 * LANES], jnp.zeros((NSA_GH, LANES), F32), None, qs_ref)
        qq = qs_ref[...]
        gl = slice(g * NSA_GH * HD, g * NSA_GH * HD + LANES)
        kfar = kcw_ref[0, 0, CMP_PAD_FRONT:CMP_PAD_FRONT + n, gl]
        vfar = kcw_ref[0, 1, CMP_PAD_FRONT:CMP_PAD_FRONT + n, gl]
        knear = kcw_ref[0, 0, pl.ds(st, LANES), gl]
        vnear = kcw_ref[0, 1, pl.ds(st, LANES), gl]
        bfar = jnp.zeros((mrows, 1), F32)
        for hh in range(NSA_GH):
            bfar = jnp.where(rowhead == hh, rb_ref[N_BUCKETS - 1, g * NSA_GH + hh], bfar)
        sf = jnp.where(farmask, _dot_nt(qq, kfar) + bfar, NEG)
        sn = jnp.where(nearmask, _dot_nt(qq, knear) + fc_ref[g], NEG)
        m = jnp.maximum(jnp.max(sf, axis=1, keepdims=True), jnp.max(sn, axis=1, keepdims=True))
        pf = jnp.where(sf > 0.5 * NEG, jnp.exp(sf - m), 0.0)
        pn = jnp.where(sn > 0.5 * NEG, jnp.exp(sn - m), 0.0)
        l = jnp.sum(pf, axis=1, keepdims=True) + jnp.sum(pn, axis=1, keepdims=True)
        inv = 1.0 / jnp.maximum(l, TINY)
        pf = pf * inv
        pn = pn * inv
        outs.append(_unstack_heads(_dot(pf.astype(BF16), vfar) + _dot(pn.astype(BF16), vnear), tq))
        ps_far.append(sum(pf[hh * tq:(hh + 1) * tq] for hh in range(NSA_GH)))
        ps_near.append(sum(pn[hh * tq:(hh + 1) * tq] for hh in range(NSA_GH)))
    oc_ref[0] = jnp.concatenate(outs, axis=1)

    mov_far = jnp.where(_overlap(_iota((n, LANES), 0), _iota((n, LANES), 1)), 1.0, 0.0).astype(BF16)
    mov_near = jnp.where(_overlap(_iota((LANES, LANES), 0) + i0, _iota((LANES, LANES), 1)), 1.0, 0.0).astype(BF16)
    qpos = qs + _iota((tq, LANES), 0)
    jb = _iota((tq, LANES), 1)
    qblk = _div(qpos, SEL_LEN)
    valid = jb * SEL_LEN <= qpos
    forced = (jb == 0) | (jb == qblk) | (jb == qblk - 1)
    scores = []
    for g in range(NSA_KV):
        p_slc = _dot2_l(ps_far[g], mov_far) + _dot2_l(ps_near[g], mov_near)
        scores.append(jnp.where(forced, FORCE_SCORE, jnp.where(valid, p_slc, -1.0)))
    st = jnp.concatenate([sc.T for sc in scores], axis=1)
    keep = jnp.where((_topk_mask(st, k_eff, axis=0) > 0.5) & (st >= 0.0), 0.0, NEG)
    for g in range(NSA_KV):
        mq_ref[0, g] = keep[:, g * tq:(g + 1) * tq].T.astype(BF16)


def _nsa_cmp(rb, qn, kcw):
    B, S, _ = qn.shape
    tq = min(256, S)
    npad = kcw.shape[2]
    return pl.pallas_call(
        _nsa_cmp_kernel,
        grid=(B, S // tq),
        in_specs=[pl.BlockSpec(memory_space=pltpu.SMEM),
                  pl.BlockSpec((1, tq, NSA_W), lambda b, i: (b, i, 0)),
                  pl.BlockSpec((1, 2, npad, NSA_W), lambda b, i: (b, 0, 0, 0))],
        out_specs=[pl.BlockSpec((1, tq, NSA_W), lambda b, i: (b, i, 0)),
                   pl.BlockSpec((1, NSA_KV, tq, LANES), lambda b, i: (b, 0, i, 0))],
        out_shape=[jax.ShapeDtypeStruct((B, S, NSA_W), F32),
                   jax.ShapeDtypeStruct((B, NSA_KV, S, LANES), BF16)],
        scratch_shapes=[pltpu.VMEM((NSA_KV, NSA_GH * tq, LANES), F32), pltpu.VMEM((NSA_GH * tq, LANES), BF16)],
        compiler_params=_cparams(("arbitrary", "arbitrary")),
        name="nsa_cmp",
    )(rb, qn, kcw)


def _stack_heads(q, g_rows, extra, qs_ref, scale=SCALE):
    tq = q.shape[0]
    lane = _iota((tq, LANES), 1)
    for hh in range(NSA_GH):
        a = (hh // 2) * LANES
        t = q[:, a:a + LANES]
        if hh % 2:
            t = pltpu.roll(t, HD, 1)
        left = jnp.where(lane < HD, t * scale, g_rows[hh:hh + 1, :]).astype(BF16)
        if extra is None:
            qs_ref[hh * tq:(hh + 1) * tq, :] = left
        else:
            qs_ref[hh * tq:(hh + 1) * tq, :] = jnp.concatenate([left, extra], axis=1)


def _unstack_heads(a, tq):
    lane = _iota((tq, LANES), 1)
    p0 = jnp.where(lane < HD, a[0:tq], a[tq:2 * tq])
    p1 = jnp.where(lane < HD, a[2 * tq:3 * tq], a[3 * tq:4 * tq])
    return jnp.concatenate([p0, p1], axis=1)


SEL_BACK = 128


def _nsa_sel_kernel(rb_ref, q_ref, mq_ref, brow_ref, ksa_ref, vt_ref, o_ref, dn_ref, qs_ref, m_ref, acc_ref):
    b = pl.program_id(0)
    g = pl.program_id(1)
    i = pl.program_id(2)
    tq = q_ref.shape[1]
    wn = tq + SEL_BACK

    @pl.when((b == 0) & (g == 0) & (i == 0))
    def _():
        dist = _iota((wn, tq), 1) + SEL_BACK - _iota((wn, tq), 0)
        bk = _bucket(jnp.clip(dist, 0, MAX_DISTANCE - 1))
        for h in range(H_NSA):
            far = rb_ref[N_BUCKETS - 1, h]
            val = (_bias_lookup(bk, lambda bb, h=h: rb_ref[bb, h]) - far) * LOG2E
            dn_ref[h // NSA_GH, :, (h % NSA_GH) * tq:(h % NSA_GH + 1) * tq] = jnp.where(dist >= 0, val, NEG)

    qs = i * tq
    _stack_heads(q_ref[0], brow_ref[0], mq_ref[0, 0], qs_ref, scale=SCALE * LOG2E)
    _softmax_init_t(m_ref, acc_ref)
    hw = (NSA_GH // 2) * tq

    def tile(start, size, bias):
        k = ksa_ref[0, 0, pl.ds(start, size), :]
        vt = vt_ref[0, 0, :, pl.ds(start, size)]
        for half in range(2):
            s = _dot_nt(k, qs_ref[half * hw:(half + 1) * hw, :])
            if bias is not None:
                s = s + bias(half)
            _softmax_update_t(s, vt, m_ref.at[half], acc_ref.at[half])

    @pl.when(i == 0)
    def _():
        tile(0, tq, lambda half: dn_ref[g, SEL_BACK:, half * hw:(half + 1) * hw])

    @pl.when(i > 0)
    def _():
        tile(pl.multiple_of(qs - SEL_BACK, LANES), wn, lambda half: dn_ref[g, :, half * hw:(half + 1) * hw])
        tile(pl.multiple_of(qs - tq, LANES), tq - SEL_BACK, None)

    def far_tile(kt, c):
        tile(pl.multiple_of(kt * tq, tq), tq, None)
        return c

    lax.fori_loop(0, jnp.maximum(i - 1, 0), far_tile, 0)
    parts = []
    for half in range(2):
        acc = acc_ref[half]
        a = acc[0:HD, :] / jnp.maximum(acc[HD:HD + 1, :], TINY)
        parts += [a[:, 0:tq], a[:, tq:2 * tq]]
    o_ref[0] = jnp.concatenate(parts, axis=0).T


def _nsa_sel(rb, qn, maskq, brow, ksa, vst):
    B, S, _ = qn.shape
    tq = min(512, S)
    G = NSA_KV
    return pl.pallas_call(
        _nsa_sel_kernel,
        grid=(B, G, S // tq),
        in_specs=[pl.BlockSpec(memory_space=pltpu.SMEM),
                  pl.BlockSpec((1, tq, 2 * LANES), lambda b, g, i: (b, i, g)),
                  pl.BlockSpec((1, 1, tq, LANES), lambda b, g, i: (b, g, i, 0)),
                  pl.BlockSpec((1, NSA_GH, LANES), lambda b, g, i: (g, 0, 0)),
                  pl.BlockSpec((1, 1, S, 2 * LANES), lambda b, g, i: (b, g, 0, 0)),
                  pl.BlockSpec((1, 1, HD + ONES_ROWS, S), lambda b, g, i: (b, g, 0, 0))],
        out_specs=pl.BlockSpec((1, tq, 2 * LANES), lambda b, g, i: (b, i, g)),
        out_shape=jax.ShapeDtypeStruct((B, S, NSA_W), F32),
        scratch_shapes=[pltpu.VMEM((G, tq + SEL_BACK, NSA_GH * tq), F32),
                        pltpu.VMEM((NSA_GH * tq, 2 * LANES), BF16),
                        pltpu.VMEM((2, 1, NSA_GH // 2 * tq), F32),
                        pltpu.VMEM((2, HD + ONES_ROWS, NSA_GH // 2 * tq), F32)],
        compiler_params=_cparams(("arbitrary", "arbitrary", "arbitrary")),
        name="nsa_sel",
    )(rb, qn, maskq, brow, ksa, vst)


def _nsa_win_kernel(rb_ref, q_ref, kwa_ref, vt_ref, o_ref, dw_ref, qs_ref):
    b = pl.program_id(0)
    g = pl.program_id(1)
    i = pl.program_id(2)
    tq = q_ref.shape[1]
    wk = tq + WINDOW

    @pl.when((b == 0) & (g == 0) & (i == 0))
    def _():
        dist = _iota((wk, tq), 1) + WINDOW - _iota((wk, tq), 0)
        bk = _bucket(jnp.clip(dist, 0, MAX_DISTANCE - 1))
        ok = (dist >= 0) & (dist < WINDOW)
        for h in range(H_NSA):
            val = _bias_lookup(bk, lambda bb, h=h: rb_ref[bb, h]) * LOG2E
            dw_ref[h // NSA_GH, :, (h % NSA_GH) * tq:(h % NSA_GH + 1) * tq] = jnp.where(ok, val, NEG)

    qs = pl.multiple_of(i * tq, tq)
    _stack_heads(q_ref[0], jnp.zeros((NSA_GH, LANES), F32), None, qs_ref, scale=SCALE * LOG2E)
    s = _dot_nt(kwa_ref[0, 0, pl.ds(qs, wk), :], qs_ref[...]) + dw_ref[g]
    s = jnp.where(_iota(s.shape, 0) + qs >= WINDOW, s, NEG)
    m = jnp.max(s, axis=0, keepdims=True)
    p = jnp.exp2(s - m).astype(BF16)
    acc = _dot(vt_ref[0, 0, :, pl.ds(qs, wk)], p)
    a = acc[0:HD, :] / jnp.maximum(acc[HD:HD + 1, :], TINY)
    o_ref[0] = jnp.concatenate([a[:, hh * tq:(hh + 1) * tq] for hh in range(NSA_GH)], axis=0).T


def _nsa_win(rb, qn, kwa_p, vwt_p):
    B, S, _ = qn.shape
    tq = min(256, S)
    G = NSA_KV
    sp = kwa_p.shape[2]
    return pl.pallas_call(
        _nsa_win_kernel,
        grid=(B, G, S // tq),
        in_specs=[pl.BlockSpec(memory_space=pltpu.SMEM),
                  pl.BlockSpec((1, tq, 2 * LANES), lambda b, g, i: (b, i, g)),
                  pl.BlockSpec((1, 1, sp, LANES), lambda b, g, i: (b, g, 0, 0)),
                  pl.BlockSpec((1, 1, HD + ONES_ROWS, sp), lambda b, g, i: (b, g, 0, 0))],
        out_specs=pl.BlockSpec((1, tq, 2 * LANES), lambda b, g, i: (b, i, g)),
        out_shape=jax.ShapeDtypeStruct((B, S, NSA_W), F32),
        scratch_shapes=[pltpu.VMEM((G, tq + WINDOW, NSA_GH * tq), F32),
                        pltpu.VMEM((NSA_GH * tq, LANES), BF16)],
        compiler_params=_cparams(("arbitrary", "arbitrary", "arbitrary")),
        name="nsa_win",
    )(rb, qn, kwa_p, vwt_p)


def _merge_kernel(x_ref, of_ref, oc_ref, os_ref, ow_ref, misc_ref, g_ref, w_ref, out_ref):
    tm = x_ref.shape[0]
    hi, lo = _split2(misc_ref[...])
    er = _iota((LANES, NSA_W), 0)
    ec = _iota((LANES, NSA_W), 1)
    onsa = jnp.zeros((tm, NSA_W), F32)
    for k, o_ref in enumerate((oc_ref, os_ref, ow_ref)):
        e = jnp.where(er == MISC_GATE + k * H_NSA + _div(ec, HD), 1.0, 0.0).astype(BF16)
        onsa = onsa + (_dot(hi, e) + _dot(lo, e)) * o_ref[...]
    g = g_ref[...]
    a = _rms(of_ref[...], g[:, :FOX_W]).astype(BF16)
    c = _rms(onsa, g[:, FOX_W:]).astype(BF16)
    out_ref[...] = x_ref[...] + _dot(a, w_ref[0:FOX_W, :]) + _dot(c, w_ref[FOX_W:, :])


def _merge(x2, ofox, oc, os_, ow, misc, g, w):
    T, D = x2.shape
    tm = min(512, T)
    row = lambda wd: pl.BlockSpec((tm, wd), lambda i: (i, 0))
    return pl.pallas_call(
        _merge_kernel,
        grid=(T // tm,),
        in_specs=[row(D), row(FOX_W), row(NSA_W), row(NSA_W), row(NSA_W), row(LANES),
                  pl.BlockSpec((1, D), lambda i: (0, 0)),
                  pl.BlockSpec((D, D), lambda i: (0, 0))],
        out_specs=row(D),
        out_shape=jax.ShapeDtypeStruct((T, D), F32),
        compiler_params=_cparams(("arbitrary",)),
        name="merge",
    )(x2, ofox, oc, os_, ow, misc, g, w)


FF_CHUNK = 1408


def _ffn_kernel(*refs, seq_len, short_len, final):
    if seq_len is None:
        (x_ref, g2_ref, wg_ref, wu_ref, cw_ref, cb_ref, wd_ref, gf_ref, hm1_ref, hm2_ref,
         out_ref, gt_ref, h_sc, acc_sc) = refs
    else:
        (x_ref, g2_ref, wg_ref, wu_ref, cw_ref, cb_ref, wd_ref, gf_ref,
         out_ref, gt_ref, h_sc, acc_sc, carry_sc) = refs
    i = pl.program_id(0)
    j = pl.program_id(1)
    tm = x_ref.shape[0]

    @pl.when(j == 0)
    def _():
        h_sc[...] = _rms(x_ref[...], g2_ref[...]).astype(BF16)
        acc_sc[...] = jnp.zeros_like(acc_sc)

    h = h_sc[...]
    gch = _dot(h, wg_ref[...])
    u = _dot(h, wu_ref[...])
    r1 = pltpu.roll(gch, 1, 0)
    r2 = pltpu.roll(gch, 2, 0)
    row = _iota(gch.shape, 0)
    if seq_len is None:
        t = _mod(row, short_len)
        m1 = jnp.where(t == 0, hm1_ref[...], r1)
        m2 = jnp.where(t < 2, hm2_ref[...], r2)
        gt_ref[...] = gch
    else:
        first = (i % (seq_len // tm)) == 0
        c = jnp.where(first, 0.0, carry_sc[j])
        m1 = jnp.where(row == 0, c[1:2, :], r1)
        m2 = jnp.where(row == 0, c[0:1, :], jnp.where(row == 1, c[1:2, :], r2))
        carry_sc[j, 0:2, :] = gch[tm - 2:tm, :]
        gt_ref[0] = gch[tm - 8:tm, :]
    cw = cw_ref[...]
    gc = cb_ref[...] + cw[0:1, :] * m2 + cw[1:2, :] * m1 + cw[2:3, :] * gch
    act = gc * (1.0 / (1.0 + jnp.exp(-gc)))
    acc_sc[...] += _dot((act * u).astype(BF16), wd_ref[...])

    @pl.when(j == pl.num_programs(1) - 1)
    def _():
        y = x_ref[...] + acc_sc[...]
        if final:
            y = _rms(y, gf_ref[...])
        out_ref[...] = y


def _ffn(x2, g2, wgu, cw, cb, wd, gf, *, seq_len, final, hist=None, short_len=None):
    T, D = x2.shape
    dff = wd.shape[0]
    fc = FF_CHUNK if dff % FF_CHUNK == 0 else LANES
    nff = dff // fc
    tm = min(512, T) if seq_len is not None else T
    nt = T // tm
    in_specs = [pl.BlockSpec((tm, D), lambda i, j: (i, 0)),
                pl.BlockSpec((1, D), lambda i, j: (0, 0)),
                pl.BlockSpec((D, fc), lambda i, j: (0, j)),
                pl.BlockSpec((D, fc), lambda i, j: (0, nff + j)),
                pl.BlockSpec((CONV_W, fc), lambda i, j: (0, j)),
                pl.BlockSpec((1, fc), lambda i, j: (0, j)),
                pl.BlockSpec((fc, D), lambda i, j: (j, 0)),
                pl.BlockSpec((1, D), lambda i, j: (0, 0))]
    args = [x2, g2, wgu, wgu, cw, cb, wd, gf]
    scratch = [pltpu.VMEM((tm, D), BF16), pltpu.VMEM((tm, D), F32)]
    if seq_len is None:
        in_specs += [pl.BlockSpec((tm, fc), lambda i, j: (i, j)), pl.BlockSpec((tm, fc), lambda i, j: (i, j))]
        args += list(hist)
        gt_spec = pl.BlockSpec((tm, fc), lambda i, j: (i, j))
        gt_shape = jax.ShapeDtypeStruct((T, dff), F32)
    else:
        scratch.append(pltpu.VMEM((nff, 8, fc), F32))
        gt_spec = pl.BlockSpec((1, 8, fc), lambda i, j: (i, 0, j))
        gt_shape = jax.ShapeDtypeStruct((nt, 8, dff), F32)
    return pl.pallas_call(
        functools.partial(_ffn_kernel, seq_len=seq_len, short_len=short_len, final=final),
        grid=(nt, nff),
        in_specs=in_specs,
        out_specs=[pl.BlockSpec((tm, D), lambda i, j: (i, 0)), gt_spec],
        out_shape=[jax.ShapeDtypeStruct((T, D), F32), gt_shape],
        scratch_shapes=scratch,
        compiler_params=_cparams(("arbitrary", "arbitrary")),
        name="ffn",
    )(*args)


def _rows_th(q):
    tq = q.shape[0]
    rows = jnp.concatenate([jnp.broadcast_to(q[t:t + 1, :], (8, q.shape[1])) for t in range(tq)], axis=0)
    keep = _div(_iota(rows.shape, 1), HD) == _mod(_iota(rows.shape, 0), 8)
    return jnp.where(keep, rows * SCALE, 0.0).astype(BF16)


def _diag_rows(o_ref, o32):
    keep = _div(_iota(o32.shape, 1), HD) == _mod(_iota(o32.shape, 0), 8)
    od = jnp.where(keep, o32, 0.0)
    for t in range(o32.shape[0] // 8):
        o_ref[0, t:t + 1, :] = jnp.sum(od[t * 8:(t + 1) * 8, :], axis=0, keepdims=True)


FOX_PAGES = 16


def _fox_sample_kernel(pt_tab, *refs):
    P = FOX_PAGES
    q_ref, kvn_ref, lfn_ref = refs[0:3]
    k_refs = refs[3:3 + P]
    v_refs = refs[3 + P:3 + 2 * P]
    lf_refs = refs[3 + 2 * P:3 + 3 * P]
    o_ref, q_sc, m_ref, l_ref, acc_ref, carry_ref, new_sc = refs[3 + 3 * P:]
    s = pl.program_id(1)
    tq = q_ref.shape[1]
    nr = tq * 8

    @pl.when(s == 0)
    def _():
        q_sc[...] = _rows_th(q_ref[0])
        _softmax_init(m_ref, l_ref, acc_ref)
        carry_ref[...] = jnp.zeros_like(carry_ref)
        new_sc[...] = jnp.zeros_like(new_sc)
        new_sc[0:tq, :] = kvn_ref[0]

    qq = q_sc[...]
    triu = (_iota((PAGE, PAGE), 0) <= _iota((PAGE, PAGE), 1)).astype(BF16)

    def attend(score_fns, value_fns, lfs, extra_mask):
        off = carry_ref[...]
        sc = []
        for fn, lf in zip(score_fns, lfs):
            cum = _dot3_l(lf, triu) + off
            off = jnp.broadcast_to(cum[:, PAGE - 1:PAGE], cum.shape)
            sc.append(fn() - jnp.concatenate([cum] * tq, axis=0))
        carry_ref[...] = off
        sc = jnp.concatenate(sc, axis=1)
        if extra_mask is not None:
            sc = jnp.where(extra_mask, sc, NEG)
        m_old = m_ref[...]
        m_new = jnp.maximum(m_old, jnp.max(sc, axis=1, keepdims=True))
        alpha = jnp.exp(m_old - m_new)
        p = jnp.exp(sc - m_new)
        l_ref[...] = alpha * l_ref[...] + jnp.sum(p, axis=1, keepdims=True)
        acc = alpha * acc_ref[...]
        for k, fn in enumerate(value_fns):
            acc = acc + fn(p[:, k * PAGE:(k + 1) * PAGE].astype(BF16))
        acc_ref[...] = acc
        m_ref[...] = m_new

    attend([lambda r=r: _dot(qq, r[...].reshape(FOX_W, PAGE).astype(BF16)) for r in k_refs],
           [lambda p, r=r: _dot_nt(p, r[...].reshape(FOX_W, PAGE).astype(BF16)) for r in v_refs],
           [r[...] for r in lf_refs], None)

    @pl.when(s == pl.num_programs(1) - 1)
    def _():
        key = _iota((nr, PAGE), 1)
        ok = (key < tq) & (key <= _div(_iota((nr, PAGE), 0), 8))
        new = new_sc[...]
        attend([lambda: _dot_nt(qq, new[:, :FOX_W].astype(BF16))],
               [lambda p: _dot(p, new[:, FOX_W:].astype(BF16))], [lfn_ref[0]], ok)
        _diag_rows(o_ref, acc_ref[...] / jnp.maximum(l_ref[...], TINY))


def _fox_sample(qf, fkv_new, lfn_t, cache_kt, cache_lft, page_table, layer):
    nb, npages = page_table.shape
    tq = qf.shape[1]
    P = FOX_PAGES
    kvw = 2 * FOX_W
    seq = lambda b, s, pt: (b, 0, 0)

    def kv_spec(k, c):
        return pl.BlockSpec((None, None, None, H_FOX, HD, PAGE),
                            lambda b, s, pt: (layer, pt[b, s * P + k], c, 0, 0, 0))

    def lf_spec(k):
        return pl.BlockSpec((None, None, H_FOX, PAGE), lambda b, s, pt: (layer, pt[b, s * P + k], 0, 0))

    return pl.pallas_call(
        _fox_sample_kernel,
        grid_spec=pltpu.PrefetchScalarGridSpec(
            num_scalar_prefetch=1,
            grid=(nb, npages // P),
            in_specs=[pl.BlockSpec((1, tq, FOX_W), seq), pl.BlockSpec((1, tq, kvw), seq),
                      pl.BlockSpec((1, H_FOX, PAGE), seq)]
            + [kv_spec(k, 0) for k in range(P)] + [kv_spec(k, 1) for k in range(P)]
            + [lf_spec(k) for k in range(P)],
            out_specs=pl.BlockSpec((1, tq, FOX_W), seq),
            scratch_shapes=[pltpu.VMEM((tq * 8, FOX_W), BF16),
                            pltpu.VMEM((tq * 8, 1), F32), pltpu.VMEM((tq * 8, 1), F32),
                            pltpu.VMEM((tq * 8, FOX_W), F32),
                            pltpu.VMEM((H_FOX, PAGE), F32),
                            pltpu.VMEM((PAGE, kvw), F32)]),
        out_shape=jax.ShapeDtypeStruct((nb, tq, FOX_W), F32),
        compiler_params=_cparams(("arbitrary", "arbitrary")),
        name="fox_sample",
    )(page_table, qf, fkv_new, lfn_t, *([cache_kt] * (2 * P)), *([cache_lft] * P))


def _place_wide():
    r = _iota((LANES, NSA_W), 0)
    c = _iota((LANES, NSA_W), 1)
    return jnp.where(r == _div(c, NSA_GH * HD) * HD + _mod(c, HD), 1.0, 0.0).astype(BF16)


def _rb_col(rbt_ref):
    return lambda bb: rbt_ref[:, bb:bb + 1]


def _fold_heads():
    r = _iota((NSA_W, LANES), 0)
    c = _iota((NSA_W, LANES), 1)
    return jnp.where(c == _div(r, NSA_GH * HD) * HD + _mod(r, HD), 1.0, 0.0).astype(BF16)


def _ns_attend_kernel(q_ref, kcw_ref, kwt_ref, vwt_ref, wn_ref, rbt_ref, oc_ref, ow_ref, mb_ref, new_sc, *, past):
    tq = q_ref.shape[1]
    nr = tq * 8
    n = kcw_ref.shape[2] - CMP_PAD_FRONT - CMP_PAD_BACK
    n_selp = past // SEL_LEN
    wb = kwt_ref.shape[2]
    qq = _rows_th(q_ref[0])
    kc = kcw_ref[0, 0, CMP_PAD_FRONT:CMP_PAD_FRONT + n, :]
    vc = kcw_ref[0, 1, CMP_PAD_FRONT:CMP_PAD_FRONT + n, :]
    trow = _div(_iota((nr, n), 0), 8)
    dist = past + trow - CMP_STRIDE * _iota((nr, n), 1) - (CMP_LEN - 1)
    bias = _bias_lookup(_bucket(jnp.clip(dist, 0, MAX_DISTANCE - 1)), _rb_col(rbt_ref))
    ok = dist >= 0
    s = jnp.where(ok, _dot_nt(qq, kc) + bias, NEG)
    m = jnp.max(s, axis=1, keepdims=True)
    p = jnp.where(ok, jnp.exp(s - m), 0.0)
    p = p / jnp.maximum(jnp.sum(p, axis=1, keepdims=True), TINY)
    _diag_rows(oc_ref, _dot(p.astype(BF16), vc))
    mov = jnp.where(_overlap(_iota((n, LANES), 0), _iota((n, LANES), 1)), 1.0, 0.0).astype(BF16)
    x = _dot2_l(p, mov)
    z = x + pltpu.roll(x, nr - 1, 0) + pltpu.roll(x, nr - 2, 0) + pltpu.roll(x, nr - 3, 0)
    z0 = jnp.where(_mod(_iota(z.shape, 0), NSA_GH) == 0, z, 0.0)
    p_slc = z0 + pltpu.roll(z0, 1, 0) + pltpu.roll(z0, 2, 0) + pltpu.roll(z0, 3, 0)
    jb = _iota((nr, LANES), 1)
    forced = (jb == 0) | (jb == n_selp - 1)
    score = jnp.where(forced, FORCE_SCORE, jnp.where(jb < n_selp, p_slc, -1.0))
    k_past = min(N_SELECT, n_selp + 1) - 1
    sel = _topk_mask(score, k_past)
    mb_ref[0] = jnp.where((sel > 0.5) & (score >= 0.0), 0.0, NEG).astype(BF16)
    qf = _dot(qq, _fold_heads()).astype(BF16)
    kwt = kwt_ref[...].reshape(LANES, wb).astype(BF16)
    vwt = vwt_ref[...].reshape(LANES, wb).astype(BF16)
    new_sc[...] = jnp.zeros_like(new_sc)
    new_sc[0:tq, :] = wn_ref[0]
    wnew = new_sc[...]
    dw = _div(_iota((nr, wb), 0), 8) + wb - _iota((nr, wb), 1)
    okw = (dw >= 0) & (dw < WINDOW)
    sw = _dot(qf, kwt) + _bias_lookup(_bucket(jnp.clip(dw, 0, MAX_DISTANCE - 1)), _rb_col(rbt_ref))
    sw = jnp.where(okw, sw, NEG)
    dn = _div(_iota((nr, PAGE), 0), 8) - _iota((nr, PAGE), 1)
    okn = (dn >= 0) & (_iota((nr, PAGE), 1) < tq)
    sn = _dot_nt(qf, wnew[:, 0:LANES].astype(BF16))
    sn = sn + _bias_lookup(_bucket(jnp.clip(dn, 0, MAX_DISTANCE - 1)), _rb_col(rbt_ref))
    sn = jnp.where(okn, sn, NEG)
    mw = jnp.maximum(jnp.max(sw, axis=1, keepdims=True), jnp.max(sn, axis=1, keepdims=True))
    pw = jnp.where(okw, jnp.exp(sw - mw), 0.0)
    pn = jnp.where(okn, jnp.exp(sn - mw), 0.0)
    lw = jnp.maximum(jnp.sum(pw, axis=1, keepdims=True) + jnp.sum(pn, axis=1, keepdims=True), TINY)
    a = (_dot_nt(pw.astype(BF16), vwt) + _dot(pn.astype(BF16), wnew[:, LANES:2 * LANES].astype(BF16))) / lw
    _diag_rows(ow_ref, _dot2_l(a, _place_wide()))


def _ns_attend(qn, kcw, win_t, wkv_new, rbt, layer, past):
    nb, tq, _ = qn.shape
    npad = kcw.shape[2]
    wb = win_t.shape[5]
    seq = lambda b: (b, 0, 0)

    def win_spec(c):
        return pl.BlockSpec((None, None, None, NSA_KV, HD, wb), lambda b: (layer, b, c, 0, 0, 0))

    return pl.pallas_call(
        functools.partial(_ns_attend_kernel, past=past),
        grid=(nb,),
        in_specs=[pl.BlockSpec((1, tq, NSA_W), seq),
                  pl.BlockSpec((1, 2, npad, NSA_W), lambda b: (b, 0, 0, 0)),
                  win_spec(0), win_spec(1),
                  pl.BlockSpec((1, tq, 2 * LANES), seq),
                  pl.BlockSpec((tq * 8, N_BUCKETS), lambda b: (0, 0))],
        out_specs=[pl.BlockSpec((1, tq, NSA_W), seq), pl.BlockSpec((1, tq, NSA_W), seq),
                   pl.BlockSpec((1, tq * 8, LANES), seq)],
        out_shape=[jax.ShapeDtypeStruct((nb, tq, NSA_W), F32), jax.ShapeDtypeStruct((nb, tq, NSA_W), F32),
                   jax.ShapeDtypeStruct((nb, tq * 8, LANES), BF16)],
        scratch_shapes=[pltpu.VMEM((PAGE, 2 * LANES), F32)],
        compiler_params=_cparams(("arbitrary",)),
        name="ns_attend",
    )(qn, kcw, win_t, win_t, wkv_new, rbt)


def _ns_select_kernel(pt_tab, *refs, past):
    P = PAGES_PER_STEP
    q_ref, mb_ref, nn_ref, rbt_ref = refs[0:4]
    k_refs = refs[4:4 + P]
    v_refs = refs[4 + P:4 + 2 * P]
    o_ref, q_sc, m_ref, l_ref, acc_ref, new_sc = refs[4 + 2 * P:]
    s = pl.program_id(1)
    tq = q_ref.shape[1]
    nr = tq * 8

    @pl.when(s == 0)
    def _():
        q_sc[...] = _dot(_rows_th(q_ref[0]), _fold_heads()).astype(BF16)
        _softmax_init(m_ref, l_ref, acc_ref)
        new_sc[...] = jnp.zeros_like(new_sc)
        new_sc[0:tq, :] = nn_ref[0][:, 2 * LANES:]

    qq = q_sc[...]
    trow = _div(_iota((nr, PAGE), 0), 8)
    key = _iota((nr, PAGE), 1)

    def attend(score_fns, value_fns, biases):
        sc = jnp.concatenate([fn() for fn in score_fns], axis=1) + biases
        m_old = m_ref[...]
        m_new = jnp.maximum(m_old, jnp.max(sc, axis=1, keepdims=True))
        alpha = jnp.exp(m_old - m_new)
        p = jnp.exp(sc - m_new)
        l_ref[...] = alpha * l_ref[...] + jnp.sum(p, axis=1, keepdims=True)
        acc = alpha * acc_ref[...]
        for k, fn in enumerate(value_fns):
            acc = acc + fn(p[:, k * PAGE:(k + 1) * PAGE].astype(BF16))
        acc_ref[...] = acc
        m_ref[...] = m_new

    blk = _iota((LANES, P * PAGE), 0)
    kcol = _iota((LANES, P * PAGE), 1)
    expand = jnp.where(blk == _div(s * (P * PAGE) + kcol, SEL_LEN), 1.0, 0.0).astype(BF16)
    bias = _dot(mb_ref[0], expand) + rbt_ref[:, N_BUCKETS - 1:N_BUCKETS]
    tiles = ([lambda r=r: _dot(qq, r[...].reshape(LANES, PAGE).astype(BF16)) for r in k_refs],
             [lambda p, r=r: _dot_nt(p, r[...].reshape(LANES, PAGE).astype(BF16)) for r in v_refs])

    @pl.when(s < pl.num_programs(1) - 1)
    def _():
        attend(*tiles, bias)

    @pl.when(s == pl.num_programs(1) - 1)
    def _():
        d_last = past + trow - (past - PAGE + key)
        b_last = _bias_lookup(_bucket(jnp.clip(d_last, 0, MAX_DISTANCE - 1)), _rb_col(rbt_ref))
        fix = jnp.concatenate([jnp.zeros((nr, (P - 1) * PAGE), F32),
                               b_last - rbt_ref[:, N_BUCKETS - 1:N_BUCKETS]], axis=1)
        attend(*tiles, bias + fix)
        d_new = trow - key
        b_new = _bias_lookup(_bucket(jnp.clip(d_new, 0, MAX_DISTANCE - 1)), _rb_col(rbt_ref))
        new = new_sc[...]
        attend([lambda: _dot_nt(qq, new[:, 0:LANES].astype(BF16))],
               [lambda p: _dot(p, new[:, LANES:2 * LANES].astype(BF16))],
               jnp.where((d_new >= 0) & (key < tq), b_new, NEG))
        a = acc_ref[...] / jnp.maximum(l_ref[...], TINY)
        place = _place_wide()
        _diag_rows(o_ref, _dot2_l(a, place))


def _ns_select(qn, maskb, nkv_new, rbt, cache_nsa, page_table, layer, past):
    nb, npages = page_table.shape
    tq = qn.shape[1]
    P = PAGES_PER_STEP
    seq = lambda b, s, pt: (b, 0, 0)

    def page_spec(k, c):
        return pl.BlockSpec((None, None, None, NSA_KV, HD, PAGE),
                            lambda b, s, pt: (layer, pt[b, s * P + k], c, 0, 0, 0))

    return pl.pallas_call(
        functools.partial(_ns_select_kernel, past=past),
        grid_spec=pltpu.PrefetchScalarGridSpec(
            num_scalar_prefetch=1,
            grid=(nb, npages // P),
            in_specs=[pl.BlockSpec((1, tq, NSA_W), seq), pl.BlockSpec((1, tq * 8, LANES), seq),
                      pl.BlockSpec((1, tq, 4 * LANES), seq),
                      pl.BlockSpec((tq * 8, N_BUCKETS), lambda b, s, pt: (0, 0))]
            + [page_spec(k, 2) for k in range(P)] + [page_spec(k, 3) for k in range(P)],
            out_specs=pl.BlockSpec((1, tq, NSA_W), seq),
            scratch_shapes=[pltpu.VMEM((tq * 8, LANES), BF16),
                            pltpu.VMEM((tq * 8, 1), F32), pltpu.VMEM((tq * 8, 1), F32),
                            pltpu.VMEM((tq * 8, LANES), F32),
                            pltpu.VMEM((PAGE, 2 * LANES), F32)]),
        out_shape=jax.ShapeDtypeStruct((nb, tq, NSA_W), F32),
        compiler_params=_cparams(("arbitrary", "arbitrary")),
        name="ns_select",
    )(page_table, qn, maskb, nkv_new, rbt, *([cache_nsa] * (2 * P)))


def _prep_w_in(w):
    d = w.shape[0]
    o_logf = 3 * FOX_W
    o_qn = o_logf + H_FOX
    o_kv = o_qn + NSA_W
    o_gate = o_kv + 6 * NSA_KV * HD
    misc = jnp.concatenate([w[:, o_logf:o_qn], w[:, o_gate:o_gate + 3 * H_NSA],
                            jnp.zeros((d, LANES - H_FOX - 3 * H_NSA), w.dtype)], axis=1)
    return jnp.concatenate([w[:, :o_logf], w[:, o_qn:o_kv], w[:, o_kv:o_gate], misc], axis=1).astype(BF16)


def _prep_cmp(pos, w1, b1, w2):
    eye = jnp.eye(NSA_KV, dtype=w1.dtype)
    w1r = w1.reshape(2, 2, CMP_STRIDE, HD, CMP_HID)
    wide = jnp.einsum("whjdc,ab->whjadbc", w1r, eye).reshape(2, 2, CMP_STRIDE * LANES, 2 * CMP_HID)
    posr = jnp.broadcast_to(pos.reshape(2, 2, CMP_STRIDE, 1, HD), (2, 2, CMP_STRIDE, NSA_KV, HD))
    posr = posr.reshape(2, 2, 1, CMP_STRIDE * LANES)
    group_of_head = (jnp.arange(H_NSA) // NSA_GH)[None, :] == jnp.arange(NSA_KV)[:, None]
    w2w = jnp.einsum("wcd,ah->wachd", w2, group_of_head.astype(w2.dtype)).reshape(2, 2 * CMP_HID, NSA_W)
    return {"pt": posr[:, 0], "pb": posr[:, 1],
            "w1t": wide[:, 0].astype(BF16), "w1b": wide[:, 1].astype(BF16),
            "b1": jnp.concatenate([b1, b1], axis=-1)[:, None, :],
            "w2w": w2w.astype(BF16)}


def _far_bias_rows(rel_bias):
    far = rel_bias[N_BUCKETS - 1]
    hi = far.astype(BF16).astype(F32)
    r = far - hi
    mid = r.astype(BF16).astype(F32)
    lo = (r - mid).astype(BF16).astype(F32)
    rows = jnp.zeros((H_NSA, LANES), F32)
    rows = rows.at[:, HD].set(hi).at[:, HD + 1].set(mid).at[:, HD + 2].set(lo)
    return rows.reshape(NSA_KV, NSA_GH, LANES)


def kernel(x_prompt, x_sample, cache_fox_kv, cache_fox_logf, cache_nsa_kv, state_win_kv, state_conv,
           page_table, norm1_g, w_in, b_forget, cmp_pos, cmp_w1, cmp_b1, cmp_w2, out_norm_g, w_out,
           norm2_g, w_gu, conv_w, conv_b, w_down, rel_bias, final_norm_g):
    B, S, D = x_prompt.shape
    nb, tq, _ = x_sample.shape
    depth = w_in.shape[0]
    n_pool = cache_fox_kv.shape[1]
    npages = page_table.shape[1]
    past = npages * PAGE
    dff = w_down.shape[1]
    wb = state_win_kv.shape[2]
    assert tq & (tq - 1) == 0 and tq >= CONV_W - 1
    assert S % 256 == 0 and S // SEL_LEN <= LANES and past // SEL_LEN <= LANES
    assert npages % PAGES_PER_STEP == 0 and wb == WINDOW and past >= WINDOW and tq * 8 <= LANES

    assert npages % FOX_PAGES == 0 and PAGE == LANES
    cache_kt = jnp.transpose(cache_fox_kv, (0, 1, 3, 4, 5, 2))
    cache_lft = jnp.swapaxes(cache_fox_logf, 2, 3)
    cache_nsa = jnp.transpose(cache_nsa_kv, (0, 1, 3, 4, 5, 2))
    win_t = jnp.transpose(state_win_kv, (0, 1, 3, 4, 5, 2))
    rbt = jnp.tile(rel_bias.T, (tq, 1))
    brow = _far_bias_rows(rel_bias * LOG2E)
    gf = final_norm_g.reshape(1, D)

    xp = x_prompt.reshape(B * S, D)
    xs = x_sample.reshape(nb * tq, D)
    outs = [[] for _ in range(10)]
    for l in range(depth):
        w_l = _prep_w_in(w_in[l])
        bf = jnp.zeros((1, LANES), F32).at[0, :H_FOX].set(b_forget[l])
        cw = _prep_cmp(cmp_pos[l], cmp_w1[l], cmp_b1[l], cmp_w2[l])
        g1 = norm1_g[l].reshape(1, D)
        g2 = norm2_g[l].reshape(1, D)
        go = out_norm_g[l].reshape(1, D)
        wo = w_out[l].astype(BF16)
        wgu = w_gu[l].astype(BF16)
        wd = w_down[l].astype(BF16)
        cb = conv_b[l].reshape(1, dff)
        final = l == depth - 1

        qf, fkv, qn, nkv, wkv, misc = _proj(xp, g1, w_l, bf)
        r3 = lambda a: a.reshape(B, S, a.shape[-1])
        qa, ka, vb = _fox_prep(r3(qf), r3(fkv), r3(misc))
        vt = jnp.swapaxes(vb, 1, 2).reshape(B, H_FOX, HD, S)
        vt = jnp.concatenate([vt, jnp.ones((B, H_FOX, ONES_ROWS, S), BF16)], axis=2)
        o_fox = _fox_flash(qa, ka, vt)
        kcw = _compress_prompt(r3(nkv), cw)
        ksa, kwa = _nsa_prep(r3(nkv), r3(wkv))
        vst = r3(nkv)[:, :, 3 * LANES:].astype(BF16).reshape(B, S, NSA_KV, HD).transpose(0, 2, 3, 1)
        vst = jnp.concatenate([vst, jnp.ones((B, NSA_KV, ONES_ROWS, S), BF16)], axis=2)
        o_c, maskq = _nsa_cmp(rel_bias, r3(qn), kcw)
        o_s = _nsa_sel(rel_bias, r3(qn), maskq, brow, ksa, vst)
        vwt = r3(wkv)[:, :, LANES:].astype(BF16).reshape(B, S, NSA_KV, HD).transpose(0, 2, 3, 1)
        vwt = jnp.pad(vwt, ((0, 0), (0, 0), (0, 0), (WINDOW, 0)))
        vwt = jnp.concatenate([vwt, jnp.ones((B, NSA_KV, ONES_ROWS, S + WINDOW), BF16)], axis=2)
        o_w = _nsa_win(rel_bias, r3(qn), jnp.pad(kwa, ((0, 0), (0, 0), (WINDOW, 0), (0, 0))), vwt)
        f2 = lambda a: a.reshape(B * S, a.shape[-1])
        xp = _merge(xp, f2(o_fox), f2(o_c), f2(o_s), f2(o_w), misc, go, wo)
        xp, gtail = _ffn(xp, g2, wgu, conv_w[l], cb, wd, gf, seq_len=S, final=final)
        tiles_per_seq = gtail.shape[0] // B
        conv_p = gtail.reshape(B, tiles_per_seq, 8, dff)[:, -1, 8 - (CONV_W - 1):, :]
        outs[0].append(fkv.reshape(B, S, 2, H_FOX, HD))
        outs[2].append(misc[:, :H_FOX].reshape(B, S, H_FOX))
        outs[4].append(nkv.reshape(B, S, 4, NSA_KV, HD))
        outs[6].append(wkv.reshape(B, S, 2, NSA_KV, HD)[:, -min(WINDOW, S):])
        outs[8].append(conv_p)

        qf, fkv, qn, nkv, wkv, misc = _proj(xs, g1, w_l, bf)
        s3 = lambda a: a.reshape(nb, tq, a.shape[-1])
        lfn = jnp.swapaxes(s3(misc)[:, :, :H_FOX], 1, 2)
        lfn = jnp.pad(lfn, ((0, 0), (0, 0), (0, PAGE - tq)))
        o_fox = _fox_sample(s3(qf), s3(fkv), lfn, cache_kt, cache_lft, page_table, l)
        kcw = _compress_sample(cache_nsa, page_table, l, cw)
        o_c, o_w, maskb = _ns_attend(s3(qn), kcw, win_t, s3(wkv), rbt, l, past)
        o_s = _ns_select(s3(qn), maskb, s3(nkv), rbt, cache_nsa, page_table, l, past)
        s2 = lambda a: a.reshape(nb * tq, a.shape[-1])
        xs = _merge(xs, s2(o_fox), s2(o_c), s2(o_s), s2(o_w), misc, go, wo)
        hist = state_conv[l]
        zero = jnp.zeros((nb, 1, dff), F32)
        hm1 = jnp.concatenate([hist[:, 1:2], zero, zero, zero][:tq], axis=1).reshape(nb * tq, dff)
        hm2 = jnp.concatenate([hist[:, 0:1], hist[:, 1:2], zero, zero][:tq], axis=1).reshape(nb * tq, dff)
        xs, gfull = _ffn(xs, g2, wgu, conv_w[l], cb, wd, gf, seq_len=None, final=final, hist=(hm1, hm2),
                         short_len=tq)
        win_new = jnp.concatenate([state_win_kv[l], wkv.reshape(nb, tq, 2, NSA_KV, HD)], axis=1)[:, -wb:]
        outs[1].append(fkv.reshape(nb, tq, 2, H_FOX, HD))
        outs[3].append(misc[:, :H_FOX].reshape(nb, tq, H_FOX))
        outs[5].append(nkv.reshape(nb, tq, 4, NSA_KV, HD))
        outs[7].append(win_new)
        outs[9].append(gfull.reshape(nb, tq, dff)[:, -(CONV_W - 1):])

    st = [jnp.stack(o) for o in outs]
    return (xp.reshape(B, S, D), xs.reshape(nb, tq, D),
            st[0], st[1], st[2], st[3], st[4], st[5], st[6], st[7], st[8], st[9])
```

```python
import functools
import math

import numpy as np
import jax
import jax.numpy as jnp
from jax import lax
from jax.experimental import pallas as pl
from jax.experimental.pallas import tpu as pltpu

F32 = jnp.float32
BF16 = jnp.bfloat16
I32 = jnp.int32

HD = 64
H_FOX = 8
H_NSA = 8
NSA_KV = 2
NSA_GH = 4
FOX_W = H_FOX * HD
NSA_W = H_NSA * HD
CMP_LEN = 32
CMP_STRIDE = 16
CMP_HID = 256
SEL_LEN = 64
N_SELECT = 16
WINDOW = 512
N_BUCKETS = 32
MAX_DISTANCE = 128
CONV_W = 3
PAGE = 128
EPS = 1e-6
NEG = -1e30
TINY = 1e-30
FORCE_SCORE = 1e4
SCALE = HD ** -0.5
LOG2E = math.log2(math.e)
ONES_ROWS = 8
LANES = 128
VMEM_LIMIT = 56 * 1024 * 1024

C_QF, C_FKV, C_QN, C_NKV, C_WKV, C_MISC, C_END = 0, 512, 1536, 2048, 2560, 2816, 2944
MISC_LOGF = 0
MISC_GATE = 8

AUG0 = HD
AUG1 = HD + 3

PAGES_PER_STEP = 32


def _bucket_thresholds():
    exact = N_BUCKETS // 2
    n = np.arange(1, 4 * MAX_DISTANCE, dtype=np.float64)
    far = exact + (np.log(n / exact) / math.log(MAX_DISTANCE / exact) * (N_BUCKETS - exact)).astype(np.int64)
    b = np.where(n < exact, n, np.minimum(far, N_BUCKETS - 1)).astype(np.int64)
    return [int(n[b >= k].min()) for k in range(exact + 1, N_BUCKETS)]


_THR = _bucket_thresholds()


def _cparams(sem):
    return pltpu.CompilerParams(dimension_semantics=sem, vmem_limit_bytes=VMEM_LIMIT)


def _dot(a, b):
    return jnp.dot(a, b, preferred_element_type=F32)


def _dot_nt(a, b):
    return lax.dot_general(a, b, (((1,), (1,)), ((), ())), preferred_element_type=F32)


def _split2(x):
    hi = x.astype(BF16)
    lo = (x - hi.astype(F32)).astype(BF16)
    return hi, lo


def _split3(x):
    hi = x.astype(BF16)
    r = x - hi.astype(F32)
    mid = r.astype(BF16)
    lo = (r - mid.astype(F32)).astype(BF16)
    return hi, mid, lo


def _dot3_l(x, m):
    hi, mid, lo = _split3(x)
    return _dot(hi, m) + _dot(mid, m) + _dot(lo, m)


def _dot3_r(m, x):
    hi, mid, lo = _split3(x)
    return _dot(m, hi) + _dot(m, mid) + _dot(m, lo)


def _dot2_l(x, m):
    hi, lo = _split2(x)
    return _dot(hi, m) + _dot(lo, m)


def _iota(shape, dim):
    return lax.broadcasted_iota(I32, shape, dim)


def _div(x, k):
    return lax.shift_right_arithmetic(x, jnp.int32(k.bit_length() - 1))


def _mod(x, k):
    return x & (k - 1)


def _rms(x, g):
    r = lax.rsqrt(jnp.mean(x * x, axis=-1, keepdims=True) + EPS)
    return (x * r) * g


def _bucket(d):
    far = jnp.full(d.shape, N_BUCKETS // 2, I32)
    for thr in _THR:
        far = far + (d >= thr).astype(I32)
    return jnp.where(d < N_BUCKETS // 2, d, far)


def _bias_lookup(bucket, rb_get):
    acc = jnp.zeros(bucket.shape, F32)
    for b in range(N_BUCKETS):
        acc = jnp.where(bucket == b, rb_get(b), acc)
    return acc


def _topk_mask(score, k, axis=1):
    idx = _iota(score.shape, axis).astype(F32)

    def body(_, c):
        work, sel = c
        mx = jnp.max(work, axis=axis, keepdims=True)
        first = jnp.min(jnp.where(work == mx, idx, 1e9), axis=axis, keepdims=True)
        hit = idx == first
        return jnp.where(hit, -3.0, work), jnp.where(hit, 1.0, sel)

    _, sel = lax.fori_loop(0, k, body, (score, jnp.zeros(score.shape, F32)))
    return sel


def _proj_kernel(x_ref, g_ref, w_ref, bf_ref, qf_ref, fkv_ref, qn_ref, nkv_ref, wkv_ref, misc_ref):
    h = _rms(x_ref[...], g_ref[...]).astype(BF16)
    qf_ref[...] = _dot(h, w_ref[:, C_QF:C_FKV])
    fkv_ref[...] = _dot(h, w_ref[:, C_FKV:C_QN])
    qn_ref[...] = _dot(h, w_ref[:, C_QN:C_NKV])
    nkv_ref[...] = _dot(h, w_ref[:, C_NKV:C_WKV])
    wkv_ref[...] = _dot(h, w_ref[:, C_WKV:C_MISC])
    z = _dot(h, w_ref[:, C_MISC:C_END]) + bf_ref[...]
    lane = _iota(z.shape, 1)
    logsig = jnp.minimum(z, 0.0) - jnp.log(1.0 + jnp.exp(-jnp.abs(z)))
    sig = 1.0 / (1.0 + jnp.exp(-z))
    misc_ref[...] = jnp.where(lane < MISC_GATE, logsig, jnp.where(lane < MISC_GATE + 3 * H_NSA, sig, 0.0))


def _proj(x2, g, w, bf):
    T, D = x2.shape
    tm = min(512, T)
    widths = (C_FKV - C_QF, C_QN - C_FKV, C_NKV - C_QN, C_WKV - C_NKV, C_MISC - C_WKV, C_END - C_MISC)
    return pl.pallas_call(
        _proj_kernel,
        grid=(T // tm,),
        in_specs=[pl.BlockSpec((tm, D), lambda i: (i, 0)),
                  pl.BlockSpec((1, D), lambda i: (0, 0)),
                  pl.BlockSpec((D, C_END), lambda i: (0, 0)),
                  pl.BlockSpec((1, LANES), lambda i: (0, 0))],
        out_specs=[pl.BlockSpec((tm, wd), lambda i: (i, 0)) for wd in widths],
        out_shape=[jax.ShapeDtypeStruct((T, wd), F32) for wd in widths],
        compiler_params=_cparams(("arbitrary",)),
        name="proj",
    )(x2, g, w, bf)


def _fox_prep_kernel(qf_ref, k_ref, v_ref, misc_ref, qa_ref, ka_ref, vb_ref, carry_ref):
    i = pl.program_id(1)
    tm = qf_ref.shape[1]

    @pl.when(i == 0)
    def _():
        carry_ref[...] = jnp.zeros_like(carry_ref)

    lane = _iota((tm, LANES), 1)
    lf = jnp.where(lane < H_FOX, misc_ref[0], 0.0)
    tril = (_iota((tm, tm), 0) >= _iota((tm, tm), 1)).astype(BF16)
    cum = _dot3_r(tril, lf) + carry_ref[0:1, :]
    carry_ref[...] = jnp.broadcast_to(cum[tm - 1:tm, :], carry_ref.shape)
    cl2 = cum * LOG2E
    hi = cl2.astype(BF16).astype(F32)
    r = cl2 - hi
    mid = r.astype(BF16).astype(F32)
    lo = (r - mid).astype(BF16).astype(F32)
    cc = (hi + pltpu.roll(mid, H_FOX, 1) + pltpu.roll(lo, 2 * H_FOX, 1)).astype(BF16)
    er = _iota((LANES, H_FOX * LANES), 0)
    ec = _iota((LANES, H_FOX * LANES), 1)
    part = _div(er, H_FOX)
    head = _mod(er, H_FOX)
    inb = er < 3 * H_FOX
    eq = jnp.where(inb & (ec == head * LANES + AUG0 + part), 1.0, 0.0).astype(BF16)
    ek = jnp.where(inb & (ec == head * LANES + AUG1 + part), -1.0, 0.0).astype(BF16)
    cl = _mod(_iota((1, H_FOX * LANES), 1), LANES)
    ones_q = jnp.where((cl >= AUG1) & (cl < AUG1 + 3), 1.0, 0.0)
    ones_k = jnp.where((cl >= AUG0) & (cl < AUG0 + 3), 1.0, 0.0)
    cols_q = _dot(cc, eq) + ones_q
    cols_k = _dot(cc, ek) + ones_k
    q = qf_ref[0]
    k = k_ref[0]
    for h in range(H_FOX):
        a = (h // 2) * LANES
        qt = q[:, a:a + LANES]
        kt = k[:, a:a + LANES]
        if h % 2:
            qt = pltpu.roll(qt, HD, 1)
            kt = pltpu.roll(kt, HD, 1)
        qa_ref[0, h] = jnp.where(lane < HD, qt * (SCALE * LOG2E), cols_q[:, h * LANES:(h + 1) * LANES]).astype(BF16)
        ka_ref[0, h] = jnp.where(lane < HD, kt, cols_k[:, h * LANES:(h + 1) * LANES]).astype(BF16)
    vb_ref[0] = v_ref[0].astype(BF16)


def _fox_prep(qf, fkv, misc):
    B, S, _ = qf.shape
    tm = min(512, S)
    return pl.pallas_call(
        _fox_prep_kernel,
        grid=(B, S // tm),
        in_specs=[pl.BlockSpec((1, tm, FOX_W), lambda b, i: (b, i, 0)),
                  pl.BlockSpec((1, tm, FOX_W), lambda b, i: (b, i, 0)),
                  pl.BlockSpec((1, tm, FOX_W), lambda b, i: (b, i, 1)),
                  pl.BlockSpec((1, tm, LANES), lambda b, i: (b, i, 0))],
        out_specs=[pl.BlockSpec((1, H_FOX, tm, LANES), lambda b, i: (b, 0, i, 0)),
                   pl.BlockSpec((1, H_FOX, tm, LANES), lambda b, i: (b, 0, i, 0)),
                   pl.BlockSpec((1, tm, FOX_W), lambda b, i: (b, i, 0))],
        out_shape=[jax.ShapeDtypeStruct((B, H_FOX, S, LANES), BF16),
                   jax.ShapeDtypeStruct((B, H_FOX, S, LANES), BF16),
                   jax.ShapeDtypeStruct((B, S, FOX_W), BF16)],
        scratch_shapes=[pltpu.VMEM((8, LANES), F32)],
        compiler_params=_cparams(("arbitrary", "arbitrary")),
        name="fox_prep",
    )(qf, fkv, fkv, misc)


def _softmax_update(s, v, m_ref, l_ref, acc_ref):
    m_old = m_ref[...]
    m_new = jnp.maximum(m_old, jnp.max(s, axis=1, keepdims=True))
    alpha = jnp.exp(m_old - m_new)
    p = jnp.exp(s - m_new)
    l_ref[...] = alpha * l_ref[...] + jnp.sum(p, axis=1, keepdims=True)
    acc_ref[...] = alpha * acc_ref[...] + _dot(p.astype(BF16), v)
    m_ref[...] = m_new


def _softmax_init(m_ref, l_ref, acc_ref):
    m_ref[...] = jnp.full(m_ref.shape, NEG, F32)
    l_ref[...] = jnp.zeros(l_ref.shape, F32)
    acc_ref[...] = jnp.zeros(acc_ref.shape, F32)


def _softmax_update_t(s, vt, m_ref, acc_ref):
    m_old = m_ref[...]
    m_new = jnp.maximum(m_old, jnp.max(s, axis=0, keepdims=True))
    alpha = jnp.exp2(m_old - m_new)
    p = jnp.exp2(s - m_new).astype(BF16)
    acc_ref[...] = alpha * acc_ref[...] + _dot(vt, p)
    m_ref[...] = m_new


def _softmax_init_t(m_ref, acc_ref):
    m_ref[...] = jnp.full(m_ref.shape, NEG, F32)
    acc_ref[...] = jnp.zeros(acc_ref.shape, F32)


FOX_QUERY_TILE = 1024
FOX_KEY_TILE = 1024


def _fox_flash_kernel(qa_ref, ka_ref, vt_ref, o_ref, m0_ref, m1_ref, acc0_ref, acc1_ref):
    qi = pl.program_id(2)
    tq = qa_ref.shape[2]
    m_ref = (m0_ref, m1_ref)
    acc_ref = (acc0_ref, acc1_ref)
    for hh in range(2):
        _softmax_init_t(m_ref[hh], acc_ref[hh])

    tk = min(FOX_KEY_TILE, tq)
    nk = tq // tk

    def tile(kt, diag_block):
        ks = pl.multiple_of(kt * tk, tk)
        for hh in range(2):
            s = _dot_nt(ka_ref[0, hh, pl.ds(ks, tk), :], qa_ref[0, hh])
            if diag_block is not None:
                s = jnp.where(_iota(s.shape, 0) + diag_block * tk <= _iota(s.shape, 1), s, NEG)
            _softmax_update_t(s, vt_ref[0, hh, :, pl.ds(ks, tk)], m_ref[hh], acc_ref[hh])

    def full_tile(kt, c):
        tile(kt, None)
        return c

    lax.fori_loop(0, qi * nk, full_tile, 0)
    for kb in range(nk):
        tile(qi * nk + kb, kb)
    halves = []
    for hh in range(2):
        a = acc_ref[hh][...]
        halves.append(a[0:HD, :] / jnp.maximum(a[HD:HD + 1, :], TINY))
    o_ref[0] = jnp.concatenate(halves, axis=0).T


def _fox_flash(qa, ka, vt):
    B, H, S, _ = qa.shape
    tq = min(FOX_QUERY_TILE, S)
    rows = vt.shape[2]
    return pl.pallas_call(
        _fox_flash_kernel,
        grid=(B, H // 2, S // tq),
        in_specs=[pl.BlockSpec((1, 2, tq, LANES), lambda b, p, i: (b, p, i, 0)),
                  pl.BlockSpec((1, 2, S, LANES), lambda b, p, i: (b, p, 0, 0)),
                  pl.BlockSpec((1, 2, rows, S), lambda b, p, i: (b, p, 0, 0))],
        out_specs=pl.BlockSpec((1, tq, LANES), lambda b, p, i: (b, i, p)),
        out_shape=jax.ShapeDtypeStruct((B, S, FOX_W), F32),
        scratch_shapes=[pltpu.VMEM((1, tq), F32), pltpu.VMEM((1, tq), F32),
                        pltpu.VMEM((rows, tq), F32), pltpu.VMEM((rows, tq), F32)],
        compiler_params=_cparams(("arbitrary", "arbitrary", "arbitrary")),
        name="fox_flash",
    )(qa, ka, vt)


def _gelu_tanh(x):
    return 0.5 * x * (1.0 + jnp.tanh(math.sqrt(2.0 / math.pi) * (x + 0.044715 * (x * x * x))))


def _compress_core(x, pt, pb, w1t, w1b, b1, w2w):
    n = x.shape[0]
    a = _dot((x + pt).astype(BF16), w1t)
    b = _dot((x + pb).astype(BF16), w1b)
    h = a + pltpu.roll(b, n - 1, 0) + b1
    return _dot(_gelu_tanh(h).astype(BF16), w2w)


CMP_PAD_FRONT = 16
CMP_PAD_BACK = 112


def _store_cmp(out_ref, idx, res, n):
    out_ref[idx] = jnp.zeros(out_ref.shape[len(idx):], BF16)
    out_ref[idx + (slice(CMP_PAD_FRONT, CMP_PAD_FRONT + n), slice(None))] = res.astype(BF16)


def _compress_prompt_kernel(x_ref, pt_ref, pb_ref, w1t_ref, w1b_ref, b1_ref, w2w_ref, out_ref):
    n = x_ref.shape[1] // CMP_STRIDE
    x = jnp.concatenate([x_ref[0, pl.ds(j, n, stride=CMP_STRIDE), :] for j in range(CMP_STRIDE)], axis=1)
    res = _compress_core(x, pt_ref[0], pb_ref[0], w1t_ref[0], w1b_ref[0], b1_ref[0], w2w_ref[0])
    _store_cmp(out_ref, (0, 0), res, n)


def _compress_prompt(nkv, cw):
    B, S, _ = nkv.shape
    n = S // CMP_STRIDE
    npad = n + CMP_PAD_FRONT + CMP_PAD_BACK
    kx = CMP_STRIDE * LANES
    return pl.pallas_call(
        _compress_prompt_kernel,
        grid=(B, 2),
        in_specs=[pl.BlockSpec((1, S, LANES), lambda b, w: (b, 0, w)),
                  pl.BlockSpec((1, 1, kx), lambda b, w: (w, 0, 0)),
                  pl.BlockSpec((1, 1, kx), lambda b, w: (w, 0, 0)),
                  pl.BlockSpec((1, kx, 2 * CMP_HID), lambda b, w: (w, 0, 0)),
                  pl.BlockSpec((1, kx, 2 * CMP_HID), lambda b, w: (w, 0, 0)),
                  pl.BlockSpec((1, 1, 2 * CMP_HID), lambda b, w: (w, 0, 0)),
                  pl.BlockSpec((1, 2 * CMP_HID, NSA_W), lambda b, w: (w, 0, 0))],
        out_specs=pl.BlockSpec((1, 1, npad, NSA_W), lambda b, w: (b, w, 0, 0)),
        out_shape=jax.ShapeDtypeStruct((B, 2, npad, NSA_W), BF16),
        compiler_params=_cparams(("arbitrary", "arbitrary")),
        name="compress_prompt",
    )(nkv, cw["pt"], cw["pb"], cw["w1t"], cw["w1b"], cw["b1"], cw["w2w"])


def _compress_sample_kernel(pt_tab, *refs):
    P = PAGES_PER_STEP
    pages = (refs[:P], refs[P:2 * P])
    pt_ref, pb_ref, w1t_ref, w1b_ref, b1_ref, w2w_ref, out_ref, x_sc, rows_sc = refs[2 * P:]
    s = pl.program_id(1)
    n = x_sc.shape[1]
    rows = P * (PAGE // CMP_STRIDE)
    r0 = pl.multiple_of(s * rows, rows)
    for w in range(2):
        for k, pg in enumerate(pages[w]):
            rows_sc[k * PAGE:(k + 1) * PAGE, :] = pg[...].reshape(LANES, PAGE).T
        for j in range(CMP_STRIDE):
            x_sc[w, pl.ds(r0, rows), j * LANES:(j + 1) * LANES] = rows_sc[pl.ds(j, rows, stride=CMP_STRIDE), :]

    @pl.when(s == pl.num_programs(1) - 1)
    def _():
        for w in range(2):
            res = _compress_core(x_sc[w], pt_ref[w], pb_ref[w], w1t_ref[w], w1b_ref[w], b1_ref[w], w2w_ref[w])
            _store_cmp(out_ref, (0, w), res, n)


def _compress_sample(cache_nsa, page_table, layer, cw):
    nb, npages = page_table.shape
    P = PAGES_PER_STEP
    n = npages * (PAGE // CMP_STRIDE)
    npad = n + CMP_PAD_FRONT + CMP_PAD_BACK
    kx = CMP_STRIDE * LANES

    def page_spec(k, w):
        return pl.BlockSpec((None, None, None, NSA_KV, HD, PAGE),
                            lambda b, s, pt: (layer, pt[b, s * P + k], w, 0, 0, 0))

    const3 = lambda b, s, pt: (0, 0, 0)
    return pl.pallas_call(
        _compress_sample_kernel,
        grid_spec=pltpu.PrefetchScalarGridSpec(
            num_scalar_prefetch=1,
            grid=(nb, npages // P),
            in_specs=[page_spec(k, 0) for k in range(P)] + [page_spec(k, 1) for k in range(P)] + [
                pl.BlockSpec((2, 1, kx), const3), pl.BlockSpec((2, 1, kx), const3),
                pl.BlockSpec((2, kx, 2 * CMP_HID), const3), pl.BlockSpec((2, kx, 2 * CMP_HID), const3),
                pl.BlockSpec((2, 1, 2 * CMP_HID), const3), pl.BlockSpec((2, 2 * CMP_HID, NSA_W), const3)],
            out_specs=pl.BlockSpec((1, 2, npad, NSA_W), lambda b, s, pt: (b, 0, 0, 0)),
            scratch_shapes=[pltpu.VMEM((2, n, kx), F32), pltpu.VMEM((P * PAGE, LANES), F32)]),
        out_shape=jax.ShapeDtypeStruct((nb, 2, npad, NSA_W), BF16),
        compiler_params=_cparams(("arbitrary", "arbitrary")),
        name="compress_sample",
    )(page_table, *([cache_nsa] * (2 * P)), cw["pt"], cw["pb"], cw["w1t"], cw["w1b"], cw["b1"], cw["w2w"])


def _nsa_prep_kernel(sel_ref, win_ref, ksa_ref, kwa_ref):
    i = pl.program_id(1)
    tm = sel_ref.shape[1]
    lane = _iota((tm, LANES), 1)
    pos = i * tm + _iota((tm, LANES), 0)
    onehot = jnp.where(_div(pos, SEL_LEN) == lane, 1.0, 0.0)
    ks = sel_ref[0][:, 0:LANES]
    kw = win_ref[0][:, 0:LANES]
    rks = pltpu.roll(ks, HD, 1)
    rkw = pltpu.roll(kw, HD, 1)
    for g in range(NSA_KV):
        left = jnp.where(lane < HD, ks if g == 0 else rks, jnp.where(lane < HD + 3, 1.0, 0.0))
        ksa_ref[0, g] = jnp.concatenate([left, onehot], axis=1).astype(BF16)
        kwa_ref[0, g] = jnp.where(lane < HD, kw if g == 0 else rkw, 0.0).astype(BF16)


def _nsa_prep(nkv, wkv):
    B, S, _ = nkv.shape
    tm = min(512, S)
    spec128 = pl.BlockSpec((1, NSA_KV, tm, LANES), lambda b, i: (b, 0, i, 0))
    return pl.pallas_call(
        _nsa_prep_kernel,
        grid=(B, S // tm),
        in_specs=[pl.BlockSpec((1, tm, 2 * LANES), lambda b, i: (b, i, 1)),
                  pl.BlockSpec((1, tm, 2 * LANES), lambda b, i: (b, i, 0))],
        out_specs=[pl.BlockSpec((1, NSA_KV, tm, 2 * LANES), lambda b, i: (b, 0, i, 0)), spec128],
        out_shape=[jax.ShapeDtypeStruct((B, NSA_KV, S, 2 * LANES), BF16),
                   jax.ShapeDtypeStruct((B, NSA_KV, S, LANES), BF16)],
        compiler_params=_cparams(("arbitrary", "arbitrary")),
        name="nsa_prep",
    )(nkv, wkv)


def _overlap(i_blk, j_blk):
    start = i_blk * CMP_STRIDE
    return (start < j_blk * SEL_LEN + SEL_LEN) & (start + CMP_LEN > j_blk * SEL_LEN)


NEAR_BACK = 16


def _nsa_cmp_kernel(rb_ref, q_ref, kcw_ref, oc_ref, mq_ref, fc_ref, qs_ref):
    b = pl.program_id(0)
    i = pl.program_id(1)
    tq = q_ref.shape[1]
    n = kcw_ref.shape[2] - CMP_PAD_FRONT - CMP_PAD_BACK
    n_sel = n * CMP_STRIDE // SEL_LEN
    k_eff = min(N_SELECT, n_sel)

    @pl.when((b == 0) & (i == 0))
    def _():
        dist = _iota((tq, LANES), 0) + (NEAR_BACK * CMP_STRIDE - (CMP_LEN - 1)) - CMP_STRIDE * _iota((tq, LANES), 1)
        bk = _bucket(jnp.clip(dist, 0, MAX_DISTANCE - 1))
        for h in range(H_NSA):
            fc_ref[h // NSA_GH, (h % NSA_GH) * tq:(h % NSA_GH + 1) * tq, :] = jnp.where(
                dist >= 0, _bias_lookup(bk, lambda bb, h=h: rb_ref[bb, h]), NEG)

    qs = i * tq
    i0 = qs // CMP_STRIDE - NEAR_BACK
    st = pl.multiple_of(qs // CMP_STRIDE, 16)
    mrows = NSA_GH * tq
    farmask = _iota((mrows, n), 1) < i0
    nearmask = (_iota((mrows, LANES), 1) + i0) >= 0
    rowhead = _div(_iota((mrows, 1), 0), tq)
    ps_far = []
    ps_near = []
    outs = []
    for g in range(NSA_KV):
        _stack_heads(q_ref[0][:, g * 2 * LANES:(g + 1) * 2 * LANES], jnp.zeros((NSA_GH, LANES), F32), None, qs_ref)
        qq = qs_ref[...]
        gl = slice(g * NSA_GH * HD, g * NSA_GH * HD + LANES)
        kfar = kcw_ref[0, 0, CMP_PAD_FRONT:CMP_PAD_FRONT + n, gl]
        vfar = kcw_ref[0, 1, CMP_PAD_FRONT:CMP_PAD_FRONT + n, gl]
        knear = kcw_ref[0, 0, pl.ds(st, LANES), gl]
        vnear = kcw_ref[0, 1, pl.ds(st, LANES), gl]
        bfar = jnp.zeros((mrows, 1), F32)
        for hh in range(NSA_GH):
            bfar = jnp.where(rowhead == hh, rb_ref[N_BUCKETS - 1, g * NSA_GH + hh], bfar)
        sf = jnp.where(farmask, _dot_nt(qq, kfar) + bfar, NEG)
        sn = jnp.where(nearmask, _dot_nt(qq, knear) + fc_ref[g], NEG)
        m = jnp.maximum(jnp.max(sf, axis=1, keepdims=True), jnp.max(sn, axis=1, keepdims=True))
        pf = jnp.where(sf > 0.5 * NEG, jnp.exp(sf - m), 0.0)
        pn = jnp.where(sn > 0.5 * NEG, jnp.exp(sn - m), 0.0)
        l = jnp.sum(pf, axis=1, keepdims=True) + jnp.sum(pn, axis=1, keepdims=True)
        inv = 1.0 / jnp.maximum(l, TINY)
        pf = pf * inv
        pn = pn * inv
        outs.append(_unstack_heads(_dot(pf.astype(BF16), vfar) + _dot(pn.astype(BF16), vnear), tq))
        ps_far.append(sum(pf[hh * tq:(hh + 1) * tq] for hh in range(NSA_GH)))
        ps_near.append(sum(pn[hh * tq:(hh + 1) * tq] for hh in range(NSA_GH)))
    oc_ref[0] = jnp.concatenate(outs, axis=1)

    mov_far = jnp.where(_overlap(_iota((n, LANES), 0), _iota((n, LANES), 1)), 1.0, 0.0).astype(BF16)
    mov_near = jnp.where(_overlap(_iota((LANES, LANES), 0) + i0, _iota((LANES, LANES), 1)), 1.0, 0.0).astype(BF16)
    qpos = qs + _iota((tq, LANES), 0)
    jb = _iota((tq, LANES), 1)
    qblk = _div(qpos, SEL_LEN)
    valid = jb * SEL_LEN <= qpos
    forced = (jb == 0) | (jb == qblk) | (jb == qblk - 1)
    scores = []
    for g in range(NSA_KV):
        p_slc = _dot2_l(ps_far[g], mov_far) + _dot2_l(ps_near[g], mov_near)
        scores.append(jnp.where(forced, FORCE_SCORE, jnp.where(valid, p_slc, -1.0)))
    st = jnp.concatenate([sc.T for sc in scores], axis=1)
    keep = jnp.where((_topk_mask(st, k_eff, axis=0) > 0.5) & (st >= 0.0), 0.0, NEG)
    for g in range(NSA_KV):
        mq_ref[0, g] = keep[:, g * tq:(g + 1) * tq].T.astype(BF16)


def _nsa_cmp(rb, qn, kcw):
    B, S, _ = qn.shape
    tq = min(256, S)
    npad = kcw.shape[2]
    return pl.pallas_call(
        _nsa_cmp_kernel,
        grid=(B, S // tq),
        in_specs=[pl.BlockSpec(memory_space=pltpu.SMEM),
                  pl.BlockSpec((1, tq, NSA_W), lambda b, i: (b, i, 0)),
                  pl.BlockSpec((1, 2, npad, NSA_W), lambda b, i: (b, 0, 0, 0))],
        out_specs=[pl.BlockSpec((1, tq, NSA_W), lambda b, i: (b, i, 0)),
                   pl.BlockSpec((1, NSA_KV, tq, LANES), lambda b, i: (b, 0, i, 0))],
        out_shape=[jax.ShapeDtypeStruct((B, S, NSA_W), F32),
                   jax.ShapeDtypeStruct((B, NSA_KV, S, LANES), BF16)],
        scratch_shapes=[pltpu.VMEM((NSA_KV, NSA_GH * tq, LANES), F32), pltpu.VMEM((NSA_GH * tq, LANES), BF16)],
        compiler_params=_cparams(("arbitrary", "arbitrary")),
        name="nsa_cmp",
    )(rb, qn, kcw)


def _stack_heads(q, g_rows, extra, qs_ref, scale=SCALE):
    tq = q.shape[0]
    lane = _iota((tq, LANES), 1)
    for hh in range(NSA_GH):
        a = (hh // 2) * LANES
        t = q[:, a:a + LANES]
        if hh % 2:
            t = pltpu.roll(t, HD, 1)
        left = jnp.where(lane < HD, t * scale, g_rows[hh:hh + 1, :]).astype(BF16)
        if extra is None:
            qs_ref[hh * tq:(hh + 1) * tq, :] = left
        else:
            qs_ref[hh * tq:(hh + 1) * tq, :] = jnp.concatenate([left, extra], axis=1)


def _unstack_heads(a, tq):
    lane = _iota((tq, LANES), 1)
    p0 = jnp.where(lane < HD, a[0:tq], a[tq:2 * tq])
    p1 = jnp.where(lane < HD, a[2 * tq:3 * tq], a[3 * tq:4 * tq])
    return jnp.concatenate([p0, p1], axis=1)


SEL_BACK = 128


def _nsa_sel_kernel(rb_ref, q_ref, mq_ref, brow_ref, ksa_ref, vt_ref, o_ref, dn_ref, qs_ref, m_ref, acc_ref):
    b = pl.program_id(0)
    g = pl.program_id(1)
    i = pl.program_id(2)
    tq = q_ref.shape[1]
    wn = tq + SEL_BACK

    @pl.when((b == 0) & (g == 0) & (i == 0))
    def _():
        dist = _iota((wn, tq), 1) + SEL_BACK - _iota((wn, tq), 0)
        bk = _bucket(jnp.clip(dist, 0, MAX_DISTANCE - 1))
        for h in range(H_NSA):
            far = rb_ref[N_BUCKETS - 1, h]
            val = (_bias_lookup(bk, lambda bb, h=h: rb_ref[bb, h]) - far) * LOG2E
            dn_ref[h // NSA_GH, :, (h % NSA_GH) * tq:(h % NSA_GH + 1) * tq] = jnp.where(dist >= 0, val, NEG)

    qs = i * tq
    _stack_heads(q_ref[0], brow_ref[0], mq_ref[0, 0], qs_ref, scale=SCALE * LOG2E)
    _softmax_init_t(m_ref, acc_ref)
    hw = (NSA_GH // 2) * tq

    def tile(start, size, bias):
        k = ksa_ref[0, 0, pl.ds(start, size), :]
        vt = vt_ref[0, 0, :, pl.ds(start, size)]
        for half in range(2):
            s = _dot_nt(k, qs_ref[half * hw:(half + 1) * hw, :])
            if bias is not None:
                s = s + bias(half)
            _softmax_update_t(s, vt, m_ref.at[half], acc_ref.at[half])

    @pl.when(i == 0)
    def _():
        tile(0, tq, lambda half: dn_ref[g, SEL_BACK:, half * hw:(half + 1) * hw])

    @pl.when(i > 0)
    def _():
        tile(pl.multiple_of(qs - SEL_BACK, LANES), wn, lambda half: dn_ref[g, :, half * hw:(half + 1) * hw])
        tile(pl.multiple_of(qs - tq, LANES), tq - SEL_BACK, None)

    n_far = jnp.maximum(i - 1, 0)

    def far_pair(kt, c):
        tile(pl.multiple_of(kt * 2 * tq, 2 * tq), 2 * tq, None)
        return c

    lax.fori_loop(0, n_far // 2, far_pair, 0)

    @pl.when(n_far % 2 == 1)
    def _():
        tile(pl.multiple_of((n_far - 1) * tq, tq), tq, None)

    parts = []
    for half in range(2):
        acc = acc_ref[half]
        a = acc[0:HD, :] / jnp.maximum(acc[HD:HD + 1, :], TINY)
        parts += [a[:, 0:tq], a[:, tq:2 * tq]]
    o_ref[0] = jnp.concatenate(parts, axis=0).T


def _nsa_sel(rb, qn, maskq, brow, ksa, vst):
    B, S, _ = qn.shape
    tq = min(512, S)
    G = NSA_KV
    return pl.pallas_call(
        _nsa_sel_kernel,
        grid=(B, G, S // tq),
        in_specs=[pl.BlockSpec(memory_space=pltpu.SMEM),
                  pl.BlockSpec((1, tq, 2 * LANES), lambda b, g, i: (b, i, g)),
                  pl.BlockSpec((1, 1, tq, LANES), lambda b, g, i: (b, g, i, 0)),
                  pl.BlockSpec((1, NSA_GH, LANES), lambda b, g, i: (g, 0, 0)),
                  pl.BlockSpec((1, 1, S, 2 * LANES), lambda b, g, i: (b, g, 0, 0)),
                  pl.BlockSpec((1, 1, HD + ONES_ROWS, S), lambda b, g, i: (b, g, 0, 0))],
        out_specs=pl.BlockSpec((1, tq, 2 * LANES), lambda b, g, i: (b, i, g)),
        out_shape=jax.ShapeDtypeStruct((B, S, NSA_W), F32),
        scratch_shapes=[pltpu.VMEM((G, tq + SEL_BACK, NSA_GH * tq), F32),
                        pltpu.VMEM((NSA_GH * tq, 2 * LANES), BF16),
                        pltpu.VMEM((2, 1, NSA_GH // 2 * tq), F32),
                        pltpu.VMEM((2, HD + ONES_ROWS, NSA_GH // 2 * tq), F32)],
        compiler_params=_cparams(("arbitrary", "arbitrary", "arbitrary")),
        name="nsa_sel",
    )(rb, qn, maskq, brow, ksa, vst)


def _nsa_win_kernel(rb_ref, q_ref, kwa_ref, vt_ref, o_ref, dw_ref, qs_ref):
    b = pl.program_id(0)
    g = pl.program_id(1)
    i = pl.program_id(2)
    tq = q_ref.shape[1]
    wk = tq + WINDOW

    @pl.when((b == 0) & (g == 0) & (i == 0))
    def _():
        dist = _iota((wk, tq), 1) + WINDOW - _iota((wk, tq), 0)
        bk = _bucket(jnp.clip(dist, 0, MAX_DISTANCE - 1))
        ok = (dist >= 0) & (dist < WINDOW)
        for h in range(H_NSA):
            val = _bias_lookup(bk, lambda bb, h=h: rb_ref[bb, h]) * LOG2E
            dw_ref[h // NSA_GH, :, (h % NSA_GH) * tq:(h % NSA_GH + 1) * tq] = jnp.where(ok, val, NEG)

    qs = pl.multiple_of(i * tq, tq)
    _stack_heads(q_ref[0], jnp.zeros((NSA_GH, LANES), F32), None, qs_ref, scale=SCALE * LOG2E)
    s = _dot_nt(kwa_ref[0, 0, pl.ds(qs, wk), :], qs_ref[...]) + dw_ref[g]
    s = jnp.where(_iota(s.shape, 0) + qs >= WINDOW, s, NEG)
    m = jnp.max(s, axis=0, keepdims=True)
    p = jnp.exp2(s - m).astype(BF16)
    acc = _dot(vt_ref[0, 0, :, pl.ds(qs, wk)], p)
    a = acc[0:HD, :] / jnp.maximum(acc[HD:HD + 1, :], TINY)
    o_ref[0] = jnp.concatenate([a[:, hh * tq:(hh + 1) * tq] for hh in range(NSA_GH)], axis=0).T


def _nsa_win(rb, qn, kwa_p, vwt_p):
    B, S, _ = qn.shape
    tq = min(256, S)
    G = NSA_KV
    sp = kwa_p.shape[2]
    return pl.pallas_call(
        _nsa_win_kernel,
        grid=(B, G, S // tq),
        in_specs=[pl.BlockSpec(memory_space=pltpu.SMEM),
                  pl.BlockSpec((1, tq, 2 * LANES), lambda b, g, i: (b, i, g)),
                  pl.BlockSpec((1, 1, sp, LANES), lambda b, g, i: (b, g, 0, 0)),
                  pl.BlockSpec((1, 1, HD + ONES_ROWS, sp), lambda b, g, i: (b, g, 0, 0))],
        out_specs=pl.BlockSpec((1, tq, 2 * LANES), lambda b, g, i: (b, i, g)),
        out_shape=jax.ShapeDtypeStruct((B, S, NSA_W), F32),
        scratch_shapes=[pltpu.VMEM((G, tq + WINDOW, NSA_GH * tq), F32),
                        pltpu.VMEM((NSA_GH * tq, LANES), BF16)],
        compiler_params=_cparams(("arbitrary", "arbitrary", "arbitrary")),
        name="nsa_win",
    )(rb, qn, kwa_p, vwt_p)


def _merge_kernel(x_ref, of_ref, oc_ref, os_ref, ow_ref, misc_ref, g_ref, w_ref, out_ref):
    tm = x_ref.shape[0]
    hi, lo = _split2(misc_ref[...])
    er = _iota((LANES, NSA_W), 0)
    ec = _iota((LANES, NSA_W), 1)
    onsa = jnp.zeros((tm, NSA_W), F32)
    for k, o_ref in enumerate((oc_ref, os_ref, ow_ref)):
        e = jnp.where(er == MISC_GATE + k * H_NSA + _div(ec, HD), 1.0, 0.0).astype(BF16)
        onsa = onsa + (_dot(hi, e) + _dot(lo, e)) * o_ref[...]
    g = g_ref[...]
    a = _rms(of_ref[...], g[:, :FOX_W]).astype(BF16)
    c = _rms(onsa, g[:, FOX_W:]).astype(BF16)
    out_ref[...] = x_ref[...] + _dot(a, w_ref[0:FOX_W, :]) + _dot(c, w_ref[FOX_W:, :])


def _merge(x2, ofox, oc, os_, ow, misc, g, w):
    T, D = x2.shape
    tm = min(512, T)
    row = lambda wd: pl.BlockSpec((tm, wd), lambda i: (i, 0))
    return pl.pallas_call(
        _merge_kernel,
        grid=(T // tm,),
        in_specs=[row(D), row(FOX_W), row(NSA_W), row(NSA_W), row(NSA_W), row(LANES),
                  pl.BlockSpec((1, D), lambda i: (0, 0)),
                  pl.BlockSpec((D, D), lambda i: (0, 0))],
        out_specs=row(D),
        out_shape=jax.ShapeDtypeStruct((T, D), F32),
        compiler_params=_cparams(("arbitrary",)),
        name="merge",
    )(x2, ofox, oc, os_, ow, misc, g, w)


FF_CHUNK = 1408


def _ffn_kernel(*refs, seq_len, short_len, final):
    if seq_len is None:
        (x_ref, g2_ref, wg_ref, wu_ref, cw_ref, cb_ref, wd_ref, gf_ref, hm1_ref, hm2_ref,
         out_ref, gt_ref, h_sc, acc_sc) = refs
    else:
        (x_ref, g2_ref, wg_ref, wu_ref, cw_ref, cb_ref, wd_ref, gf_ref,
         out_ref, gt_ref, h_sc, acc_sc, carry_sc) = refs
    i = pl.program_id(0)
    j = pl.program_id(1)
    tm = x_ref.shape[0]

    @pl.when(j == 0)
    def _():
        h_sc[...] = _rms(x_ref[...], g2_ref[...]).astype(BF16)
        acc_sc[...] = jnp.zeros_like(acc_sc)

    h = h_sc[...]
    gch = _dot(h, wg_ref[...])
    u = _dot(h, wu_ref[...])
    r1 = pltpu.roll(gch, 1, 0)
    r2 = pltpu.roll(gch, 2, 0)
    row = _iota(gch.shape, 0)
    if seq_len is None:
        t = _mod(row, short_len)
        m1 = jnp.where(t == 0, hm1_ref[...], r1)
        m2 = jnp.where(t < 2, hm2_ref[...], r2)
        gt_ref[...] = gch
    else:
        first = (i % (seq_len // tm)) == 0
        c = jnp.where(first, 0.0, carry_sc[j])
        m1 = jnp.where(row == 0, c[1:2, :], r1)
        m2 = jnp.where(row == 0, c[0:1, :], jnp.where(row == 1, c[1:2, :], r2))
        carry_sc[j, 0:2, :] = gch[tm - 2:tm, :]
        gt_ref[0] = gch[tm - 8:tm, :]
    cw = cw_ref[...]
    gc = cb_ref[...] + cw[0:1, :] * m2 + cw[1:2, :] * m1 + cw[2:3, :] * gch
    act = gc * (1.0 / (1.0 + jnp.exp(-gc)))
    acc_sc[...] += _dot((act * u).astype(BF16), wd_ref[...])

    @pl.when(j == pl.num_programs(1) - 1)
    def _():
        y = x_ref[...] + acc_sc[...]
        if final:
            y = _rms(y, gf_ref[...])
        out_ref[...] = y


def _ffn(x2, g2, wgu, cw, cb, wd, gf, *, seq_len, final, hist=None, short_len=None):
    T, D = x2.shape
    dff = wd.shape[0]
    fc = FF_CHUNK if dff % FF_CHUNK == 0 else LANES
    nff = dff // fc
    tm = min(512, T) if seq_len is not None else T
    nt = T // tm
    in_specs = [pl.BlockSpec((tm, D), lambda i, j: (i, 0)),
                pl.BlockSpec((1, D), lambda i, j: (0, 0)),
                pl.BlockSpec((D, fc), lambda i, j: (0, j)),
                pl.BlockSpec((D, fc), lambda i, j: (0, nff + j)),
                pl.BlockSpec((CONV_W, fc), lambda i, j: (0, j)),
                pl.BlockSpec((1, fc), lambda i, j: (0, j)),
                pl.BlockSpec((fc, D), lambda i, j: (j, 0)),
                pl.BlockSpec((1, D), lambda i, j: (0, 0))]
    args = [x2, g2, wgu, wgu, cw, cb, wd, gf]
    scratch = [pltpu.VMEM((tm, D), BF16), pltpu.VMEM((tm, D), F32)]
    if seq_len is None:
        in_specs += [pl.BlockSpec((tm, fc), lambda i, j: (i, j)), pl.BlockSpec((tm, fc), lambda i, j: (i, j))]
        args += list(hist)
        gt_spec = pl.BlockSpec((tm, fc), lambda i, j: (i, j))
        gt_shape = jax.ShapeDtypeStruct((T, dff), F32)
    else:
        scratch.append(pltpu.VMEM((nff, 8, fc), F32))
        gt_spec = pl.BlockSpec((1, 8, fc), lambda i, j: (i, 0, j))
        gt_shape = jax.ShapeDtypeStruct((nt, 8, dff), F32)
    return pl.pallas_call(
        functools.partial(_ffn_kernel, seq_len=seq_len, short_len=short_len, final=final),
        grid=(nt, nff),
        in_specs=in_specs,
        out_specs=[pl.BlockSpec((tm, D), lambda i, j: (i, 0)), gt_spec],
        out_shape=[jax.ShapeDtypeStruct((T, D), F32), gt_shape],
        scratch_shapes=scratch,
        compiler_params=_cparams(("arbitrary", "arbitrary")),
        name="ffn",
    )(*args)


def _rows_th(q):
    tq = q.shape[0]
    rows = jnp.concatenate([jnp.broadcast_to(q[t:t + 1, :], (8, q.shape[1])) for t in range(tq)], axis=0)
    keep = _div(_iota(rows.shape, 1), HD) == _mod(_iota(rows.shape, 0), 8)
    return jnp.where(keep, rows * SCALE, 0.0).astype(BF16)


def _diag_rows(o_ref, o32):
    keep = _div(_iota(o32.shape, 1), HD) == _mod(_iota(o32.shape, 0), 8)
    od = jnp.where(keep, o32, 0.0)
    for t in range(o32.shape[0] // 8):
        o_ref[0, t:t + 1, :] = jnp.sum(od[t * 8:(t + 1) * 8, :], axis=0, keepdims=True)


FOX_PAGES = 16


def _fox_sample_kernel(pt_tab, *refs):
    P = FOX_PAGES
    q_ref, kvn_ref, lfn_ref = refs[0:3]
    k_refs = refs[3:3 + P]
    v_refs = refs[3 + P:3 + 2 * P]
    lf_refs = refs[3 + 2 * P:3 + 3 * P]
    o_ref, q_sc, m_ref, l_ref, acc_ref, carry_ref, new_sc = refs[3 + 3 * P:]
    s = pl.program_id(1)
    tq = q_ref.shape[1]
    nr = tq * 8

    @pl.when(s == 0)
    def _():
        q_sc[...] = _rows_th(q_ref[0])
        _softmax_init(m_ref, l_ref, acc_ref)
        carry_ref[...] = jnp.zeros_like(carry_ref)
        new_sc[...] = jnp.zeros_like(new_sc)
        new_sc[0:tq, :] = kvn_ref[0]

    qq = q_sc[...]
    triu = (_iota((PAGE, PAGE), 0) <= _iota((PAGE, PAGE), 1)).astype(BF16)

    def attend(score_fns, value_fns, lfs, extra_mask):
        off = carry_ref[...]
        sc = []
        within = _dot3_l(jnp.concatenate(lfs, axis=0), triu)
        for k, fn in enumerate(score_fns):
            cum = within[H_FOX * k:H_FOX * (k + 1), :] + off
            off = jnp.broadcast_to(cum[:, PAGE - 1:PAGE], cum.shape)
            sc.append(fn() - jnp.concatenate([cum] * tq, axis=0))
        carry_ref[...] = off
        sc = jnp.concatenate(sc, axis=1)
        if extra_mask is not None:
            sc = jnp.where(extra_mask, sc, NEG)
        m_old = m_ref[...]
        m_new = jnp.maximum(m_old, jnp.max(sc, axis=1, keepdims=True))
        alpha = jnp.exp(m_old - m_new)
        p = jnp.exp(sc - m_new)
        l_ref[...] = alpha * l_ref[...] + jnp.sum(p, axis=1, keepdims=True)
        acc = alpha * acc_ref[...]
        for k, fn in enumerate(value_fns):
            acc = acc + fn(p[:, k * PAGE:(k + 1) * PAGE].astype(BF16))
        acc_ref[...] = acc
        m_ref[...] = m_new

    attend([lambda r=r: _dot(qq, r[...].reshape(FOX_W, PAGE).astype(BF16)) for r in k_refs],
           [lambda p, r=r: _dot_nt(p, r[...].reshape(FOX_W, PAGE).astype(BF16)) for r in v_refs],
           [r[...] for r in lf_refs], None)

    @pl.when(s == pl.num_programs(1) - 1)
    def _():
        key = _iota((nr, PAGE), 1)
        ok = (key < tq) & (key <= _div(_iota((nr, PAGE), 0), 8))
        new = new_sc[...]
        attend([lambda: _dot_nt(qq, new[:, :FOX_W].astype(BF16))],
               [lambda p: _dot(p, new[:, FOX_W:].astype(BF16))], [lfn_ref[0]], ok)
        _diag_rows(o_ref, acc_ref[...] / jnp.maximum(l_ref[...], TINY))


def _fox_sample(qf, fkv_new, lfn_t, cache_kt, cache_lft, page_table, layer):
    nb, npages = page_table.shape
    tq = qf.shape[1]
    P = FOX_PAGES
    kvw = 2 * FOX_W
    seq = lambda b, s, pt: (b, 0, 0)

    def kv_spec(k, c):
        return pl.BlockSpec((None, None, None, H_FOX, HD, PAGE),
                            lambda b, s, pt: (layer, pt[b, s * P + k], c, 0, 0, 0))

    def lf_spec(k):
        return pl.BlockSpec((None, None, H_FOX, PAGE), lambda b, s, pt: (layer, pt[b, s * P + k], 0, 0))

    return pl.pallas_call(
        _fox_sample_kernel,
        grid_spec=pltpu.PrefetchScalarGridSpec(
            num_scalar_prefetch=1,
            grid=(nb, npages // P),
            in_specs=[pl.BlockSpec((1, tq, FOX_W), seq), pl.BlockSpec((1, tq, kvw), seq),
                      pl.BlockSpec((1, H_FOX, PAGE), seq)]
            + [kv_spec(k, 0) for k in range(P)] + [kv_spec(k, 1) for k in range(P)]
            + [lf_spec(k) for k in range(P)],
            out_specs=pl.BlockSpec((1, tq, FOX_W), seq),
            scratch_shapes=[pltpu.VMEM((tq * 8, FOX_W), BF16),
                            pltpu.VMEM((tq * 8, 1), F32), pltpu.VMEM((tq * 8, 1), F32),
                            pltpu.VMEM((tq * 8, FOX_W), F32),
                            pltpu.VMEM((H_FOX, PAGE), F32),
                            pltpu.VMEM((PAGE, kvw), F32)]),
        out_shape=jax.ShapeDtypeStruct((nb, tq, FOX_W), F32),
        compiler_params=_cparams(("arbitrary", "arbitrary")),
        name="fox_sample",
    )(page_table, qf, fkv_new, lfn_t, *([cache_kt] * (2 * P)), *([cache_lft] * P))


def _place_wide():
    r = _iota((LANES, NSA_W), 0)
    c = _iota((LANES, NSA_W), 1)
    return jnp.where(r == _div(c, NSA_GH * HD) * HD + _mod(c, HD), 1.0, 0.0).astype(BF16)


def _rb_col(rbt_ref):
    return lambda bb: rbt_ref[:, bb:bb + 1]


def _fold_heads():
    r = _iota((NSA_W, LANES), 0)
    c = _iota((NSA_W, LANES), 1)
    return jnp.where(c == _div(r, NSA_GH * HD) * HD + _mod(r, HD), 1.0, 0.0).astype(BF16)


def _ns_attend_kernel(q_ref, kcw_ref, kwt_ref, vwt_ref, wn_ref, rbt_ref, oc_ref, ow_ref, mb_ref, new_sc, *, past):
    tq = q_ref.shape[1]
    nr = tq * 8
    n = kcw_ref.shape[2] - CMP_PAD_FRONT - CMP_PAD_BACK
    n_selp = past // SEL_LEN
    wb = kwt_ref.shape[2]
    qq = _rows_th(q_ref[0])
    kc = kcw_ref[0, 0, CMP_PAD_FRONT:CMP_PAD_FRONT + n, :]
    vc = kcw_ref[0, 1, CMP_PAD_FRONT:CMP_PAD_FRONT + n, :]
    trow = _div(_iota((nr, n), 0), 8)
    dist = past + trow - CMP_STRIDE * _iota((nr, n), 1) - (CMP_LEN - 1)
    bias = _bias_lookup(_bucket(jnp.clip(dist, 0, MAX_DISTANCE - 1)), _rb_col(rbt_ref))
    ok = dist >= 0
    s = jnp.where(ok, _dot_nt(qq, kc) + bias, NEG)
    m = jnp.max(s, axis=1, keepdims=True)
    p = jnp.where(ok, jnp.exp(s - m), 0.0)
    p = p / jnp.maximum(jnp.sum(p, axis=1, keepdims=True), TINY)
    _diag_rows(oc_ref, _dot(p.astype(BF16), vc))
    mov = jnp.where(_overlap(_iota((n, LANES), 0), _iota((n, LANES), 1)), 1.0, 0.0).astype(BF16)
    x = _dot2_l(p, mov)
    z = x + pltpu.roll(x, nr - 1, 0) + pltpu.roll(x, nr - 2, 0) + pltpu.roll(x, nr - 3, 0)
    z0 = jnp.where(_mod(_iota(z.shape, 0), NSA_GH) == 0, z, 0.0)
    p_slc = z0 + pltpu.roll(z0, 1, 0) + pltpu.roll(z0, 2, 0) + pltpu.roll(z0, 3, 0)
    jb = _iota((nr, LANES), 1)
    forced = (jb == 0) | (jb == n_selp - 1)
    score = jnp.where(forced, FORCE_SCORE, jnp.where(jb < n_selp, p_slc, -1.0))
    k_past = min(N_SELECT, n_selp + 1) - 1
    sel = _topk_mask(score, k_past)
    mb_ref[0] = jnp.where((sel > 0.5) & (score >= 0.0), 0.0, NEG).astype(BF16)
    qf = _dot(qq, _fold_heads()).astype(BF16)
    kwt = kwt_ref[...].reshape(LANES, wb).astype(BF16)
    vwt = vwt_ref[...].reshape(LANES, wb).astype(BF16)
    new_sc[...] = jnp.zeros_like(new_sc)
    new_sc[0:tq, :] = wn_ref[0]
    wnew = new_sc[...]
    dw = _div(_iota((nr, wb), 0), 8) + wb - _iota((nr, wb), 1)
    okw = (dw >= 0) & (dw < WINDOW)
    sw = _dot(qf, kwt) + _bias_lookup(_bucket(jnp.clip(dw, 0, MAX_DISTANCE - 1)), _rb_col(rbt_ref))
    sw = jnp.where(okw, sw, NEG)
    dn = _div(_iota((nr, PAGE), 0), 8) - _iota((nr, PAGE), 1)
    okn = (dn >= 0) & (_iota((nr, PAGE), 1) < tq)
    sn = _dot_nt(qf, wnew[:, 0:LANES].astype(BF16))
    sn = sn + _bias_lookup(_bucket(jnp.clip(dn, 0, MAX_DISTANCE - 1)), _rb_col(rbt_ref))
    sn = jnp.where(okn, sn, NEG)
    mw = jnp.maximum(jnp.max(sw, axis=1, keepdims=True), jnp.max(sn, axis=1, keepdims=True))
    pw = jnp.where(okw, jnp.exp(sw - mw), 0.0)
    pn = jnp.where(okn, jnp.exp(sn - mw), 0.0)
    lw = jnp.maximum(jnp.sum(pw, axis=1, keepdims=True) + jnp.sum(pn, axis=1, keepdims=True), TINY)
    a = (_dot_nt(pw.astype(BF16), vwt) + _dot(pn.astype(BF16), wnew[:, LANES:2 * LANES].astype(BF16))) / lw
    _diag_rows(ow_ref, _dot2_l(a, _place_wide()))


def _ns_attend(qn, kcw, win_t, wkv_new, rbt, layer, past):
    nb, tq, _ = qn.shape
    npad = kcw.shape[2]
    wb = win_t.shape[5]
    seq = lambda b: (b, 0, 0)

    def win_spec(c):
        return pl.BlockSpec((None, None, None, NSA_KV, HD, wb), lambda b: (layer, b, c, 0, 0, 0))

    return pl.pallas_call(
        functools.partial(_ns_attend_kernel, past=past),
        grid=(nb,),
        in_specs=[pl.BlockSpec((1, tq, NSA_W), seq),
                  pl.BlockSpec((1, 2, npad, NSA_W), lambda b: (b, 0, 0, 0)),
                  win_spec(0), win_spec(1),
                  pl.BlockSpec((1, tq, 2 * LANES), seq),
                  pl.BlockSpec((tq * 8, N_BUCKETS), lambda b: (0, 0))],
        out_specs=[pl.BlockSpec((1, tq, NSA_W), seq), pl.BlockSpec((1, tq, NSA_W), seq),
                   pl.BlockSpec((1, tq * 8, LANES), seq)],
        out_shape=[jax.ShapeDtypeStruct((nb, tq, NSA_W), F32), jax.ShapeDtypeStruct((nb, tq, NSA_W), F32),
                   jax.ShapeDtypeStruct((nb, tq * 8, LANES), BF16)],
        scratch_shapes=[pltpu.VMEM((PAGE, 2 * LANES), F32)],
        compiler_params=_cparams(("arbitrary",)),
        name="ns_attend",
    )(qn, kcw, win_t, win_t, wkv_new, rbt)


def _ns_select_kernel(pt_tab, *refs, past):
    P = PAGES_PER_STEP
    q_ref, mb_ref, nn_ref, rbt_ref = refs[0:4]
    k_refs = refs[4:4 + P]
    v_refs = refs[4 + P:4 + 2 * P]
    o_ref, q_sc, m_ref, l_ref, acc_ref, new_sc = refs[4 + 2 * P:]
    s = pl.program_id(1)
    tq = q_ref.shape[1]
    nr = tq * 8

    @pl.when(s == 0)
    def _():
        q_sc[...] = _dot(_rows_th(q_ref[0]), _fold_heads()).astype(BF16)
        _softmax_init(m_ref, l_ref, acc_ref)
        new_sc[...] = jnp.zeros_like(new_sc)
        new_sc[0:tq, :] = nn_ref[0][:, 2 * LANES:]

    qq = q_sc[...]
    trow = _div(_iota((nr, PAGE), 0), 8)
    key = _iota((nr, PAGE), 1)

    def attend(score_fns, value_fns, biases):
        sc = jnp.concatenate([fn() for fn in score_fns], axis=1) + biases
        m_old = m_ref[...]
        m_new = jnp.maximum(m_old, jnp.max(sc, axis=1, keepdims=True))
        alpha = jnp.exp(m_old - m_new)
        p = jnp.exp(sc - m_new)
        l_ref[...] = alpha * l_ref[...] + jnp.sum(p, axis=1, keepdims=True)
        acc = alpha * acc_ref[...]
        for k, fn in enumerate(value_fns):
            acc = acc + fn(p[:, k * PAGE:(k + 1) * PAGE].astype(BF16))
        acc_ref[...] = acc
        m_ref[...] = m_new

    blk = _iota((LANES, P * PAGE), 0)
    kcol = _iota((LANES, P * PAGE), 1)
    expand = jnp.where(blk == _div(s * (P * PAGE) + kcol, SEL_LEN), 1.0, 0.0).astype(BF16)
    bias = _dot(mb_ref[0], expand) + rbt_ref[:, N_BUCKETS - 1:N_BUCKETS]
    tiles = ([lambda r=r: _dot(qq, r[...].reshape(LANES, PAGE).astype(BF16)) for r in k_refs],
             [lambda p, r=r: _dot_nt(p, r[...].reshape(LANES, PAGE).astype(BF16)) for r in v_refs])

    @pl.when(s < pl.num_programs(1) - 1)
    def _():
        attend(*tiles, bias)

    @pl.when(s == pl.num_programs(1) - 1)
    def _():
        d_last = past + trow - (past - PAGE + key)
        b_last = _bias_lookup(_bucket(jnp.clip(d_last, 0, MAX_DISTANCE - 1)), _rb_col(rbt_ref))
        fix = jnp.concatenate([jnp.zeros((nr, (P - 1) * PAGE), F32),
                               b_last - rbt_ref[:, N_BUCKETS - 1:N_BUCKETS]], axis=1)
        attend(*tiles, bias + fix)
        d_new = trow - key
        b_new = _bias_lookup(_bucket(jnp.clip(d_new, 0, MAX_DISTANCE - 1)), _rb_col(rbt_ref))
        new = new_sc[...]
        attend([lambda: _dot_nt(qq, new[:, 0:LANES].astype(BF16))],
               [lambda p: _dot(p, new[:, LANES:2 * LANES].astype(BF16))],
               jnp.where((d_new >= 0) & (key < tq), b_new, NEG))
        a = acc_ref[...] / jnp.maximum(l_ref[...], TINY)
        place = _place_wide()
        _diag_rows(o_ref, _dot2_l(a, place))


def _ns_select(qn, maskb, nkv_new, rbt, cache_nsa, page_table, layer, past):
    nb, npages = page_table.shape
    tq = qn.shape[1]
    P = PAGES_PER_STEP
    seq = lambda b, s, pt: (b, 0, 0)

    def page_spec(k, c):
        return pl.BlockSpec((None, None, None, NSA_KV, HD, PAGE),
                            lambda b, s, pt: (layer, pt[b, s * P + k], c, 0, 0, 0))

    return pl.pallas_call(
        functools.partial(_ns_select_kernel, past=past),
        grid_spec=pltpu.PrefetchScalarGridSpec(
            num_scalar_prefetch=1,
            grid=(nb, npages // P),
            in_specs=[pl.BlockSpec((1, tq, NSA_W), seq), pl.BlockSpec((1, tq * 8, LANES), seq),
                      pl.BlockSpec((1, tq, 4 * LANES), seq),
                      pl.BlockSpec((tq * 8, N_BUCKETS), lambda b, s, pt: (0, 0))]
            + [page_spec(k, 2) for k in range(P)] + [page_spec(k, 3) for k in range(P)],
            out_specs=pl.BlockSpec((1, tq, NSA_W), seq),
            scratch_shapes=[pltpu.VMEM((tq * 8, LANES), BF16),
                            pltpu.VMEM((tq * 8, 1), F32), pltpu.VMEM((tq * 8, 1), F32),
                            pltpu.VMEM((tq * 8, LANES), F32),
                            pltpu.VMEM((PAGE, 2 * LANES), F32)]),
        out_shape=jax.ShapeDtypeStruct((nb, tq, NSA_W), F32),
        compiler_params=_cparams(("arbitrary", "arbitrary")),
        name="ns_select",
    )(page_table, qn, maskb, nkv_new, rbt, *([cache_nsa] * (2 * P)))


def _prep_w_in(w):
    d = w.shape[0]
    o_logf = 3 * FOX_W
    o_qn = o_logf + H_FOX
    o_kv = o_qn + NSA_W
    o_gate = o_kv + 6 * NSA_KV * HD
    misc = jnp.concatenate([w[:, o_logf:o_qn], w[:, o_gate:o_gate + 3 * H_NSA],
                            jnp.zeros((d, LANES - H_FOX - 3 * H_NSA), w.dtype)], axis=1)
    return jnp.concatenate([w[:, :o_logf], w[:, o_qn:o_kv], w[:, o_kv:o_gate], misc], axis=1).astype(BF16)


def _prep_cmp(pos, w1, b1, w2):
    eye = jnp.eye(NSA_KV, dtype=w1.dtype)
    w1r = w1.reshape(2, 2, CMP_STRIDE, HD, CMP_HID)
    wide = jnp.einsum("whjdc,ab->whjadbc", w1r, eye).reshape(2, 2, CMP_STRIDE * LANES, 2 * CMP_HID)
    posr = jnp.broadcast_to(pos.reshape(2, 2, CMP_STRIDE, 1, HD), (2, 2, CMP_STRIDE, NSA_KV, HD))
    posr = posr.reshape(2, 2, 1, CMP_STRIDE * LANES)
    group_of_head = (jnp.arange(H_NSA) // NSA_GH)[None, :] == jnp.arange(NSA_KV)[:, None]
    w2w = jnp.einsum("wcd,ah->wachd", w2, group_of_head.astype(w2.dtype)).reshape(2, 2 * CMP_HID, NSA_W)
    return {"pt": posr[:, 0], "pb": posr[:, 1],
            "w1t": wide[:, 0].astype(BF16), "w1b": wide[:, 1].astype(BF16),
            "b1": jnp.concatenate([b1, b1], axis=-1)[:, None, :],
            "w2w": w2w.astype(BF16)}


def _far_bias_rows(rel_bias):
    far = rel_bias[N_BUCKETS - 1]
    hi = far.astype(BF16).astype(F32)
    r = far - hi
    mid = r.astype(BF16).astype(F32)
    lo = (r - mid).astype(BF16).astype(F32)
    rows = jnp.zeros((H_NSA, LANES), F32)
    rows = rows.at[:, HD].set(hi).at[:, HD + 1].set(mid).at[:, HD + 2].set(lo)
    return rows.reshape(NSA_KV, NSA_GH, LANES)


def kernel(x_prompt, x_sample, cache_fox_kv, cache_fox_logf, cache_nsa_kv, state_win_kv, state_conv,
           page_table, norm1_g, w_in, b_forget, cmp_pos, cmp_w1, cmp_b1, cmp_w2, out_norm_g, w_out,
           norm2_g, w_gu, conv_w, conv_b, w_down, rel_bias, final_norm_g):
    B, S, D = x_prompt.shape
    nb, tq, _ = x_sample.shape
    depth = w_in.shape[0]
    n_pool = cache_fox_kv.shape[1]
    npages = page_table.shape[1]
    past = npages * PAGE
    dff = w_down.shape[1]
    wb = state_win_kv.shape[2]
    assert tq & (tq - 1) == 0 and tq >= CONV_W - 1
    assert S % 256 == 0 and S // SEL_LEN <= LANES and past // SEL_LEN <= LANES
    assert npages % PAGES_PER_STEP == 0 and wb == WINDOW and past >= WINDOW and tq * 8 <= LANES

    assert npages % FOX_PAGES == 0 and PAGE == LANES
    cache_kt = jnp.transpose(cache_fox_kv, (0, 1, 3, 4, 5, 2))
    cache_lft = jnp.swapaxes(cache_fox_logf, 2, 3)
    cache_nsa = jnp.transpose(cache_nsa_kv, (0, 1, 3, 4, 5, 2))
    win_t = jnp.transpose(state_win_kv, (0, 1, 3, 4, 5, 2))
    rbt = jnp.tile(rel_bias.T, (tq, 1))
    brow = _far_bias_rows(rel_bias * LOG2E)
    gf = final_norm_g.reshape(1, D)

    xp = x_prompt.reshape(B * S, D)
    xs = x_sample.reshape(nb * tq, D)
    outs = [[] for _ in range(10)]
    for l in range(depth):
        w_l = _prep_w_in(w_in[l])
        bf = jnp.zeros((1, LANES), F32).at[0, :H_FOX].set(b_forget[l])
        cw = _prep_cmp(cmp_pos[l], cmp_w1[l], cmp_b1[l], cmp_w2[l])
        g1 = norm1_g[l].reshape(1, D)
        g2 = norm2_g[l].reshape(1, D)
        go = out_norm_g[l].reshape(1, D)
        wo = w_out[l].astype(BF16)
        wgu = w_gu[l].astype(BF16)
        wd = w_down[l].astype(BF16)
        cb = conv_b[l].reshape(1, dff)
        final = l == depth - 1

        qf, fkv, qn, nkv, wkv, misc = _proj(xp, g1, w_l, bf)
        r3 = lambda a: a.reshape(B, S, a.shape[-1])
        qa, ka, vb = _fox_prep(r3(qf), r3(fkv), r3(misc))
        vt = jnp.swapaxes(vb, 1, 2).reshape(B, H_FOX, HD, S)
        vt = jnp.concatenate([vt, jnp.ones((B, H_FOX, ONES_ROWS, S), BF16)], axis=2)
        o_fox = _fox_flash(qa, ka, vt)
        kcw = _compress_prompt(r3(nkv), cw)
        ksa, kwa = _nsa_prep(r3(nkv), r3(wkv))
        vst = r3(nkv)[:, :, 3 * LANES:].astype(BF16).reshape(B, S, NSA_KV, HD).transpose(0, 2, 3, 1)
        vst = jnp.concatenate([vst, jnp.ones((B, NSA_KV, ONES_ROWS, S), BF16)], axis=2)
        o_c, maskq = _nsa_cmp(rel_bias, r3(qn), kcw)
        o_s = _nsa_sel(rel_bias, r3(qn), maskq, brow, ksa, vst)
        vwt = r3(wkv)[:, :, LANES:].astype(BF16).reshape(B, S, NSA_KV, HD).transpose(0, 2, 3, 1)
        vwt = jnp.pad(vwt, ((0, 0), (0, 0), (0, 0), (WINDOW, 0)))
        vwt = jnp.concatenate([vwt, jnp.ones((B, NSA_KV, ONES_ROWS, S + WINDOW), BF16)], axis=2)
        o_w = _nsa_win(rel_bias, r3(qn), jnp.pad(kwa, ((0, 0), (0, 0), (WINDOW, 0), (0, 0))), vwt)
        f2 = lambda a: a.reshape(B * S, a.shape[-1])
        xp = _merge(xp, f2(o_fox), f2(o_c), f2(o_s), f2(o_w), misc, go, wo)
        xp, gtail = _ffn(xp, g2, wgu, conv_w[l], cb, wd, gf, seq_len=S, final=final)
        tiles_per_seq = gtail.shape[0] // B
        conv_p = gtail.reshape(B, tiles_per_seq, 8, dff)[:, -1, 8 - (CONV_W - 1):, :]
        outs[0].append(fkv.reshape(B, S, 2, H_FOX, HD))
        outs[2].append(misc[:, :H_FOX].reshape(B, S, H_FOX))
        outs[4].append(nkv.reshape(B, S, 4, NSA_KV, HD))
        outs[6].append(wkv.reshape(B, S, 2, NSA_KV, HD)[:, -min(WINDOW, S):])
        outs[8].append(conv_p)

        qf, fkv, qn, nkv, wkv, misc = _proj(xs, g1, w_l, bf)
        s3 = lambda a: a.reshape(nb, tq, a.shape[-1])
        lfn = jnp.swapaxes(s3(misc)[:, :, :H_FOX], 1, 2)
        lfn = jnp.pad(lfn, ((0, 0), (0, 0), (0, PAGE - tq)))
        o_fox = _fox_sample(s3(qf), s3(fkv), lfn, cache_kt, cache_lft, page_table, l)
        kcw = _compress_sample(cache_nsa, page_table, l, cw)
        o_c, o_w, maskb = _ns_attend(s3(qn), kcw, win_t, s3(wkv), rbt, l, past)
        o_s = _ns_select(s3(qn), maskb, s3(nkv), rbt, cache_nsa, page_table, l, past)
        s2 = lambda a: a.reshape(nb * tq, a.shape[-1])
        xs = _merge(xs, s2(o_fox), s2(o_c), s2(o_s), s2(o_w), misc, go, wo)
        hist = state_conv[l]
        zero = jnp.zeros((nb, 1, dff), F32)
        hm1 = jnp.concatenate([hist[:, 1:2], zero, zero, zero][:tq], axis=1).reshape(nb * tq, dff)
        hm2 = jnp.concatenate([hist[:, 0:1], hist[:, 1:2], zero, zero][:tq], axis=1).reshape(nb * tq, dff)
        xs, gfull = _ffn(xs, g2, wgu, conv_w[l], cb, wd, gf, seq_len=None, final=final, hist=(hm1, hm2),
                         short_len=tq)
        win_new = jnp.concatenate([state_win_kv[l], wkv.reshape(nb, tq, 2, NSA_KV, HD)], axis=1)[:, -wb:]
        outs[1].append(fkv.reshape(nb, tq, 2, H_FOX, HD))
        outs[3].append(misc[:, :H_FOX].reshape(nb, tq, H_FOX))
        outs[5].append(nkv.reshape(nb, tq, 4, NSA_KV, HD))
        outs[7].append(win_new)
        outs[9].append(gfull.reshape(nb, tq, dff)[:, -(CONV_W - 1):])

    st = [jnp.stack(o) for o in outs]
    return (xp.reshape(B, S, D), xs.reshape(nb, tq, D),
            st[0], st[1], st[2], st[3], st[4], st[5], st[6], st[7], st[8], st[9])
```

```python
import functools
import math

import numpy as np
import jax
import jax.numpy as jnp
from jax import lax
from jax.experimental import pallas as pl
from jax.experimental.pallas import tpu as pltpu

F32 = jnp.float32
BF16 = jnp.bfloat16
I32 = jnp.int32

HD = 64
H_FOX = 8
H_NSA = 8
NSA_KV = 2
NSA_GH = 4
FOX_W = H_FOX * HD
NSA_W = H_NSA * HD
CMP_LEN = 32
CMP_STRIDE = 16
CMP_HID = 256
SEL_LEN = 64
N_SELECT = 16
WINDOW = 512
N_BUCKETS = 32
MAX_DISTANCE = 128
CONV_W = 3
PAGE = 128
EPS = 1e-6
NEG = -1e30
TINY = 1e-30
FORCE_SCORE = 1e4
SCALE = HD ** -0.5
LOG2E = math.log2(math.e)
ONES_ROWS = 8
LANES = 128
VMEM_LIMIT = 56 * 1024 * 1024

C_QF, C_FKV, C_QN, C_NKV, C_WKV, C_MISC, C_END = 0, 512, 1536, 2048, 2560, 2816, 2944
MISC_LOGF = 0
MISC_GATE = 8

AUG0 = HD
AUG1 = HD + 3

PAGES_PER_STEP = 32


def _bucket_thresholds():
    exact = N_BUCKETS // 2
    n = np.arange(1, 4 * MAX_DISTANCE, dtype=np.float64)
    far = exact + (np.log(n / exact) / math.log(MAX_DISTANCE / exact) * (N_BUCKETS - exact)).astype(np.int64)
    b = np.where(n < exact, n, np.minimum(far, N_BUCKETS - 1)).astype(np.int64)
    return [int(n[b >= k].min()) for k in range(exact + 1, N_BUCKETS)]


_THR = _bucket_thresholds()


def _cparams(sem):
    return pltpu.CompilerParams(dimension_semantics=sem, vmem_limit_bytes=VMEM_LIMIT)


def _dot(a, b):
    return jnp.dot(a, b, preferred_element_type=F32)


def _dot_nt(a, b):
    return lax.dot_general(a, b, (((1,), (1,)), ((), ())), preferred_element_type=F32)


def _split2(x):
    hi = x.astype(BF16)
    lo = (x - hi.astype(F32)).astype(BF16)
    return hi, lo


def _split3(x):
    hi = x.astype(BF16)
    r = x - hi.astype(F32)
    mid = r.astype(BF16)
    lo = (r - mid.astype(F32)).astype(BF16)
    return hi, mid, lo


def _dot3_l(x, m):
    hi, mid, lo = _split3(x)
    return _dot(hi, m) + _dot(mid, m) + _dot(lo, m)


def _dot3_r(m, x):
    hi, mid, lo = _split3(x)
    return _dot(m, hi) + _dot(m, mid) + _dot(m, lo)


def _dot2_l(x, m):
    hi, lo = _split2(x)
    return _dot(hi, m) + _dot(lo, m)


def _iota(shape, dim):
    return lax.broadcasted_iota(I32, shape, dim)


def _div(x, k):
    return lax.shift_right_arithmetic(x, jnp.int32(k.bit_length() - 1))


def _mod(x, k):
    return x & (k - 1)


def _rms(x, g):
    r = lax.rsqrt(jnp.mean(x * x, axis=-1, keepdims=True) + EPS)
    return (x * r) * g


def _bucket(d):
    far = jnp.full(d.shape, N_BUCKETS // 2, I32)
    for thr in _THR:
        far = far + (d >= thr).astype(I32)
    return jnp.where(d < N_BUCKETS // 2, d, far)


def _bias_lookup(bucket, rb_get):
    acc = jnp.zeros(bucket.shape, F32)
    for b in range(N_BUCKETS):
        acc = jnp.where(bucket == b, rb_get(b), acc)
    return acc


def _topk_mask(score, k, axis=1):
    idx = _iota(score.shape, axis).astype(F32)

    def body(_, c):
        work, sel = c
        mx = jnp.max(work, axis=axis, keepdims=True)
        first = jnp.min(jnp.where(work == mx, idx, 1e9), axis=axis, keepdims=True)
        hit = idx == first
        return jnp.where(hit, -3.0, work), jnp.where(hit, 1.0, sel)

    _, sel = lax.fori_loop(0, k, body, (score, jnp.zeros(score.shape, F32)))
    return sel


def _proj_kernel(x_ref, g_ref, w_ref, bf_ref, qf_ref, fkv_ref, qn_ref, nkv_ref, wkv_ref, misc_ref):
    h = _rms(x_ref[...], g_ref[...]).astype(BF16)
    qf_ref[...] = _dot(h, w_ref[:, C_QF:C_FKV])
    fkv_ref[...] = _dot(h, w_ref[:, C_FKV:C_QN])
    qn_ref[...] = _dot(h, w_ref[:, C_QN:C_NKV])
    nkv_ref[...] = _dot(h, w_ref[:, C_NKV:C_WKV])
    wkv_ref[...] = _dot(h, w_ref[:, C_WKV:C_MISC])
    z = _dot(h, w_ref[:, C_MISC:C_END]) + bf_ref[...]
    lane = _iota(z.shape, 1)
    logsig = jnp.minimum(z, 0.0) - jnp.log(1.0 + jnp.exp(-jnp.abs(z)))
    sig = 1.0 / (1.0 + jnp.exp(-z))
    misc_ref[...] = jnp.where(lane < MISC_GATE, logsig, jnp.where(lane < MISC_GATE + 3 * H_NSA, sig, 0.0))


def _proj(x2, g, w, bf):
    T, D = x2.shape
    tm = min(512, T)
    widths = (C_FKV - C_QF, C_QN - C_FKV, C_NKV - C_QN, C_WKV - C_NKV, C_MISC - C_WKV, C_END - C_MISC)
    return pl.pallas_call(
        _proj_kernel,
        grid=(T // tm,),
        in_specs=[pl.BlockSpec((tm, D), lambda i: (i, 0)),
                  pl.BlockSpec((1, D), lambda i: (0, 0)),
                  pl.BlockSpec((D, C_END), lambda i: (0, 0)),
                  pl.BlockSpec((1, LANES), lambda i: (0, 0))],
        out_specs=[pl.BlockSpec((tm, wd), lambda i: (i, 0)) for wd in widths],
        out_shape=[jax.ShapeDtypeStruct((T, wd), F32) for wd in widths],
        compiler_params=_cparams(("arbitrary",)),
        name="proj",
    )(x2, g, w, bf)


def _fox_prep_kernel(qf_ref, k_ref, v_ref, misc_ref, qa_ref, ka_ref, vb_ref, carry_ref):
    i = pl.program_id(1)
    tm = qf_ref.shape[1]

    @pl.when(i == 0)
    def _():
        carry_ref[...] = jnp.zeros_like(carry_ref)

    lane = _iota((tm, LANES), 1)
    lf = jnp.where(lane < H_FOX, misc_ref[0], 0.0)
    tril = (_iota((tm, tm), 0) >= _iota((tm, tm), 1)).astype(BF16)
    cum = _dot3_r(tril, lf) + carry_ref[0:1, :]
    carry_ref[...] = jnp.broadcast_to(cum[tm - 1:tm, :], carry_ref.shape)
    cl2 = cum * LOG2E
    hi = cl2.astype(BF16).astype(F32)
    r = cl2 - hi
    mid = r.astype(BF16).astype(F32)
    lo = (r - mid).astype(BF16).astype(F32)
    cc = (hi + pltpu.roll(mid, H_FOX, 1) + pltpu.roll(lo, 2 * H_FOX, 1)).astype(BF16)
    er = _iota((LANES, H_FOX * LANES), 0)
    ec = _iota((LANES, H_FOX * LANES), 1)
    part = _div(er, H_FOX)
    head = _mod(er, H_FOX)
    inb = er < 3 * H_FOX
    eq = jnp.where(inb & (ec == head * LANES + AUG0 + part), 1.0, 0.0).astype(BF16)
    ek = jnp.where(inb & (ec == head * LANES + AUG1 + part), -1.0, 0.0).astype(BF16)
    cl = _mod(_iota((1, H_FOX * LANES), 1), LANES)
    ones_q = jnp.where((cl >= AUG1) & (cl < AUG1 + 3), 1.0, 0.0)
    ones_k = jnp.where((cl >= AUG0) & (cl < AUG0 + 3), 1.0, 0.0)
    cols_q = _dot(cc, eq) + ones_q
    cols_k = _dot(cc, ek) + ones_k
    q = qf_ref[0]
    k = k_ref[0]
    for h in range(H_FOX):
        a = (h // 2) * LANES
        qt = q[:, a:a + LANES]
        kt = k[:, a:a + LANES]
        if h % 2:
            qt = pltpu.roll(qt, HD, 1)
            kt = pltpu.roll(kt, HD, 1)
        qa_ref[0, h] = jnp.where(lane < HD, qt * (SCALE * LOG2E), cols_q[:, h * LANES:(h + 1) * LANES]).astype(BF16)
        ka_ref[0, h] = jnp.where(lane < HD, kt, cols_k[:, h * LANES:(h + 1) * LANES]).astype(BF16)
    vb_ref[0] = v_ref[0].astype(BF16)


def _fox_prep(qf, fkv, misc):
    B, S, _ = qf.shape
    tm = min(512, S)
    return pl.pallas_call(
        _fox_prep_kernel,
        grid=(B, S // tm),
        in_specs=[pl.BlockSpec((1, tm, FOX_W), lambda b, i: (b, i, 0)),
                  pl.BlockSpec((1, tm, FOX_W), lambda b, i: (b, i, 0)),
                  pl.BlockSpec((1, tm, FOX_W), lambda b, i: (b, i, 1)),
                  pl.BlockSpec((1, tm, LANES), lambda b, i: (b, i, 0))],
        out_specs=[pl.BlockSpec((1, H_FOX, tm, LANES), lambda b, i: (b, 0, i, 0)),
                   pl.BlockSpec((1, H_FOX, tm, LANES), lambda b, i: (b, 0, i, 0)),
                   pl.BlockSpec((1, tm, FOX_W), lambda b, i: (b, i, 0))],
        out_shape=[jax.ShapeDtypeStruct((B, H_FOX, S, LANES), BF16),
                   jax.ShapeDtypeStruct((B, H_FOX, S, LANES), BF16),
                   jax.ShapeDtypeStruct((B, S, FOX_W), BF16)],
        scratch_shapes=[pltpu.VMEM((8, LANES), F32)],
        compiler_params=_cparams(("arbitrary", "arbitrary")),
        name="fox_prep",
    )(qf, fkv, fkv, misc)


def _softmax_init(m_ref, l_ref, acc_ref):
    m_ref[...] = jnp.full(m_ref.shape, NEG, F32)
    l_ref[...] = jnp.zeros(l_ref.shape, F32)
    acc_ref[...] = jnp.zeros(acc_ref.shape, F32)


def _softmax_update_t(s, vt, m_ref, acc_ref):
    m_old = m_ref[...]
    m_new = jnp.maximum(m_old, jnp.max(s, axis=0, keepdims=True))
    alpha = jnp.exp2(m_old - m_new)
    p = jnp.exp2(s - m_new).astype(BF16)
    acc_ref[...] = alpha * acc_ref[...] + _dot(vt, p)
    m_ref[...] = m_new


def _softmax_init_t(m_ref, acc_ref):
    m_ref[...] = jnp.full(m_ref.shape, NEG, F32)
    acc_ref[...] = jnp.zeros(acc_ref.shape, F32)


FOX_QUERY_TILE = 1024
FOX_KEY_TILE = 1024


def _fox_flash_kernel(qa_ref, ka_ref, vt_ref, o_ref, m0_ref, m1_ref, acc0_ref, acc1_ref):
    qi = pl.program_id(2)
    tq = qa_ref.shape[2]
    m_ref = (m0_ref, m1_ref)
    acc_ref = (acc0_ref, acc1_ref)
    for hh in range(2):
        _softmax_init_t(m_ref[hh], acc_ref[hh])

    tk = min(FOX_KEY_TILE, tq)
    nk = tq // tk

    def tile(kt, diag_block):
        ks = pl.multiple_of(kt * tk, tk)
        for hh in range(2):
            s = _dot_nt(ka_ref[0, hh, pl.ds(ks, tk), :], qa_ref[0, hh])
            if diag_block is not None:
                s = jnp.where(_iota(s.shape, 0) + diag_block * tk <= _iota(s.shape, 1), s, NEG)
            _softmax_update_t(s, vt_ref[0, hh, :, pl.ds(ks, tk)], m_ref[hh], acc_ref[hh])

    def full_tile(kt, c):
        tile(kt, None)
        return c

    lax.fori_loop(0, qi * nk, full_tile, 0)
    for kb in range(nk):
        tile(qi * nk + kb, kb)
    halves = []
    for hh in range(2):
        a = acc_ref[hh][...]
        halves.append(a[0:HD, :] / jnp.maximum(a[HD:HD + 1, :], TINY))
    o_ref[0] = jnp.concatenate(halves, axis=0).T


def _fox_flash(qa, ka, vt):
    B, H, S, _ = qa.shape
    tq = min(FOX_QUERY_TILE, S)
    rows = vt.shape[2]
    return pl.pallas_call(
        _fox_flash_kernel,
        grid=(B, H // 2, S // tq),
        in_specs=[pl.BlockSpec((1, 2, tq, LANES), lambda b, p, i: (b, p, i, 0)),
                  pl.BlockSpec((1, 2, S, LANES), lambda b, p, i: (b, p, 0, 0)),
                  pl.BlockSpec((1, 2, rows, S), lambda b, p, i: (b, p, 0, 0))],
        out_specs=pl.BlockSpec((1, tq, LANES), lambda b, p, i: (b, i, p)),
        out_shape=jax.ShapeDtypeStruct((B, S, FOX_W), F32),
        scratch_shapes=[pltpu.VMEM((1, tq), F32), pltpu.VMEM((1, tq), F32),
                        pltpu.VMEM((rows, tq), F32), pltpu.VMEM((rows, tq), F32)],
        compiler_params=_cparams(("arbitrary", "arbitrary", "arbitrary")),
        name="fox_flash",
    )(qa, ka, vt)


def _gelu_tanh(x):
    return 0.5 * x * (1.0 + jnp.tanh(math.sqrt(2.0 / math.pi) * (x + 0.044715 * (x * x * x))))


def _compress_core(x, pt, pb, w1t, w1b, b1, w2w):
    n = x.shape[0]
    a = _dot((x + pt).astype(BF16), w1t)
    b = _dot((x + pb).astype(BF16), w1b)
    h = a + pltpu.roll(b, n - 1, 0) + b1
    return _dot(_gelu_tanh(h).astype(BF16), w2w)


CMP_PAD_FRONT = 16
CMP_PAD_BACK = 112


def _store_cmp(out_ref, idx, res, n):
    out_ref[idx] = jnp.zeros(out_ref.shape[len(idx):], BF16)
    out_ref[idx + (slice(CMP_PAD_FRONT, CMP_PAD_FRONT + n), slice(None))] = res.astype(BF16)


def _compress_prompt_kernel(x_ref, pt_ref, pb_ref, w1t_ref, w1b_ref, b1_ref, w2w_ref, out_ref):
    n = x_ref.shape[1] // CMP_STRIDE
    x = jnp.concatenate([x_ref[0, pl.ds(j, n, stride=CMP_STRIDE), :] for j in range(CMP_STRIDE)], axis=1)
    res = _compress_core(x, pt_ref[0], pb_ref[0], w1t_ref[0], w1b_ref[0], b1_ref[0], w2w_ref[0])
    _store_cmp(out_ref, (0, 0), res, n)


def _compress_prompt(nkv, cw):
    B, S, _ = nkv.shape
    n = S // CMP_STRIDE
    npad = n + CMP_PAD_FRONT + CMP_PAD_BACK
    kx = CMP_STRIDE * LANES
    return pl.pallas_call(
        _compress_prompt_kernel,
        grid=(B, 2),
        in_specs=[pl.BlockSpec((1, S, LANES), lambda b, w: (b, 0, w)),
                  pl.BlockSpec((1, 1, kx), lambda b, w: (w, 0, 0)),
                  pl.BlockSpec((1, 1, kx), lambda b, w: (w, 0, 0)),
                  pl.BlockSpec((1, kx, 2 * CMP_HID), lambda b, w: (w, 0, 0)),
                  pl.BlockSpec((1, kx, 2 * CMP_HID), lambda b, w: (w, 0, 0)),
                  pl.BlockSpec((1, 1, 2 * CMP_HID), lambda b, w: (w, 0, 0)),
                  pl.BlockSpec((1, 2 * CMP_HID, NSA_W), lambda b, w: (w, 0, 0))],
        out_specs=pl.BlockSpec((1, 1, npad, NSA_W), lambda b, w: (b, w, 0, 0)),
        out_shape=jax.ShapeDtypeStruct((B, 2, npad, NSA_W), BF16),
        compiler_params=_cparams(("arbitrary", "arbitrary")),
        name="compress_prompt",
    )(nkv, cw["pt"], cw["pb"], cw["w1t"], cw["w1b"], cw["b1"], cw["w2w"])


def _compress_sample_kernel(pt_tab, *refs):
    P = PAGES_PER_STEP
    pages = (refs[:P], refs[P:2 * P])
    pt_ref, pb_ref, w1t_ref, w1b_ref, b1_ref, w2w_ref, out_ref, x_sc, rows_sc = refs[2 * P:]
    s = pl.program_id(1)
    n = x_sc.shape[1]
    rows = P * (PAGE // CMP_STRIDE)
    r0 = pl.multiple_of(s * rows, rows)
    for w in range(2):
        for k, pg in enumerate(pages[w]):
            rows_sc[k * PAGE:(k + 1) * PAGE, :] = pg[...].reshape(LANES, PAGE).T
        for j in range(CMP_STRIDE):
            x_sc[w, pl.ds(r0, rows), j * LANES:(j + 1) * LANES] = rows_sc[pl.ds(j, rows, stride=CMP_STRIDE), :]

    @pl.when(s == pl.num_programs(1) - 1)
    def _():
        for w in range(2):
            res = _compress_core(x_sc[w], pt_ref[w], pb_ref[w], w1t_ref[w], w1b_ref[w], b1_ref[w], w2w_ref[w])
            _store_cmp(out_ref, (0, w), res, n)


def _compress_sample(cache_nsa, page_table, layer, cw):
    nb, npages = page_table.shape
    P = PAGES_PER_STEP
    n = npages * (PAGE // CMP_STRIDE)
    npad = n + CMP_PAD_FRONT + CMP_PAD_BACK
    kx = CMP_STRIDE * LANES

    def page_spec(k, w):
        return pl.BlockSpec((None, None, None, NSA_KV, HD, PAGE),
                            lambda b, s, pt: (layer, pt[b, s * P + k], w, 0, 0, 0))

    const3 = lambda b, s, pt: (0, 0, 0)
    return pl.pallas_call(
        _compress_sample_kernel,
        grid_spec=pltpu.PrefetchScalarGridSpec(
            num_scalar_prefetch=1,
            grid=(nb, npages // P),
            in_specs=[page_spec(k, 0) for k in range(P)] + [page_spec(k, 1) for k in range(P)] + [
                pl.BlockSpec((2, 1, kx), const3), pl.BlockSpec((2, 1, kx), const3),
                pl.BlockSpec((2, kx, 2 * CMP_HID), const3), pl.BlockSpec((2, kx, 2 * CMP_HID), const3),
                pl.BlockSpec((2, 1, 2 * CMP_HID), const3), pl.BlockSpec((2, 2 * CMP_HID, NSA_W), const3)],
            out_specs=pl.BlockSpec((1, 2, npad, NSA_W), lambda b, s, pt: (b, 0, 0, 0)),
            scratch_shapes=[pltpu.VMEM((2, n, kx), F32), pltpu.VMEM((P * PAGE, LANES), F32)]),
        out_shape=jax.ShapeDtypeStruct((nb, 2, npad, NSA_W), BF16),
        compiler_params=_cparams(("arbitrary", "arbitrary")),
        name="compress_sample",
    )(page_table, *([cache_nsa] * (2 * P)), cw["pt"], cw["pb"], cw["w1t"], cw["w1b"], cw["b1"], cw["w2w"])


def _nsa_prep_kernel(sel_ref, win_ref, ksa_ref, kwa_ref):
    i = pl.program_id(1)
    tm = sel_ref.shape[1]
    lane = _iota((tm, LANES), 1)
    pos = i * tm + _iota((tm, LANES), 0)
    onehot = jnp.where(_div(pos, SEL_LEN) == lane, 1.0, 0.0)
    ks = sel_ref[0][:, 0:LANES]
    kw = win_ref[0][:, 0:LANES]
    rks = pltpu.roll(ks, HD, 1)
    rkw = pltpu.roll(kw, HD, 1)
    for g in range(NSA_KV):
        left = jnp.where(lane < HD, ks if g == 0 else rks, jnp.where(lane < HD + 3, 1.0, 0.0))
        ksa_ref[0, g] = jnp.concatenate([left, onehot], axis=1).astype(BF16)
        kwa_ref[0, g] = jnp.where(lane < HD, kw if g == 0 else rkw, 0.0).astype(BF16)


def _nsa_prep(nkv, wkv):
    B, S, _ = nkv.shape
    tm = min(512, S)
    spec128 = pl.BlockSpec((1, NSA_KV, tm, LANES), lambda b, i: (b, 0, i, 0))
    return pl.pallas_call(
        _nsa_prep_kernel,
        grid=(B, S // tm),
        in_specs=[pl.BlockSpec((1, tm, 2 * LANES), lambda b, i: (b, i, 1)),
                  pl.BlockSpec((1, tm, 2 * LANES), lambda b, i: (b, i, 0))],
        out_specs=[pl.BlockSpec((1, NSA_KV, tm, 2 * LANES), lambda b, i: (b, 0, i, 0)), spec128],
        out_shape=[jax.ShapeDtypeStruct((B, NSA_KV, S, 2 * LANES), BF16),
                   jax.ShapeDtypeStruct((B, NSA_KV, S, LANES), BF16)],
        compiler_params=_cparams(("arbitrary", "arbitrary")),
        name="nsa_prep",
    )(nkv, wkv)


def _overlap(i_blk, j_blk):
    start = i_blk * CMP_STRIDE
    return (start < j_blk * SEL_LEN + SEL_LEN) & (start + CMP_LEN > j_blk * SEL_LEN)


NEAR_BACK = 16


def _nsa_cmp_kernel(rb_ref, q_ref, kcw_ref, oc_ref, mq_ref, fc_ref, qs_ref):
    b = pl.program_id(0)
    i = pl.program_id(1)
    tq = q_ref.shape[1]
    n = kcw_ref.shape[2] - CMP_PAD_FRONT - CMP_PAD_BACK
    n_sel = n * CMP_STRIDE // SEL_LEN
    k_eff = min(N_SELECT, n_sel)

    @pl.when((b == 0) & (i == 0))
    def _():
        dist = _iota((tq, LANES), 0) + (NEAR_BACK * CMP_STRIDE - (CMP_LEN - 1)) - CMP_STRIDE * _iota((tq, LANES), 1)
        bk = _bucket(jnp.clip(dist, 0, MAX_DISTANCE - 1))
        for h in range(H_NSA):
            fc_ref[h // NSA_GH, (h % NSA_GH) * tq:(h % NSA_GH + 1) * tq, :] = jnp.where(
                dist >= 0, _bias_lookup(bk, lambda bb, h=h: rb_ref[bb, h]), NEG)

    qs = i * tq
    i0 = qs // CMP_STRIDE - NEAR_BACK
    st = pl.multiple_of(qs // CMP_STRIDE, 16)
    mrows = NSA_GH * tq
    farmask = _iota((mrows, n), 1) < i0
    nearmask = (_iota((mrows, LANES), 1) + i0) >= 0
    rowhead = _div(_iota((mrows, 1), 0), tq)
    ps_far = []
    ps_near = []
    outs = []
    for g in range(NSA_KV):
        _stack_heads(q_ref[0][:, g * 2 * LANES:(g + 1) * 2 * LANES], jnp.zeros((NSA_GH, LANES), F32), None, qs_ref)
        qq = qs_ref[...]
        gl = slice(g * NSA_GH * HD, g * NSA_GH * HD + LANES)
        kfar = kcw_ref[0, 0, CMP_PAD_FRONT:CMP_PAD_FRONT + n, gl]
        vfar = kcw_ref[0, 1, CMP_PAD_FRONT:CMP_PAD_FRONT + n, gl]
        knear = kcw_ref[0, 0, pl.ds(st, LANES), gl]
        vnear = kcw_ref[0, 1, pl.ds(st, LANES), gl]
        bfar = jnp.zeros((mrows, 1), F32)
        for hh in range(NSA_GH):
            bfar = jnp.where(rowhead == hh, rb_ref[N_BUCKETS - 1, g * NSA_GH + hh], bfar)
        sf = jnp.where(farmask, _dot_nt(qq, kfar) + bfar, NEG)
        sn = jnp.where(nearmask, _dot_nt(qq, knear) + fc_ref[g], NEG)
        m = jnp.maximum(jnp.max(sf, axis=1, keepdims=True), jnp.max(sn, axis=1, keepdims=True))
        pf = jnp.where(sf > 0.5 * NEG, jnp.exp(sf - m), 0.0)
        pn = jnp.where(sn > 0.5 * NEG, jnp.exp(sn - m), 0.0)
        l = jnp.sum(pf, axis=1, keepdims=True) + jnp.sum(pn, axis=1, keepdims=True)
        inv = 1.0 / jnp.maximum(l, TINY)
        pf = pf * inv
        pn = pn * inv
        outs.append(_unstack_heads(_dot(pf.astype(BF16), vfar) + _dot(pn.astype(BF16), vnear), tq))
        ps_far.append(sum(pf[hh * tq:(hh + 1) * tq] for hh in range(NSA_GH)))
        ps_near.append(sum(pn[hh * tq:(hh + 1) * tq] for hh in range(NSA_GH)))
    oc_ref[0] = jnp.concatenate(outs, axis=1)

    mov_far = jnp.where(_overlap(_iota((n, LANES), 0), _iota((n, LANES), 1)), 1.0, 0.0).astype(BF16)
    mov_near = jnp.where(_overlap(_iota((LANES, LANES), 0) + i0, _iota((LANES, LANES), 1)), 1.0, 0.0).astype(BF16)
    qpos = qs + _iota((tq, LANES), 0)
    jb = _iota((tq, LANES), 1)
    qblk = _div(qpos, SEL_LEN)
    valid = jb * SEL_LEN <= qpos
    forced = (jb == 0) | (jb == qblk) | (jb == qblk - 1)
    scores = []
    for g in range(NSA_KV):
        p_slc = _dot2_l(ps_far[g], mov_far) + _dot2_l(ps_near[g], mov_near)
        scores.append(jnp.where(forced, FORCE_SCORE, jnp.where(valid, p_slc, -1.0)))
    st = jnp.concatenate([sc.T for sc in scores], axis=1)
    keep = jnp.where((_topk_mask(st, k_eff, axis=0) > 0.5) & (st >= 0.0), 0.0, NEG)
    for g in range(NSA_KV):
        mq_ref[0, g] = keep[:, g * tq:(g + 1) * tq].T.astype(BF16)


def _nsa_cmp(rb, qn, kcw):
    B, S, _ = qn.shape
    tq = min(256, S)
    npad = kcw.shape[2]
    return pl.pallas_call(
        _nsa_cmp_kernel,
        grid=(B, S // tq),
        in_specs=[pl.BlockSpec(memory_space=pltpu.SMEM),
                  pl.BlockSpec((1, tq, NSA_W), lambda b, i: (b, i, 0)),
                  pl.BlockSpec((1, 2, npad, NSA_W), lambda b, i: (b, 0, 0, 0))],
        out_specs=[pl.BlockSpec((1, tq, NSA_W), lambda b, i: (b, i, 0)),
                   pl.BlockSpec((1, NSA_KV, tq, LANES), lambda b, i: (b, 0, i, 0))],
        out_shape=[jax.ShapeDtypeStruct((B, S, NSA_W), F32),
                   jax.ShapeDtypeStruct((B, NSA_KV, S, LANES), BF16)],
        scratch_shapes=[pltpu.VMEM((NSA_KV, NSA_GH * tq, LANES), F32), pltpu.VMEM((NSA_GH * tq, LANES), BF16)],
        compiler_params=_cparams(("arbitrary", "arbitrary")),
        name="nsa_cmp",
    )(rb, qn, kcw)


def _stack_heads(q, g_rows, extra, qs_ref, scale=SCALE):
    tq = q.shape[0]
    lane = _iota((tq, LANES), 1)
    for hh in range(NSA_GH):
        a = (hh // 2) * LANES
        t = q[:, a:a + LANES]
        if hh % 2:
            t = pltpu.roll(t, HD, 1)
        left = jnp.where(lane < HD, t * scale, g_rows[hh:hh + 1, :]).astype(BF16)
        if extra is None:
            qs_ref[hh * tq:(hh + 1) * tq, :] = left
        else:
            qs_ref[hh * tq:(hh + 1) * tq, :] = jnp.concatenate([left, extra], axis=1)


def _unstack_heads(a, tq):
    lane = _iota((tq, LANES), 1)
    p0 = jnp.where(lane < HD, a[0:tq], a[tq:2 * tq])
    p1 = jnp.where(lane < HD, a[2 * tq:3 * tq], a[3 * tq:4 * tq])
    return jnp.concatenate([p0, p1], axis=1)


SEL_BACK = 128
NSEL_CHAINS = 1


def _nsa_sel_kernel(rb_ref, q_ref, mq_ref, brow_ref, ksa_ref, vt_ref, o_ref, dn_ref, qs_ref, m_ref, acc_ref):
    b = pl.program_id(0)
    g = pl.program_id(1)
    i = pl.program_id(2)
    tq = q_ref.shape[1]
    wn = tq + SEL_BACK

    @pl.when((b == 0) & (g == 0) & (i == 0))
    def _():
        dist = _iota((wn, tq), 1) + SEL_BACK - _iota((wn, tq), 0)
        bk = _bucket(jnp.clip(dist, 0, MAX_DISTANCE - 1))
        for h in range(H_NSA):
            far = rb_ref[N_BUCKETS - 1, h]
            val = (_bias_lookup(bk, lambda bb, h=h: rb_ref[bb, h]) - far) * LOG2E
            dn_ref[h // NSA_GH, :, (h % NSA_GH) * tq:(h % NSA_GH + 1) * tq] = jnp.where(dist >= 0, val, NEG)

    qs = i * tq
    _stack_heads(q_ref[0], brow_ref[0], mq_ref[0, 0], qs_ref, scale=SCALE * LOG2E)
    _softmax_init_t(m_ref, acc_ref)
    nch = m_ref.shape[0]
    hw = (NSA_GH // nch) * tq

    def tile(start, size, bias):
        k = ksa_ref[0, 0, pl.ds(start, size), :]
        vt = vt_ref[0, 0, :, pl.ds(start, size)]
        for half in range(nch):
            s = _dot_nt(k, qs_ref[half * hw:(half + 1) * hw, :])
            if bias is not None:
                s = s + bias(half)
            _softmax_update_t(s, vt, m_ref.at[half], acc_ref.at[half])

    @pl.when(i == 0)
    def _():
        tile(0, tq, lambda half: dn_ref[g, SEL_BACK:, half * hw:(half + 1) * hw])

    @pl.when(i > 0)
    def _():
        tile(pl.multiple_of(qs - SEL_BACK, LANES), wn, lambda half: dn_ref[g, :, half * hw:(half + 1) * hw])
        tile(pl.multiple_of(qs - tq, LANES), tq - SEL_BACK, None)

    n_far = jnp.maximum(i - 1, 0)

    def far_pair(kt, c):
        tile(pl.multiple_of(kt * 2 * tq, 2 * tq), 2 * tq, None)
        return c

    lax.fori_loop(0, n_far // 2, far_pair, 0)

    @pl.when(n_far % 2 == 1)
    def _():
        tile(pl.multiple_of((n_far - 1) * tq, tq), tq, None)

    parts = []
    for half in range(nch):
        acc = acc_ref[half]
        a = acc[0:HD, :] / jnp.maximum(acc[HD:HD + 1, :], TINY)
        parts += [a[:, j * tq:(j + 1) * tq] for j in range(NSA_GH // nch)]
    o_ref[0] = jnp.concatenate(parts, axis=0).T


def _nsa_sel(rb, qn, maskq, brow, ksa, vst):
    B, S, _ = qn.shape
    tq = min(512, S)
    G = NSA_KV
    return pl.pallas_call(
        _nsa_sel_kernel,
        grid=(B, G, S // tq),
        in_specs=[pl.BlockSpec(memory_space=pltpu.SMEM),
                  pl.BlockSpec((1, tq, 2 * LANES), lambda b, g, i: (b, i, g)),
                  pl.BlockSpec((1, 1, tq, LANES), lambda b, g, i: (b, g, i, 0)),
                  pl.BlockSpec((1, NSA_GH, LANES), lambda b, g, i: (g, 0, 0)),
                  pl.BlockSpec((1, 1, S, 2 * LANES), lambda b, g, i: (b, g, 0, 0)),
                  pl.BlockSpec((1, 1, HD + ONES_ROWS, S), lambda b, g, i: (b, g, 0, 0))],
        out_specs=pl.BlockSpec((1, tq, 2 * LANES), lambda b, g, i: (b, i, g)),
        out_shape=jax.ShapeDtypeStruct((B, S, NSA_W), F32),
        scratch_shapes=[pltpu.VMEM((G, tq + SEL_BACK, NSA_GH * tq), F32),
                        pltpu.VMEM((NSA_GH * tq, 2 * LANES), BF16),
                        pltpu.VMEM((NSEL_CHAINS, 1, NSA_GH // NSEL_CHAINS * tq), F32),
                        pltpu.VMEM((NSEL_CHAINS, HD + ONES_ROWS, NSA_GH // NSEL_CHAINS * tq), F32)],
        compiler_params=_cparams(("arbitrary", "arbitrary", "arbitrary")),
        name="nsa_sel",
    )(rb, qn, maskq, brow, ksa, vst)


def _nsa_win_kernel(rb_ref, q_ref, kwa_ref, vt_ref, o_ref, dw_ref, qs_ref):
    b = pl.program_id(0)
    g = pl.program_id(1)
    i = pl.program_id(2)
    tq = q_ref.shape[1]
    wk = tq + WINDOW

    @pl.when((b == 0) & (g == 0) & (i == 0))
    def _():
        dist = _iota((wk, tq), 1) + WINDOW - _iota((wk, tq), 0)
        bk = _bucket(jnp.clip(dist, 0, MAX_DISTANCE - 1))
        ok = (dist >= 0) & (dist < WINDOW)
        for h in range(H_NSA):
            val = _bias_lookup(bk, lambda bb, h=h: rb_ref[bb, h]) * LOG2E
            dw_ref[h // NSA_GH, :, (h % NSA_GH) * tq:(h % NSA_GH + 1) * tq] = jnp.where(ok, val, NEG)

    qs = pl.multiple_of(i * tq, tq)
    _stack_heads(q_ref[0], jnp.zeros((NSA_GH, LANES), F32), None, qs_ref, scale=SCALE * LOG2E)
    s = _dot_nt(kwa_ref[0, 0, pl.ds(qs, wk), :], qs_ref[...]) + dw_ref[g]
    s = jnp.where(_iota(s.shape, 0) + qs >= WINDOW, s, NEG)
    m = jnp.max(s, axis=0, keepdims=True)
    p = jnp.exp2(s - m).astype(BF16)
    acc = _dot(vt_ref[0, 0, :, pl.ds(qs, wk)], p)
    a = acc[0:HD, :] / jnp.maximum(acc[HD:HD + 1, :], TINY)
    o_ref[0] = jnp.concatenate([a[:, hh * tq:(hh + 1) * tq] for hh in range(NSA_GH)], axis=0).T


def _nsa_win(rb, qn, kwa_p, vwt_p):
    B, S, _ = qn.shape
    tq = min(256, S)
    G = NSA_KV
    sp = kwa_p.shape[2]
    return pl.pallas_call(
        _nsa_win_kernel,
        grid=(B, G, S // tq),
        in_specs=[pl.BlockSpec(memory_space=pltpu.SMEM),
                  pl.BlockSpec((1, tq, 2 * LANES), lambda b, g, i: (b, i, g)),
                  pl.BlockSpec((1, 1, sp, LANES), lambda b, g, i: (b, g, 0, 0)),
                  pl.BlockSpec((1, 1, HD + ONES_ROWS, sp), lambda b, g, i: (b, g, 0, 0))],
        out_specs=pl.BlockSpec((1, tq, 2 * LANES), lambda b, g, i: (b, i, g)),
        out_shape=jax.ShapeDtypeStruct((B, S, NSA_W), F32),
        scratch_shapes=[pltpu.VMEM((G, tq + WINDOW, NSA_GH * tq), F32),
                        pltpu.VMEM((NSA_GH * tq, LANES), BF16)],
        compiler_params=_cparams(("arbitrary", "arbitrary", "arbitrary")),
        name="nsa_win",
    )(rb, qn, kwa_p, vwt_p)


def _merge_kernel(x_ref, of_ref, oc_ref, os_ref, ow_ref, misc_ref, g_ref, w_ref, out_ref):
    tm = x_ref.shape[0]
    hi, lo = _split2(misc_ref[...])
    er = _iota((LANES, NSA_W), 0)
    ec = _iota((LANES, NSA_W), 1)
    onsa = jnp.zeros((tm, NSA_W), F32)
    for k, o_ref in enumerate((oc_ref, os_ref, ow_ref)):
        e = jnp.where(er == MISC_GATE + k * H_NSA + _div(ec, HD), 1.0, 0.0).astype(BF16)
        onsa = onsa + (_dot(hi, e) + _dot(lo, e)) * o_ref[...]
    g = g_ref[...]
    a = _rms(of_ref[...], g[:, :FOX_W]).astype(BF16)
    c = _rms(onsa, g[:, FOX_W:]).astype(BF16)
    out_ref[...] = x_ref[...] + _dot(a, w_ref[0:FOX_W, :]) + _dot(c, w_ref[FOX_W:, :])


def _merge(x2, ofox, oc, os_, ow, misc, g, w):
    T, D = x2.shape
    tm = min(512, T)
    row = lambda wd: pl.BlockSpec((tm, wd), lambda i: (i, 0))
    return pl.pallas_call(
        _merge_kernel,
        grid=(T // tm,),
        in_specs=[row(D), row(FOX_W), row(NSA_W), row(NSA_W), row(NSA_W), row(LANES),
                  pl.BlockSpec((1, D), lambda i: (0, 0)),
                  pl.BlockSpec((D, D), lambda i: (0, 0))],
        out_specs=row(D),
        out_shape=jax.ShapeDtypeStruct((T, D), F32),
        compiler_params=_cparams(("arbitrary",)),
        name="merge",
    )(x2, ofox, oc, os_, ow, misc, g, w)


FF_CHUNK = 1408


def _ffn_kernel(*refs, seq_len, short_len, final):
    if seq_len is None:
        (x_ref, g2_ref, wg_ref, wu_ref, cw_ref, cb_ref, wd_ref, gf_ref, hm1_ref, hm2_ref,
         out_ref, gt_ref, h_sc, acc_sc) = refs
    else:
        (x_ref, g2_ref, wg_ref, wu_ref, cw_ref, cb_ref, wd_ref, gf_ref,
         out_ref, gt_ref, h_sc, acc_sc, carry_sc) = refs
    i = pl.program_id(0)
    j = pl.program_id(1)
    tm = x_ref.shape[0]

    @pl.when(j == 0)
    def _():
        h_sc[...] = _rms(x_ref[...], g2_ref[...]).astype(BF16)
        acc_sc[...] = jnp.zeros_like(acc_sc)

    h = h_sc[...]
    gch = _dot(h, wg_ref[...])
    u = _dot(h, wu_ref[...])
    r1 = pltpu.roll(gch, 1, 0)
    r2 = pltpu.roll(gch, 2, 0)
    row = _iota(gch.shape, 0)
    if seq_len is None:
        t = _mod(row, short_len)
        m1 = jnp.where(t == 0, hm1_ref[...], r1)
        m2 = jnp.where(t < 2, hm2_ref[...], r2)
        gt_ref[...] = gch
    else:
        first = (i % (seq_len // tm)) == 0
        c = jnp.where(first, 0.0, carry_sc[j])
        m1 = jnp.where(row == 0, c[1:2, :], r1)
        m2 = jnp.where(row == 0, c[0:1, :], jnp.where(row == 1, c[1:2, :], r2))
        carry_sc[j, 0:2, :] = gch[tm - 2:tm, :]
        gt_ref[0] = gch[tm - 8:tm, :]
    cw = cw_ref[...]
    gc = cb_ref[...] + cw[0:1, :] * m2 + cw[1:2, :] * m1 + cw[2:3, :] * gch
    act = gc * (1.0 / (1.0 + jnp.exp(-gc)))
    acc_sc[...] += _dot((act * u).astype(BF16), wd_ref[...])

    @pl.when(j == pl.num_programs(1) - 1)
    def _():
        y = x_ref[...] + acc_sc[...]
        if final:
            y = _rms(y, gf_ref[...])
        out_ref[...] = y


def _ffn(x2, g2, wgu, cw, cb, wd, gf, *, seq_len, final, hist=None, short_len=None):
    T, D = x2.shape
    dff = wd.shape[0]
    fc = FF_CHUNK if dff % FF_CHUNK == 0 else LANES
    nff = dff // fc
    tm = min(512, T) if seq_len is not None else T
    nt = T // tm
    in_specs = [pl.BlockSpec((tm, D), lambda i, j: (i, 0)),
                pl.BlockSpec((1, D), lambda i, j: (0, 0)),
                pl.BlockSpec((D, fc), lambda i, j: (0, j)),
                pl.BlockSpec((D, fc), lambda i, j: (0, nff + j)),
                pl.BlockSpec((CONV_W, fc), lambda i, j: (0, j)),
                pl.BlockSpec((1, fc), lambda i, j: (0, j)),
                pl.BlockSpec((fc, D), lambda i, j: (j, 0)),
                pl.BlockSpec((1, D), lambda i, j: (0, 0))]
    args = [x2, g2, wgu, wgu, cw, cb, wd, gf]
    scratch = [pltpu.VMEM((tm, D), BF16), pltpu.VMEM((tm, D), F32)]
    if seq_len is None:
        in_specs += [pl.BlockSpec((tm, fc), lambda i, j: (i, j)), pl.BlockSpec((tm, fc), lambda i, j: (i, j))]
        args += list(hist)
        gt_spec = pl.BlockSpec((tm, fc), lambda i, j: (i, j))
        gt_shape = jax.ShapeDtypeStruct((T, dff), F32)
    else:
        scratch.append(pltpu.VMEM((nff, 8, fc), F32))
        gt_spec = pl.BlockSpec((1, 8, fc), lambda i, j: (i, 0, j))
        gt_shape = jax.ShapeDtypeStruct((nt, 8, dff), F32)
    return pl.pallas_call(
        functools.partial(_ffn_kernel, seq_len=seq_len, short_len=short_len, final=final),
        grid=(nt, nff),
        in_specs=in_specs,
        out_specs=[pl.BlockSpec((tm, D), lambda i, j: (i, 0)), gt_spec],
        out_shape=[jax.ShapeDtypeStruct((T, D), F32), gt_shape],
        scratch_shapes=scratch,
        compiler_params=_cparams(("arbitrary", "arbitrary")),
        name="ffn",
    )(*args)


def _rows_th(q):
    tq = q.shape[0]
    rows = jnp.concatenate([jnp.broadcast_to(q[t:t + 1, :], (8, q.shape[1])) for t in range(tq)], axis=0)
    keep = _div(_iota(rows.shape, 1), HD) == _mod(_iota(rows.shape, 0), 8)
    return jnp.where(keep, rows * SCALE, 0.0).astype(BF16)


def _diag_rows(o_ref, o32):
    keep = _div(_iota(o32.shape, 1), HD) == _mod(_iota(o32.shape, 0), 8)
    od = jnp.where(keep, o32, 0.0)
    for t in range(o32.shape[0] // 8):
        o_ref[0, t:t + 1, :] = jnp.sum(od[t * 8:(t + 1) * 8, :], axis=0, keepdims=True)


FOX_PAGES = 16


def _fox_sample_kernel(pt_tab, *refs):
    P = FOX_PAGES
    q_ref, kvn_ref, lfn_ref = refs[0:3]
    k_refs = refs[3:3 + P]
    v_refs = refs[3 + P:3 + 2 * P]
    lf_refs = refs[3 + 2 * P:3 + 3 * P]
    o_ref, q_sc, m_ref, l_ref, acc_ref, carry_ref, new_sc = refs[3 + 3 * P:]
    s = pl.program_id(1)
    tq = q_ref.shape[1]
    nr = tq * 8

    @pl.when(s == 0)
    def _():
        q_sc[...] = _rows_th(q_ref[0])
        _softmax_init(m_ref, l_ref, acc_ref)
        carry_ref[...] = jnp.zeros_like(carry_ref)
        new_sc[...] = jnp.zeros_like(new_sc)
        new_sc[0:tq, :] = kvn_ref[0]

    qq = q_sc[...]
    triu = (_iota((PAGE, PAGE), 0) <= _iota((PAGE, PAGE), 1)).astype(BF16)

    def attend(score_fns, value_fns, lfs, extra_mask):
        off = carry_ref[...]
        sc = []
        within = _dot3_l(jnp.concatenate(lfs, axis=0), triu)
        for k, fn in enumerate(score_fns):
            cum = within[H_FOX * k:H_FOX * (k + 1), :] + off
            off = jnp.broadcast_to(cum[:, PAGE - 1:PAGE], cum.shape)
            sc.append(fn() - jnp.concatenate([cum] * tq, axis=0))
        carry_ref[...] = off
        sc = jnp.concatenate(sc, axis=1)
        if extra_mask is not None:
            sc = jnp.where(extra_mask, sc, NEG)
        m_old = m_ref[...]
        m_new = jnp.maximum(m_old, jnp.max(sc, axis=1, keepdims=True))
        alpha = jnp.exp(m_old - m_new)
        p = jnp.exp(sc - m_new)
        l_ref[...] = alpha * l_ref[...] + jnp.sum(p, axis=1, keepdims=True)
        acc = alpha * acc_ref[...]
        for k, fn in enumerate(value_fns):
            acc = acc + fn(p[:, k * PAGE:(k + 1) * PAGE].astype(BF16))
        acc_ref[...] = acc
        m_ref[...] = m_new

    attend([lambda r=r: _dot(qq, r[...].reshape(FOX_W, PAGE).astype(BF16)) for r in k_refs],
           [lambda p, r=r: _dot_nt(p, r[...].reshape(FOX_W, PAGE).astype(BF16)) for r in v_refs],
           [r[...] for r in lf_refs], None)

    @pl.when(s == pl.num_programs(1) - 1)
    def _():
        key = _iota((nr, PAGE), 1)
        ok = (key < tq) & (key <= _div(_iota((nr, PAGE), 0), 8))
        new = new_sc[...]
        attend([lambda: _dot_nt(qq, new[:, :FOX_W].astype(BF16))],
               [lambda p: _dot(p, new[:, FOX_W:].astype(BF16))], [lfn_ref[0]], ok)
        _diag_rows(o_ref, acc_ref[...] / jnp.maximum(l_ref[...], TINY))


def _fox_sample(qf, fkv_new, lfn_t, cache_kt, cache_lft, page_table, layer):
    nb, npages = page_table.shape
    tq = qf.shape[1]
    P = FOX_PAGES
    kvw = 2 * FOX_W
    seq = lambda b, s, pt: (b, 0, 0)

    def kv_spec(k, c):
        return pl.BlockSpec((None, None, None, H_FOX, HD, PAGE),
                            lambda b, s, pt: (layer, pt[b, s * P + k], c, 0, 0, 0))

    def lf_spec(k):
        return pl.BlockSpec((None, None, H_FOX, PAGE), lambda b, s, pt: (layer, pt[b, s * P + k], 0, 0))

    return pl.pallas_call(
        _fox_sample_kernel,
        grid_spec=pltpu.PrefetchScalarGridSpec(
            num_scalar_prefetch=1,
            grid=(nb, npages // P),
            in_specs=[pl.BlockSpec((1, tq, FOX_W), seq), pl.BlockSpec((1, tq, kvw), seq),
                      pl.BlockSpec((1, H_FOX, PAGE), seq)]
            + [kv_spec(k, 0) for k in range(P)] + [kv_spec(k, 1) for k in range(P)]
            + [lf_spec(k) for k in range(P)],
            out_specs=pl.BlockSpec((1, tq, FOX_W), seq),
            scratch_shapes=[pltpu.VMEM((tq * 8, FOX_W), BF16),
                            pltpu.VMEM((tq * 8, 1), F32), pltpu.VMEM((tq * 8, 1), F32),
                            pltpu.VMEM((tq * 8, FOX_W), F32),
                            pltpu.VMEM((H_FOX, PAGE), F32),
                            pltpu.VMEM((PAGE, kvw), F32)]),
        out_shape=jax.ShapeDtypeStruct((nb, tq, FOX_W), F32),
        compiler_params=_cparams(("arbitrary", "arbitrary")),
        name="fox_sample",
    )(page_table, qf, fkv_new, lfn_t, *([cache_kt] * (2 * P)), *([cache_lft] * P))


def _place_wide():
    r = _iota((LANES, NSA_W), 0)
    c = _iota((LANES, NSA_W), 1)
    return jnp.where(r == _div(c, NSA_GH * HD) * HD + _mod(c, HD), 1.0, 0.0).astype(BF16)


def _rb_col(rbt_ref):
    return lambda bb: rbt_ref[:, bb:bb + 1]


def _fold_heads():
    r = _iota((NSA_W, LANES), 0)
    c = _iota((NSA_W, LANES), 1)
    return jnp.where(c == _div(r, NSA_GH * HD) * HD + _mod(r, HD), 1.0, 0.0).astype(BF16)


def _ns_attend_kernel(q_ref, kcw_ref, kwt_ref, vwt_ref, wn_ref, rbt_ref, oc_ref, ow_ref, mb_ref, new_sc, *, past):
    tq = q_ref.shape[1]
    nr = tq * 8
    n = kcw_ref.shape[2] - CMP_PAD_FRONT - CMP_PAD_BACK
    n_selp = past // SEL_LEN
    wb = kwt_ref.shape[2]
    qq = _rows_th(q_ref[0])
    kc = kcw_ref[0, 0, CMP_PAD_FRONT:CMP_PAD_FRONT + n, :]
    vc = kcw_ref[0, 1, CMP_PAD_FRONT:CMP_PAD_FRONT + n, :]
    trow = _div(_iota((nr, n), 0), 8)
    dist = past + trow - CMP_STRIDE * _iota((nr, n), 1) - (CMP_LEN - 1)
    bias = _bias_lookup(_bucket(jnp.clip(dist, 0, MAX_DISTANCE - 1)), _rb_col(rbt_ref))
    ok = dist >= 0
    s = jnp.where(ok, _dot_nt(qq, kc) + bias, NEG)
    m = jnp.max(s, axis=1, keepdims=True)
    p = jnp.where(ok, jnp.exp(s - m), 0.0)
    p = p / jnp.maximum(jnp.sum(p, axis=1, keepdims=True), TINY)
    _diag_rows(oc_ref, _dot(p.astype(BF16), vc))
    mov = jnp.where(_overlap(_iota((n, LANES), 0), _iota((n, LANES), 1)), 1.0, 0.0).astype(BF16)
    x = _dot2_l(p, mov)
    z = x + pltpu.roll(x, nr - 1, 0) + pltpu.roll(x, nr - 2, 0) + pltpu.roll(x, nr - 3, 0)
    z0 = jnp.where(_mod(_iota(z.shape, 0), NSA_GH) == 0, z, 0.0)
    p_slc = z0 + pltpu.roll(z0, 1, 0) + pltpu.roll(z0, 2, 0) + pltpu.roll(z0, 3, 0)
    jb = _iota((nr, LANES), 1)
    forced = (jb == 0) | (jb == n_selp - 1)
    score = jnp.where(forced, FORCE_SCORE, jnp.where(jb < n_selp, p_slc, -1.0))
    k_past = min(N_SELECT, n_selp + 1) - 1
    sel = _topk_mask(score, k_past)
    mb_ref[0] = jnp.where((sel > 0.5) & (score >= 0.0), 0.0, NEG).astype(BF16)
    qf = _dot(qq, _fold_heads()).astype(BF16)
    kwt = kwt_ref[...].reshape(LANES, wb).astype(BF16)
    vwt = vwt_ref[...].reshape(LANES, wb).astype(BF16)
    new_sc[...] = jnp.zeros_like(new_sc)
    new_sc[0:tq, :] = wn_ref[0]
    wnew = new_sc[...]
    dw = _div(_iota((nr, wb), 0), 8) + wb - _iota((nr, wb), 1)
    okw = (dw >= 0) & (dw < WINDOW)
    sw = _dot(qf, kwt) + _bias_lookup(_bucket(jnp.clip(dw, 0, MAX_DISTANCE - 1)), _rb_col(rbt_ref))
    sw = jnp.where(okw, sw, NEG)
    dn = _div(_iota((nr, PAGE), 0), 8) - _iota((nr, PAGE), 1)
    okn = (dn >= 0) & (_iota((nr, PAGE), 1) < tq)
    sn = _dot_nt(qf, wnew[:, 0:LANES].astype(BF16))
    sn = sn + _bias_lookup(_bucket(jnp.clip(dn, 0, MAX_DISTANCE - 1)), _rb_col(rbt_ref))
    sn = jnp.where(okn, sn, NEG)
    mw = jnp.maximum(jnp.max(sw, axis=1, keepdims=True), jnp.max(sn, axis=1, keepdims=True))
    pw = jnp.where(okw, jnp.exp(sw - mw), 0.0)
    pn = jnp.where(okn, jnp.exp(sn - mw), 0.0)
    lw = jnp.maximum(jnp.sum(pw, axis=1, keepdims=True) + jnp.sum(pn, axis=1, keepdims=True), TINY)
    a = (_dot_nt(pw.astype(BF16), vwt) + _dot(pn.astype(BF16), wnew[:, LANES:2 * LANES].astype(BF16))) / lw
    _diag_rows(ow_ref, _dot2_l(a, _place_wide()))


def _ns_attend(qn, kcw, win_t, wkv_new, rbt, layer, past):
    nb, tq, _ = qn.shape
    npad = kcw.shape[2]
    wb = win_t.shape[5]
    seq = lambda b: (b, 0, 0)

    def win_spec(c):
        return pl.BlockSpec((None, None, None, NSA_KV, HD, wb), lambda b: (layer, b, c, 0, 0, 0))

    return pl.pallas_call(
        functools.partial(_ns_attend_kernel, past=past),
        grid=(nb,),
        in_specs=[pl.BlockSpec((1, tq, NSA_W), seq),
                  pl.BlockSpec((1, 2, npad, NSA_W), lambda b: (b, 0, 0, 0)),
                  win_spec(0), win_spec(1),
                  pl.BlockSpec((1, tq, 2 * LANES), seq),
                  pl.BlockSpec((tq * 8, N_BUCKETS), lambda b: (0, 0))],
        out_specs=[pl.BlockSpec((1, tq, NSA_W), seq), pl.BlockSpec((1, tq, NSA_W), seq),
                   pl.BlockSpec((1, tq * 8, LANES), seq)],
        out_shape=[jax.ShapeDtypeStruct((nb, tq, NSA_W), F32), jax.ShapeDtypeStruct((nb, tq, NSA_W), F32),
                   jax.ShapeDtypeStruct((nb, tq * 8, LANES), BF16)],
        scratch_shapes=[pltpu.VMEM((PAGE, 2 * LANES), F32)],
        compiler_params=_cparams(("arbitrary",)),
        name="ns_attend",
    )(qn, kcw, win_t, win_t, wkv_new, rbt)


def _ns_select_kernel(pt_tab, *refs, past):
    P = PAGES_PER_STEP
    q_ref, mb_ref, nn_ref, rbt_ref = refs[0:4]
    k_refs = refs[4:4 + P]
    v_refs = refs[4 + P:4 + 2 * P]
    o_ref, q_sc, m_ref, l_ref, acc_ref, new_sc = refs[4 + 2 * P:]
    s = pl.program_id(1)
    tq = q_ref.shape[1]
    nr = tq * 8

    @pl.when(s == 0)
    def _():
        q_sc[...] = _dot(_rows_th(q_ref[0]), _fold_heads()).astype(BF16)
        _softmax_init(m_ref, l_ref, acc_ref)
        new_sc[...] = jnp.zeros_like(new_sc)
        new_sc[0:tq, :] = nn_ref[0][:, 2 * LANES:]

    qq = q_sc[...]
    trow = _div(_iota((nr, PAGE), 0), 8)
    key = _iota((nr, PAGE), 1)

    def attend(score_fns, value_fns, biases):
        sc = jnp.concatenate([fn() for fn in score_fns], axis=1) + biases
        m_old = m_ref[...]
        m_new = jnp.maximum(m_old, jnp.max(sc, axis=1, keepdims=True))
        alpha = jnp.exp(m_old - m_new)
        p = jnp.exp(sc - m_new)
        l_ref[...] = alpha * l_ref[...] + jnp.sum(p, axis=1, keepdims=True)
        acc = alpha * acc_ref[...]
        for k, fn in enumerate(value_fns):
            acc = acc + fn(p[:, k * PAGE:(k + 1) * PAGE].astype(BF16))
        acc_ref[...] = acc
        m_ref[...] = m_new

    blk = _iota((LANES, P * PAGE), 0)
    kcol = _iota((LANES, P * PAGE), 1)
    expand = jnp.where(blk == _div(s * (P * PAGE) + kcol, SEL_LEN), 1.0, 0.0).astype(BF16)
    bias = _dot(mb_ref[0], expand) + rbt_ref[:, N_BUCKETS - 1:N_BUCKETS]
    tiles = ([lambda r=r: _dot(qq, r[...].reshape(LANES, PAGE).astype(BF16)) for r in k_refs],
             [lambda p, r=r: _dot_nt(p, r[...].reshape(LANES, PAGE).astype(BF16)) for r in v_refs])

    @pl.when(s < pl.num_programs(1) - 1)
    def _():
        attend(*tiles, bias)

    @pl.when(s == pl.num_programs(1) - 1)
    def _():
        d_last = past + trow - (past - PAGE + key)
        b_last = _bias_lookup(_bucket(jnp.clip(d_last, 0, MAX_DISTANCE - 1)), _rb_col(rbt_ref))
        fix = jnp.concatenate([jnp.zeros((nr, (P - 1) * PAGE), F32),
                               b_last - rbt_ref[:, N_BUCKETS - 1:N_BUCKETS]], axis=1)
        attend(*tiles, bias + fix)
        d_new = trow - key
        b_new = _bias_lookup(_bucket(jnp.clip(d_new, 0, MAX_DISTANCE - 1)), _rb_col(rbt_ref))
        new = new_sc[...]
        attend([lambda: _dot_nt(qq, new[:, 0:LANES].astype(BF16))],
               [lambda p: _dot(p, new[:, LANES:2 * LANES].astype(BF16))],
               jnp.where((d_new >= 0) & (key < tq), b_new, NEG))
        a = acc_ref[...] / jnp.maximum(l_ref[...], TINY)
        place = _place_wide()
        _diag_rows(o_ref, _dot2_l(a, place))


def _ns_select(qn, maskb, nkv_new, rbt, cache_nsa, page_table, layer, past):
    nb, npages = page_table.shape
    tq = qn.shape[1]
    P = PAGES_PER_STEP
    seq = lambda b, s, pt: (b, 0, 0)

    def page_spec(k, c):
        return pl.BlockSpec((None, None, None, NSA_KV, HD, PAGE),
                            lambda b, s, pt: (layer, pt[b, s * P + k], c, 0, 0, 0))

    return pl.pallas_call(
        functools.partial(_ns_select_kernel, past=past),
        grid_spec=pltpu.PrefetchScalarGridSpec(
            num_scalar_prefetch=1,
            grid=(nb, npages // P),
            in_specs=[pl.BlockSpec((1, tq, NSA_W), seq), pl.BlockSpec((1, tq * 8, LANES), seq),
                      pl.BlockSpec((1, tq, 4 * LANES), seq),
                      pl.BlockSpec((tq * 8, N_BUCKETS), lambda b, s, pt: (0, 0))]
            + [page_spec(k, 2) for k in range(P)] + [page_spec(k, 3) for k in range(P)],
            out_specs=pl.BlockSpec((1, tq, NSA_W), seq),
            scratch_shapes=[pltpu.VMEM((tq * 8, LANES), BF16),
                            pltpu.VMEM((tq * 8, 1), F32), pltpu.VMEM((tq * 8, 1), F32),
                            pltpu.VMEM((tq * 8, LANES), F32),
                            pltpu.VMEM((PAGE, 2 * LANES), F32)]),
        out_shape=jax.ShapeDtypeStruct((nb, tq, NSA_W), F32),
        compiler_params=_cparams(("arbitrary", "arbitrary")),
        name="ns_select",
    )(page_table, qn, maskb, nkv_new, rbt, *([cache_nsa] * (2 * P)))


def _prep_w_in(w):
    d = w.shape[0]
    o_logf = 3 * FOX_W
    o_qn = o_logf + H_FOX
    o_kv = o_qn + NSA_W
    o_gate = o_kv + 6 * NSA_KV * HD
    misc = jnp.concatenate([w[:, o_logf:o_qn], w[:, o_gate:o_gate + 3 * H_NSA],
                            jnp.zeros((d, LANES - H_FOX - 3 * H_NSA), w.dtype)], axis=1)
    return jnp.concatenate([w[:, :o_logf], w[:, o_qn:o_kv], w[:, o_kv:o_gate], misc], axis=1).astype(BF16)


def _prep_cmp(pos, w1, b1, w2):
    eye = jnp.eye(NSA_KV, dtype=w1.dtype)
    w1r = w1.reshape(2, 2, CMP_STRIDE, HD, CMP_HID)
    wide = jnp.einsum("whjdc,ab->whjadbc", w1r, eye).reshape(2, 2, CMP_STRIDE * LANES, 2 * CMP_HID)
    posr = jnp.broadcast_to(pos.reshape(2, 2, CMP_STRIDE, 1, HD), (2, 2, CMP_STRIDE, NSA_KV, HD))
    posr = posr.reshape(2, 2, 1, CMP_STRIDE * LANES)
    group_of_head = (jnp.arange(H_NSA) // NSA_GH)[None, :] == jnp.arange(NSA_KV)[:, None]
    w2w = jnp.einsum("wcd,ah->wachd", w2, group_of_head.astype(w2.dtype)).reshape(2, 2 * CMP_HID, NSA_W)
    return {"pt": posr[:, 0], "pb": posr[:, 1],
            "w1t": wide[:, 0].astype(BF16), "w1b": wide[:, 1].astype(BF16),
            "b1": jnp.concatenate([b1, b1], axis=-1)[:, None, :],
            "w2w": w2w.astype(BF16)}


def _far_bias_rows(rel_bias):
    far = rel_bias[N_BUCKETS - 1]
    hi = far.astype(BF16).astype(F32)
    r = far - hi
    mid = r.astype(BF16).astype(F32)
    lo = (r - mid).astype(BF16).astype(F32)
    rows = jnp.zeros((H_NSA, LANES), F32)
    rows = rows.at[:, HD].set(hi).at[:, HD + 1].set(mid).at[:, HD + 2].set(lo)
    return rows.reshape(NSA_KV, NSA_GH, LANES)


def kernel(x_prompt, x_sample, cache_fox_kv, cache_fox_logf, cache_nsa_kv, state_win_kv, state_conv,
           page_table, norm1_g, w_in, b_forget, cmp_pos, cmp_w1, cmp_b1, cmp_w2, out_norm_g, w_out,
           norm2_g, w_gu, conv_w, conv_b, w_down, rel_bias, final_norm_g):
    B, S, D = x_prompt.shape
    nb, tq, _ = x_sample.shape
    depth = w_in.shape[0]
    n_pool = cache_fox_kv.shape[1]
    npages = page_table.shape[1]
    past = npages * PAGE
    dff = w_down.shape[1]
    wb = state_win_kv.shape[2]
    assert tq & (tq - 1) == 0 and tq >= CONV_W - 1
    assert S % 256 == 0 and S // SEL_LEN <= LANES and past // SEL_LEN <= LANES
    assert npages % PAGES_PER_STEP == 0 and wb == WINDOW and past >= WINDOW and tq * 8 <= LANES

    assert npages % FOX_PAGES == 0 and PAGE == LANES
    cache_kt = jnp.transpose(cache_fox_kv, (0, 1, 3, 4, 5, 2))
    cache_lft = jnp.swapaxes(cache_fox_logf, 2, 3)
    cache_nsa = jnp.transpose(cache_nsa_kv, (0, 1, 3, 4, 5, 2))
    win_t = jnp.transpose(state_win_kv, (0, 1, 3, 4, 5, 2))
    rbt = jnp.tile(rel_bias.T, (tq, 1))
    brow = _far_bias_rows(rel_bias * LOG2E)
    gf = final_norm_g.reshape(1, D)

    xp = x_prompt.reshape(B * S, D)
    xs = x_sample.reshape(nb * tq, D)
    outs = [[] for _ in range(10)]
    for l in range(depth):
        w_l = _prep_w_in(w_in[l])
        bf = jnp.zeros((1, LANES), F32).at[0, :H_FOX].set(b_forget[l])
        cw = _prep_cmp(cmp_pos[l], cmp_w1[l], cmp_b1[l], cmp_w2[l])
        g1 = norm1_g[l].reshape(1, D)
        g2 = norm2_g[l].reshape(1, D)
        go = out_norm_g[l].reshape(1, D)
        wo = w_out[l].astype(BF16)
        wgu = w_gu[l].astype(BF16)
        wd = w_down[l].astype(BF16)
        cb = conv_b[l].reshape(1, dff)
        final = l == depth - 1

        qf, fkv, qn, nkv, wkv, misc = _proj(xp, g1, w_l, bf)
        r3 = lambda a: a.reshape(B, S, a.shape[-1])
        qa, ka, vb = _fox_prep(r3(qf), r3(fkv), r3(misc))
        vt = jnp.swapaxes(vb, 1, 2).reshape(B, H_FOX, HD, S)
        vt = jnp.concatenate([vt, jnp.ones((B, H_FOX, ONES_ROWS, S), BF16)], axis=2)
        o_fox = _fox_flash(qa, ka, vt)
        kcw = _compress_prompt(r3(nkv), cw)
        ksa, kwa = _nsa_prep(r3(nkv), r3(wkv))
        vst = r3(nkv)[:, :, 3 * LANES:].astype(BF16).reshape(B, S, NSA_KV, HD).transpose(0, 2, 3, 1)
        vst = jnp.concatenate([vst, jnp.ones((B, NSA_KV, ONES_ROWS, S), BF16)], axis=2)
        o_c, maskq = _nsa_cmp(rel_bias, r3(qn), kcw)
        o_s = _nsa_sel(rel_bias, r3(qn), maskq, brow, ksa, vst)
        vwt = r3(wkv)[:, :, LANES:].astype(BF16).reshape(B, S, NSA_KV, HD).transpose(0, 2, 3, 1)
        vwt = jnp.pad(vwt, ((0, 0), (0, 0), (0, 0), (WINDOW, 0)))
        vwt = jnp.concatenate([vwt, jnp.ones((B, NSA_KV, ONES_ROWS, S + WINDOW), BF16)], axis=2)
        o_w = _nsa_win(rel_bias, r3(qn), jnp.pad(kwa, ((0, 0), (0, 0), (WINDOW, 0), (0, 0))), vwt)
        f2 = lambda a: a.reshape(B * S, a.shape[-1])
        xp = _merge(xp, f2(o_fox), f2(o_c), f2(o_s), f2(o_w), misc, go, wo)
        xp, gtail = _ffn(xp, g2, wgu, conv_w[l], cb, wd, gf, seq_len=S, final=final)
        tiles_per_seq = gtail.shape[0] // B
        conv_p = gtail.reshape(B, tiles_per_seq, 8, dff)[:, -1, 8 - (CONV_W - 1):, :]
        outs[0].append(fkv.reshape(B, S, 2, H_FOX, HD))
        outs[2].append(misc[:, :H_FOX].reshape(B, S, H_FOX))
        outs[4].append(nkv.reshape(B, S, 4, NSA_KV, HD))
        outs[6].append(wkv.reshape(B, S, 2, NSA_KV, HD)[:, -min(WINDOW, S):])
        outs[8].append(conv_p)

        qf, fkv, qn, nkv, wkv, misc = _proj(xs, g1, w_l, bf)
        s3 = lambda a: a.reshape(nb, tq, a.shape[-1])
        lfn = jnp.swapaxes(s3(misc)[:, :, :H_FOX], 1, 2)
        lfn = jnp.pad(lfn, ((0, 0), (0, 0), (0, PAGE - tq)))
        o_fox = _fox_sample(s3(qf), s3(fkv), lfn, cache_kt, cache_lft, page_table, l)
        kcw = _compress_sample(cache_nsa, page_table, l, cw)
        o_c, o_w, maskb = _ns_attend(s3(qn), kcw, win_t, s3(wkv), rbt, l, past)
        o_s = _ns_select(s3(qn), maskb, s3(nkv), rbt, cache_nsa, page_table, l, past)
        s2 = lambda a: a.reshape(nb * tq, a.shape[-1])
        xs = _merge(xs, s2(o_fox), s2(o_c), s2(o_s), s2(o_w), misc, go, wo)
        hist = state_conv[l]
        zero = jnp.zeros((nb, 1, dff), F32)
        hm1 = jnp.concatenate([hist[:, 1:2], zero, zero, zero][:tq], axis=1).reshape(nb * tq, dff)
        hm2 = jnp.concatenate([hist[:, 0:1], hist[:, 1:2], zero, zero][:tq], axis=1).reshape(nb * tq, dff)
        xs, gfull = _ffn(xs, g2, wgu, conv_w[l], cb, wd, gf, seq_len=None, final=final, hist=(hm1, hm2),
                         short_len=tq)
        win_new = jnp.concatenate([state_win_kv[l], wkv.reshape(nb, tq, 2, NSA_KV, HD)], axis=1)[:, -wb:]
        outs[1].append(fkv.reshape(nb, tq, 2, H_FOX, HD))
        outs[3].append(misc[:, :H_FOX].reshape(nb, tq, H_FOX))
        outs[5].append(nkv.reshape(nb, tq, 4, NSA_KV, HD))
        outs[7].append(win_new)
        outs[9].append(gfull.reshape(nb, tq, dff)[:, -(CONV_W - 1):])

    st = [jnp.stack(o) for o in outs]
    return (xp.reshape(B, S, D), xs.reshape(nb, tq, D),
            st[0], st[1], st[2], st[3], st[4], st[5], st[6], st[7], st[8], st[9])
```

```python
import functools
import math

import numpy as np
import jax
import jax.numpy as jnp
from jax import lax
from jax.experimental import pallas as pl
from jax.experimental.pallas import tpu as pltpu

F32 = jnp.float32
BF16 = jnp.bfloat16
I32 = jnp.int32

HD = 64
H_FOX = 8
H_NSA = 8
NSA_KV = 2
NSA_GH = 4
FOX_W = H_FOX * HD
NSA_W = H_NSA * HD
CMP_LEN = 32
CMP_STRIDE = 16
CMP_HID = 256
SEL_LEN = 64
N_SELECT = 16
WINDOW = 512
N_BUCKETS = 32
MAX_DISTANCE = 128
CONV_W = 3
PAGE = 128
EPS = 1e-6
NEG = -1e30
TINY = 1e-30
FORCE_SCORE = 1e4
SCALE = HD ** -0.5
LOG2E = math.log2(math.e)
ONES_ROWS = 8
LANES = 128
VMEM_LIMIT = 56 * 1024 * 1024

C_QF, C_FKV, C_QN, C_NKV, C_WKV, C_MISC, C_END = 0, 512, 1536, 2048, 2560, 2816, 2944
MISC_LOGF = 0
MISC_GATE = 8

AUG0 = HD
AUG1 = HD + 3

PAGES_PER_STEP = 32


def _bucket_thresholds():
    exact = N_BUCKETS // 2
    n = np.arange(1, 4 * MAX_DISTANCE, dtype=np.float64)
    far = exact + (np.log(n / exact) / math.log(MAX_DISTANCE / exact) * (N_BUCKETS - exact)).astype(np.int64)
    b = np.where(n < exact, n, np.minimum(far, N_BUCKETS - 1)).astype(np.int64)
    return [int(n[b >= k].min()) for k in range(exact + 1, N_BUCKETS)]


_THR = _bucket_thresholds()


def _cparams(sem):
    return pltpu.CompilerParams(dimension_semantics=sem, vmem_limit_bytes=VMEM_LIMIT)


def _dot(a, b):
    return jnp.dot(a, b, preferred_element_type=F32)


def _dot_nt(a, b):
    return lax.dot_general(a, b, (((1,), (1,)), ((), ())), preferred_element_type=F32)


def _split2(x):
    hi = x.astype(BF16)
    lo = (x - hi.astype(F32)).astype(BF16)
    return hi, lo


def _split3(x):
    hi = x.astype(BF16)
    r = x - hi.astype(F32)
    mid = r.astype(BF16)
    lo = (r - mid.astype(F32)).astype(BF16)
    return hi, mid, lo


def _dot3_l(x, m):
    hi, mid, lo = _split3(x)
    return _dot(hi, m) + _dot(mid, m) + _dot(lo, m)


def _dot3_r(m, x):
    hi, mid, lo = _split3(x)
    return _dot(m, hi) + _dot(m, mid) + _dot(m, lo)


def _dot2_l(x, m):
    hi, lo = _split2(x)
    return _dot(hi, m) + _dot(lo, m)


def _iota(shape, dim):
    return lax.broadcasted_iota(I32, shape, dim)


def _div(x, k):
    return lax.shift_right_arithmetic(x, jnp.int32(k.bit_length() - 1))


def _mod(x, k):
    return x & (k - 1)


def _rms(x, g):
    r = lax.rsqrt(jnp.mean(x * x, axis=-1, keepdims=True) + EPS)
    return (x * r) * g


def _bucket(d):
    far = jnp.full(d.shape, N_BUCKETS // 2, I32)
    for thr in _THR:
        far = far + (d >= thr).astype(I32)
    return jnp.where(d < N_BUCKETS // 2, d, far)


def _bias_lookup(bucket, rb_get):
    acc = jnp.zeros(bucket.shape, F32)
    for b in range(N_BUCKETS):
        acc = jnp.where(bucket == b, rb_get(b), acc)
    return acc


def _topk_mask(score, k, axis=1):
    idx = _iota(score.shape, axis).astype(F32)

    def body(_, c):
        work, sel = c
        mx = jnp.max(work, axis=axis, keepdims=True)
        first = jnp.min(jnp.where(work == mx, idx, 1e9), axis=axis, keepdims=True)
        hit = idx == first
        return jnp.where(hit, -3.0, work), jnp.where(hit, 1.0, sel)

    _, sel = lax.fori_loop(0, k, body, (score, jnp.zeros(score.shape, F32)))
    return sel


def _proj_kernel(x_ref, g_ref, w_ref, bf_ref, qf_ref, fkv_ref, qn_ref, nkv_ref, wkv_ref, misc_ref):
    h = _rms(x_ref[...], g_ref[...]).astype(BF16)
    qf_ref[...] = _dot(h, w_ref[:, C_QF:C_FKV])
    fkv_ref[...] = _dot(h, w_ref[:, C_FKV:C_QN])
    qn_ref[...] = _dot(h, w_ref[:, C_QN:C_NKV])
    nkv_ref[...] = _dot(h, w_ref[:, C_NKV:C_WKV])
    wkv_ref[...] = _dot(h, w_ref[:, C_WKV:C_MISC])
    z = _dot(h, w_ref[:, C_MISC:C_END]) + bf_ref[...]
    lane = _iota(z.shape, 1)
    logsig = jnp.minimum(z, 0.0) - jnp.log(1.0 + jnp.exp(-jnp.abs(z)))
    sig = 1.0 / (1.0 + jnp.exp(-z))
    misc_ref[...] = jnp.where(lane < MISC_GATE, logsig, jnp.where(lane < MISC_GATE + 3 * H_NSA, sig, 0.0))


def _proj(x2, g, w, bf):
    T, D = x2.shape
    tm = min(512, T)
    widths = (C_FKV - C_QF, C_QN - C_FKV, C_NKV - C_QN, C_WKV - C_NKV, C_MISC - C_WKV, C_END - C_MISC)
    return pl.pallas_call(
        _proj_kernel,
        grid=(T // tm,),
        in_specs=[pl.BlockSpec((tm, D), lambda i: (i, 0)),
                  pl.BlockSpec((1, D), lambda i: (0, 0)),
                  pl.BlockSpec((D, C_END), lambda i: (0, 0)),
                  pl.BlockSpec((1, LANES), lambda i: (0, 0))],
        out_specs=[pl.BlockSpec((tm, wd), lambda i: (i, 0)) for wd in widths],
        out_shape=[jax.ShapeDtypeStruct((T, wd), F32) for wd in widths],
        compiler_params=_cparams(("arbitrary",)),
        name="proj",
    )(x2, g, w, bf)


def _fox_prep_kernel(qf_ref, k_ref, v_ref, misc_ref, qa_ref, ka_ref, vb_ref, carry_ref):
    i = pl.program_id(1)
    tm = qf_ref.shape[1]

    @pl.when(i == 0)
    def _():
        carry_ref[...] = jnp.zeros_like(carry_ref)

    lane = _iota((tm, LANES), 1)
    lf = jnp.where(lane < H_FOX, misc_ref[0], 0.0)
    tril = (_iota((tm, tm), 0) >= _iota((tm, tm), 1)).astype(BF16)
    cum = _dot3_r(tril, lf) + carry_ref[0:1, :]
    carry_ref[...] = jnp.broadcast_to(cum[tm - 1:tm, :], carry_ref.shape)
    cl2 = cum * LOG2E
    hi = cl2.astype(BF16).astype(F32)
    r = cl2 - hi
    mid = r.astype(BF16).astype(F32)
    lo = (r - mid).astype(BF16).astype(F32)
    cc = (hi + pltpu.roll(mid, H_FOX, 1) + pltpu.roll(lo, 2 * H_FOX, 1)).astype(BF16)
    er = _iota((LANES, H_FOX * LANES), 0)
    ec = _iota((LANES, H_FOX * LANES), 1)
    part = _div(er, H_FOX)
    head = _mod(er, H_FOX)
    inb = er < 3 * H_FOX
    eq = jnp.where(inb & (ec == head * LANES + AUG0 + part), 1.0, 0.0).astype(BF16)
    ek = jnp.where(inb & (ec == head * LANES + AUG1 + part), -1.0, 0.0).astype(BF16)
    cl = _mod(_iota((1, H_FOX * LANES), 1), LANES)
    ones_q = jnp.where((cl >= AUG1) & (cl < AUG1 + 3), 1.0, 0.0)
    ones_k = jnp.where((cl >= AUG0) & (cl < AUG0 + 3), 1.0, 0.0)
    cols_q = _dot(cc, eq) + ones_q
    cols_k = _dot(cc, ek) + ones_k
    q = qf_ref[0]
    k = k_ref[0]
    for h in range(H_FOX):
        a = (h // 2) * LANES
        qt = q[:, a:a + LANES]
        kt = k[:, a:a + LANES]
        if h % 2:
            qt = pltpu.roll(qt, HD, 1)
            kt = pltpu.roll(kt, HD, 1)
        qa_ref[0, h] = jnp.where(lane < HD, qt * (SCALE * LOG2E), cols_q[:, h * LANES:(h + 1) * LANES]).astype(BF16)
        ka_ref[0, h] = jnp.where(lane < HD, kt, cols_k[:, h * LANES:(h + 1) * LANES]).astype(BF16)
    vb_ref[0] = v_ref[0].astype(BF16)


def _fox_prep(qf, fkv, misc):
    B, S, _ = qf.shape
    tm = min(512, S)
    return pl.pallas_call(
        _fox_prep_kernel,
        grid=(B, S // tm),
        in_specs=[pl.BlockSpec((1, tm, FOX_W), lambda b, i: (b, i, 0)),
                  pl.BlockSpec((1, tm, FOX_W), lambda b, i: (b, i, 0)),
                  pl.BlockSpec((1, tm, FOX_W), lambda b, i: (b, i, 1)),
                  pl.BlockSpec((1, tm, LANES), lambda b, i: (b, i, 0))],
        out_specs=[pl.BlockSpec((1, H_FOX, tm, LANES), lambda b, i: (b, 0, i, 0)),
                   pl.BlockSpec((1, H_FOX, tm, LANES), lambda b, i: (b, 0, i, 0)),
                   pl.BlockSpec((1, tm, FOX_W), lambda b, i: (b, i, 0))],
        out_shape=[jax.ShapeDtypeStruct((B, H_FOX, S, LANES), BF16),
                   jax.ShapeDtypeStruct((B, H_FOX, S, LANES), BF16),
                   jax.ShapeDtypeStruct((B, S, FOX_W), BF16)],
        scratch_shapes=[pltpu.VMEM((8, LANES), F32)],
        compiler_params=_cparams(("arbitrary", "arbitrary")),
        name="fox_prep",
    )(qf, fkv, fkv, misc)


def _softmax_init(m_ref, l_ref, acc_ref):
    m_ref[...] = jnp.full(m_ref.shape, NEG, F32)
    l_ref[...] = jnp.zeros(l_ref.shape, F32)
    acc_ref[...] = jnp.zeros(acc_ref.shape, F32)


def _softmax_update_t(s, vt, m_ref, acc_ref):
    m_old = m_ref[...]
    m_new = jnp.maximum(m_old, jnp.max(s, axis=0, keepdims=True))
    alpha = jnp.exp2(m_old - m_new)
    p = jnp.exp2(s - m_new).astype(BF16)
    acc_ref[...] = alpha * acc_ref[...] + _dot(vt, p)
    m_ref[...] = m_new


def _softmax_init_t(m_ref, acc_ref):
    m_ref[...] = jnp.full(m_ref.shape, NEG, F32)
    acc_ref[...] = jnp.zeros(acc_ref.shape, F32)


FOX_QUERY_TILE = 1024
FOX_KEY_TILE = 1024


def _fox_flash_kernel(qa_ref, ka_ref, vt_ref, o_ref, m0_ref, m1_ref, acc0_ref, acc1_ref):
    qi = pl.program_id(2)
    tq = qa_ref.shape[2]
    m_ref = (m0_ref, m1_ref)
    acc_ref = (acc0_ref, acc1_ref)
    for hh in range(2):
        _softmax_init_t(m_ref[hh], acc_ref[hh])

    tk = min(FOX_KEY_TILE, tq)
    nk = tq // tk

    def tile(kt, diag_block):
        ks = pl.multiple_of(kt * tk, tk)
        for hh in range(2):
            s = _dot_nt(ka_ref[0, hh, pl.ds(ks, tk), :], qa_ref[0, hh])
            if diag_block is not None:
                s = jnp.where(_iota(s.shape, 0) + diag_block * tk <= _iota(s.shape, 1), s, NEG)
            _softmax_update_t(s, vt_ref[0, hh, :, pl.ds(ks, tk)], m_ref[hh], acc_ref[hh])

    def full_tile(kt, c):
        tile(kt, None)
        return c

    lax.fori_loop(0, qi * nk, full_tile, 0)
    for kb in range(nk):
        tile(qi * nk + kb, kb)
    halves = []
    for hh in range(2):
        a = acc_ref[hh][...]
        halves.append(a[0:HD, :] / jnp.maximum(a[HD:HD + 1, :], TINY))
    o_ref[0] = jnp.concatenate(halves, axis=0).T


def _fox_flash(qa, ka, vt):
    B, H, S, _ = qa.shape
    tq = min(FOX_QUERY_TILE, S)
    rows = vt.shape[2]
    return pl.pallas_call(
        _fox_flash_kernel,
        grid=(B, H // 2, S // tq),
        in_specs=[pl.BlockSpec((1, 2, tq, LANES), lambda b, p, i: (b, p, i, 0)),
                  pl.BlockSpec((1, 2, S, LANES), lambda b, p, i: (b, p, 0, 0)),
                  pl.BlockSpec((1, 2, rows, S), lambda b, p, i: (b, p, 0, 0))],
        out_specs=pl.BlockSpec((1, tq, LANES), lambda b, p, i: (b, i, p)),
        out_shape=jax.ShapeDtypeStruct((B, S, FOX_W), F32),
        scratch_shapes=[pltpu.VMEM((1, tq), F32), pltpu.VMEM((1, tq), F32),
                        pltpu.VMEM((rows, tq), F32), pltpu.VMEM((rows, tq), F32)],
        compiler_params=_cparams(("arbitrary", "arbitrary", "arbitrary")),
        name="fox_flash",
    )(qa, ka, vt)


def _gelu_tanh(x):
    return 0.5 * x * (1.0 + jnp.tanh(math.sqrt(2.0 / math.pi) * (x + 0.044715 * (x * x * x))))


def _compress_core(x, pt, pb, w1t, w1b, b1, w2w):
    n = x.shape[0]
    a = _dot((x + pt).astype(BF16), w1t)
    b = _dot((x + pb).astype(BF16), w1b)
    h = a + pltpu.roll(b, n - 1, 0) + b1
    return _dot(_gelu_tanh(h).astype(BF16), w2w)


CMP_PAD_FRONT = 16
CMP_PAD_BACK = 112


def _store_cmp(out_ref, idx, res, n):
    out_ref[idx] = jnp.zeros(out_ref.shape[len(idx):], BF16)
    out_ref[idx + (slice(CMP_PAD_FRONT, CMP_PAD_FRONT + n), slice(None))] = res.astype(BF16)


def _compress_prompt_kernel(x_ref, pt_ref, pb_ref, w1t_ref, w1b_ref, b1_ref, w2w_ref, out_ref):
    n = x_ref.shape[1] // CMP_STRIDE
    x = jnp.concatenate([x_ref[0, pl.ds(j, n, stride=CMP_STRIDE), :] for j in range(CMP_STRIDE)], axis=1)
    res = _compress_core(x, pt_ref[0], pb_ref[0], w1t_ref[0], w1b_ref[0], b1_ref[0], w2w_ref[0])
    _store_cmp(out_ref, (0, 0), res, n)


def _compress_prompt(nkv, cw):
    B, S, _ = nkv.shape
    n = S // CMP_STRIDE
    npad = n + CMP_PAD_FRONT + CMP_PAD_BACK
    kx = CMP_STRIDE * LANES
    return pl.pallas_call(
        _compress_prompt_kernel,
        grid=(B, 2),
        in_specs=[pl.BlockSpec((1, S, LANES), lambda b, w: (b, 0, w)),
                  pl.BlockSpec((1, 1, kx), lambda b, w: (w, 0, 0)),
                  pl.BlockSpec((1, 1, kx), lambda b, w: (w, 0, 0)),
                  pl.BlockSpec((1, kx, 2 * CMP_HID), lambda b, w: (w, 0, 0)),
                  pl.BlockSpec((1, kx, 2 * CMP_HID), lambda b, w: (w, 0, 0)),
                  pl.BlockSpec((1, 1, 2 * CMP_HID), lambda b, w: (w, 0, 0)),
                  pl.BlockSpec((1, 2 * CMP_HID, NSA_W), lambda b, w: (w, 0, 0))],
        out_specs=pl.BlockSpec((1, 1, npad, NSA_W), lambda b, w: (b, w, 0, 0)),
        out_shape=jax.ShapeDtypeStruct((B, 2, npad, NSA_W), BF16),
        compiler_params=_cparams(("arbitrary", "arbitrary")),
        name="compress_prompt",
    )(nkv, cw["pt"], cw["pb"], cw["w1t"], cw["w1b"], cw["b1"], cw["w2w"])


def _compress_sample_kernel(pt_tab, *refs):
    P = PAGES_PER_STEP
    pages = (refs[:P], refs[P:2 * P])
    pt_ref, pb_ref, w1t_ref, w1b_ref, b1_ref, w2w_ref, out_ref, x_sc, rows_sc = refs[2 * P:]
    s = pl.program_id(1)
    n = x_sc.shape[1]
    rows = P * (PAGE // CMP_STRIDE)
    r0 = pl.multiple_of(s * rows, rows)
    for w in range(2):
        for k, pg in enumerate(pages[w]):
            rows_sc[k * PAGE:(k + 1) * PAGE, :] = pg[...].reshape(LANES, PAGE).T
        for j in range(CMP_STRIDE):
            x_sc[w, pl.ds(r0, rows), j * LANES:(j + 1) * LANES] = rows_sc[pl.ds(j, rows, stride=CMP_STRIDE), :]

    @pl.when(s == pl.num_programs(1) - 1)
    def _():
        for w in range(2):
            res = _compress_core(x_sc[w], pt_ref[w], pb_ref[w], w1t_ref[w], w1b_ref[w], b1_ref[w], w2w_ref[w])
            _store_cmp(out_ref, (0, w), res, n)


def _compress_sample(cache_nsa, page_table, layer, cw):
    nb, npages = page_table.shape
    P = PAGES_PER_STEP
    n = npages * (PAGE // CMP_STRIDE)
    npad = n + CMP_PAD_FRONT + CMP_PAD_BACK
    kx = CMP_STRIDE * LANES

    def page_spec(k, w):
        return pl.BlockSpec((None, None, None, NSA_KV, HD, PAGE),
                            lambda b, s, pt: (layer, pt[b, s * P + k], w, 0, 0, 0))

    const3 = lambda b, s, pt: (0, 0, 0)
    return pl.pallas_call(
        _compress_sample_kernel,
        grid_spec=pltpu.PrefetchScalarGridSpec(
            num_scalar_prefetch=1,
            grid=(nb, npages // P),
            in_specs=[page_spec(k, 0) for k in range(P)] + [page_spec(k, 1) for k in range(P)] + [
                pl.BlockSpec((2, 1, kx), const3), pl.BlockSpec((2, 1, kx), const3),
                pl.BlockSpec((2, kx, 2 * CMP_HID), const3), pl.BlockSpec((2, kx, 2 * CMP_HID), const3),
                pl.BlockSpec((2, 1, 2 * CMP_HID), const3), pl.BlockSpec((2, 2 * CMP_HID, NSA_W), const3)],
            out_specs=pl.BlockSpec((1, 2, npad, NSA_W), lambda b, s, pt: (b, 0, 0, 0)),
            scratch_shapes=[pltpu.VMEM((2, n, kx), F32), pltpu.VMEM((P * PAGE, LANES), F32)]),
        out_shape=jax.ShapeDtypeStruct((nb, 2, npad, NSA_W), BF16),
        compiler_params=_cparams(("arbitrary", "arbitrary")),
        name="compress_sample",
    )(page_table, *([cache_nsa] * (2 * P)), cw["pt"], cw["pb"], cw["w1t"], cw["w1b"], cw["b1"], cw["w2w"])


def _nsa_prep_kernel(sel_ref, win_ref, ksa_ref, kwa_ref):
    i = pl.program_id(1)
    tm = sel_ref.shape[1]
    lane = _iota((tm, LANES), 1)
    pos = i * tm + _iota((tm, LANES), 0)
    onehot = jnp.where(_div(pos, SEL_LEN) == lane, 1.0, 0.0)
    ks = sel_ref[0][:, 0:LANES]
    kw = win_ref[0][:, 0:LANES]
    rks = pltpu.roll(ks, HD, 1)
    rkw = pltpu.roll(kw, HD, 1)
    for g in range(NSA_KV):
        left = jnp.where(lane < HD, ks if g == 0 else rks, jnp.where(lane < HD + 3, 1.0, 0.0))
        ksa_ref[0, g] = jnp.concatenate([left, onehot], axis=1).astype(BF16)
        kwa_ref[0, g] = jnp.where(lane < HD, kw if g == 0 else rkw, 0.0).astype(BF16)


def _nsa_prep(nkv, wkv):
    B, S, _ = nkv.shape
    tm = min(512, S)
    spec128 = pl.BlockSpec((1, NSA_KV, tm, LANES), lambda b, i: (b, 0, i, 0))
    return pl.pallas_call(
        _nsa_prep_kernel,
        grid=(B, S // tm),
        in_specs=[pl.BlockSpec((1, tm, 2 * LANES), lambda b, i: (b, i, 1)),
                  pl.BlockSpec((1, tm, 2 * LANES), lambda b, i: (b, i, 0))],
        out_specs=[pl.BlockSpec((1, NSA_KV, tm, 2 * LANES), lambda b, i: (b, 0, i, 0)), spec128],
        out_shape=[jax.ShapeDtypeStruct((B, NSA_KV, S, 2 * LANES), BF16),
                   jax.ShapeDtypeStruct((B, NSA_KV, S, LANES), BF16)],
        compiler_params=_cparams(("arbitrary", "arbitrary")),
        name="nsa_prep",
    )(nkv, wkv)


def _overlap(i_blk, j_blk):
    start = i_blk * CMP_STRIDE
    return (start < j_blk * SEL_LEN + SEL_LEN) & (start + CMP_LEN > j_blk * SEL_LEN)


NEAR_BACK = 16


def _nsa_cmp_kernel(rb_ref, q_ref, kcw_ref, oc_ref, mq_ref, fc_ref, qs_ref):
    b = pl.program_id(0)
    i = pl.program_id(1)
    tq = q_ref.shape[1]
    n = kcw_ref.shape[2] - CMP_PAD_FRONT - CMP_PAD_BACK
    n_sel = n * CMP_STRIDE // SEL_LEN
    k_eff = min(N_SELECT, n_sel)

    @pl.when((b == 0) & (i == 0))
    def _():
        dist = _iota((tq, LANES), 0) + (NEAR_BACK * CMP_STRIDE - (CMP_LEN - 1)) - CMP_STRIDE * _iota((tq, LANES), 1)
        bk = _bucket(jnp.clip(dist, 0, MAX_DISTANCE - 1))
        for h in range(H_NSA):
            fc_ref[h // NSA_GH, (h % NSA_GH) * tq:(h % NSA_GH + 1) * tq, :] = jnp.where(
                dist >= 0, _bias_lookup(bk, lambda bb, h=h: rb_ref[bb, h]), NEG)

    qs = i * tq
    i0 = qs // CMP_STRIDE - NEAR_BACK
    st = pl.multiple_of(qs // CMP_STRIDE, 16)
    mrows = NSA_GH * tq
    farmask = _iota((mrows, n), 1) < i0
    nearmask = (_iota((mrows, LANES), 1) + i0) >= 0
    rowhead = _div(_iota((mrows, 1), 0), tq)
    ps_far = []
    ps_near = []
    outs = []
    for g in range(NSA_KV):
        _stack_heads(q_ref[0][:, g * 2 * LANES:(g + 1) * 2 * LANES], jnp.zeros((NSA_GH, LANES), F32), None, qs_ref)
        qq = qs_ref[...]
        gl = slice(g * NSA_GH * HD, g * NSA_GH * HD + LANES)
        kfar = kcw_ref[0, 0, CMP_PAD_FRONT:CMP_PAD_FRONT + n, gl]
        vfar = kcw_ref[0, 1, CMP_PAD_FRONT:CMP_PAD_FRONT + n, gl]
        knear = kcw_ref[0, 0, pl.ds(st, LANES), gl]
        vnear = kcw_ref[0, 1, pl.ds(st, LANES), gl]
        bfar = jnp.zeros((mrows, 1), F32)
        for hh in range(NSA_GH):
            bfar = jnp.where(rowhead == hh, rb_ref[N_BUCKETS - 1, g * NSA_GH + hh], bfar)
        sf = jnp.where(farmask, _dot_nt(qq, kfar) + bfar, NEG)
        sn = jnp.where(nearmask, _dot_nt(qq, knear) + fc_ref[g], NEG)
        m = jnp.maximum(jnp.max(sf, axis=1, keepdims=True), jnp.max(sn, axis=1, keepdims=True))
        pf = jnp.where(sf > 0.5 * NEG, jnp.exp(sf - m), 0.0)
        pn = jnp.where(sn > 0.5 * NEG, jnp.exp(sn - m), 0.0)
        l = jnp.sum(pf, axis=1, keepdims=True) + jnp.sum(pn, axis=1, keepdims=True)
        inv = 1.0 / jnp.maximum(l, TINY)
        pf = pf * inv
        pn = pn * inv
        outs.append(_unstack_heads(_dot(pf.astype(BF16), vfar) + _dot(pn.astype(BF16), vnear), tq))
        ps_far.append(sum(pf[hh * tq:(hh + 1) * tq] for hh in range(NSA_GH)))
        ps_near.append(sum(pn[hh * tq:(hh + 1) * tq] for hh in range(NSA_GH)))
    oc_ref[0] = jnp.concatenate(outs, axis=1)

    mov_far = jnp.where(_overlap(_iota((n, LANES), 0), _iota((n, LANES), 1)), 1.0, 0.0).astype(BF16)
    mov_near = jnp.where(_overlap(_iota((LANES, LANES), 0) + i0, _iota((LANES, LANES), 1)), 1.0, 0.0).astype(BF16)
    qpos = qs + _iota((tq, LANES), 0)
    jb = _iota((tq, LANES), 1)
    qblk = _div(qpos, SEL_LEN)
    valid = jb * SEL_LEN <= qpos
    forced = (jb == 0) | (jb == qblk) | (jb == qblk - 1)
    scores = []
    for g in range(NSA_KV):
        p_slc = _dot2_l(ps_far[g], mov_far) + _dot2_l(ps_near[g], mov_near)
        scores.append(jnp.where(forced, FORCE_SCORE, jnp.where(valid, p_slc, -1.0)))
    st = jnp.concatenate([sc.T for sc in scores], axis=1)
    keep = jnp.where((_topk_mask(st, k_eff, axis=0) > 0.5) & (st >= 0.0), 0.0, NEG)
    for g in range(NSA_KV):
        mq_ref[0, g] = keep[:, g * tq:(g + 1) * tq].T.astype(BF16)


def _nsa_cmp(rb, qn, kcw):
    B, S, _ = qn.shape
    tq = min(256, S)
    npad = kcw.shape[2]
    return pl.pallas_call(
        _nsa_cmp_kernel,
        grid=(B, S // tq),
        in_specs=[pl.BlockSpec(memory_space=pltpu.SMEM),
                  pl.BlockSpec((1, tq, NSA_W), lambda b, i: (b, i, 0)),
                  pl.BlockSpec((1, 2, npad, NSA_W), lambda b, i: (b, 0, 0, 0))],
        out_specs=[pl.BlockSpec((1, tq, NSA_W), lambda b, i: (b, i, 0)),
                   pl.BlockSpec((1, NSA_KV, tq, LANES), lambda b, i: (b, 0, i, 0))],
        out_shape=[jax.ShapeDtypeStruct((B, S, NSA_W), F32),
                   jax.ShapeDtypeStruct((B, NSA_KV, S, LANES), BF16)],
        scratch_shapes=[pltpu.VMEM((NSA_KV, NSA_GH * tq, LANES), F32), pltpu.VMEM((NSA_GH * tq, LANES), BF16)],
        compiler_params=_cparams(("arbitrary", "arbitrary")),
        name="nsa_cmp",
    )(rb, qn, kcw)


def _stack_heads(q, g_rows, extra, qs_ref, scale=SCALE):
    tq = q.shape[0]
    lane = _iota((tq, LANES), 1)
    for hh in range(NSA_GH):
        a = (hh // 2) * LANES
        t = q[:, a:a + LANES]
        if hh % 2:
            t = pltpu.roll(t, HD, 1)
        left = jnp.where(lane < HD, t * scale, g_rows[hh:hh + 1, :]).astype(BF16)
        if extra is None:
            qs_ref[hh * tq:(hh + 1) * tq, :] = left
        else:
            qs_ref[hh * tq:(hh + 1) * tq, :] = jnp.concatenate([left, extra], axis=1)


def _unstack_heads(a, tq):
    lane = _iota((tq, LANES), 1)
    p0 = jnp.where(lane < HD, a[0:tq], a[tq:2 * tq])
    p1 = jnp.where(lane < HD, a[2 * tq:3 * tq], a[3 * tq:4 * tq])
    return jnp.concatenate([p0, p1], axis=1)


SEL_BACK = 128
NSEL_CHAINS = 1


def _nsa_sel_kernel(rb_ref, q_ref, mq_ref, brow_ref, ksa_ref, vt_ref, o_ref, dn_ref, qs_ref, m_ref, acc_ref):
    b = pl.program_id(0)
    g = pl.program_id(1)
    i = pl.program_id(2)
    tq = q_ref.shape[1]
    wn = tq + SEL_BACK

    @pl.when((b == 0) & (g == 0) & (i == 0))
    def _():
        dist = _iota((wn, tq), 1) + SEL_BACK - _iota((wn, tq), 0)
        bk = _bucket(jnp.clip(dist, 0, MAX_DISTANCE - 1))
        for h in range(H_NSA):
            far = rb_ref[N_BUCKETS - 1, h]
            val = (_bias_lookup(bk, lambda bb, h=h: rb_ref[bb, h]) - far) * LOG2E
            dn_ref[h // NSA_GH, :, (h % NSA_GH) * tq:(h % NSA_GH + 1) * tq] = jnp.where(dist >= 0, val, NEG)

    qs = i * tq
    _stack_heads(q_ref[0], brow_ref[0], mq_ref[0, 0], qs_ref, scale=SCALE * LOG2E)
    _softmax_init_t(m_ref, acc_ref)
    nch = m_ref.shape[0]
    hw = (NSA_GH // nch) * tq

    def tile(start, size, bias):
        k = ksa_ref[0, 0, pl.ds(start, size), :]
        vt = vt_ref[0, 0, :, pl.ds(start, size)]
        for half in range(nch):
            s = _dot_nt(k, qs_ref[half * hw:(half + 1) * hw, :])
            if bias is not None:
                s = s + bias(half)
            _softmax_update_t(s, vt, m_ref.at[half], acc_ref.at[half])

    @pl.when(i == 0)
    def _():
        tile(0, tq, lambda half: dn_ref[g, SEL_BACK:, half * hw:(half + 1) * hw])

    @pl.when(i > 0)
    def _():
        tile(pl.multiple_of(qs - SEL_BACK, LANES), wn, lambda half: dn_ref[g, :, half * hw:(half + 1) * hw])
        tile(pl.multiple_of(qs - tq, LANES), tq - SEL_BACK, None)

    n_far = jnp.maximum(i - 1, 0)

    def far_pair(kt, c):
        tile(pl.multiple_of(kt * 2 * tq, 2 * tq), 2 * tq, None)
        return c

    lax.fori_loop(0, n_far // 2, far_pair, 0)

    @pl.when(n_far % 2 == 1)
    def _():
        tile(pl.multiple_of((n_far - 1) * tq, tq), tq, None)

    parts = []
    for half in range(nch):
        acc = acc_ref[half]
        a = acc[0:HD, :] / jnp.maximum(acc[HD:HD + 1, :], TINY)
        parts += [a[:, j * tq:(j + 1) * tq] for j in range(NSA_GH // nch)]
    o_ref[0] = jnp.concatenate(parts, axis=0).T


def _nsa_sel(rb, qn, maskq, brow, ksa, vst):
    B, S, _ = qn.shape
    tq = min(512, S)
    G = NSA_KV
    return pl.pallas_call(
        _nsa_sel_kernel,
        grid=(B, G, S // tq),
        in_specs=[pl.BlockSpec(memory_space=pltpu.SMEM),
                  pl.BlockSpec((1, tq, 2 * LANES), lambda b, g, i: (b, i, g)),
                  pl.BlockSpec((1, 1, tq, LANES), lambda b, g, i: (b, g, i, 0)),
                  pl.BlockSpec((1, NSA_GH, LANES), lambda b, g, i: (g, 0, 0)),
                  pl.BlockSpec((1, 1, S, 2 * LANES), lambda b, g, i: (b, g, 0, 0)),
                  pl.BlockSpec((1, 1, HD + ONES_ROWS, S), lambda b, g, i: (b, g, 0, 0))],
        out_specs=pl.BlockSpec((1, tq, 2 * LANES), lambda b, g, i: (b, i, g)),
        out_shape=jax.ShapeDtypeStruct((B, S, NSA_W), F32),
        scratch_shapes=[pltpu.VMEM((G, tq + SEL_BACK, NSA_GH * tq), F32),
                        pltpu.VMEM((NSA_GH * tq, 2 * LANES), BF16),
                        pltpu.VMEM((NSEL_CHAINS, 1, NSA_GH // NSEL_CHAINS * tq), F32),
                        pltpu.VMEM((NSEL_CHAINS, HD + ONES_ROWS, NSA_GH // NSEL_CHAINS * tq), F32)],
        compiler_params=_cparams(("arbitrary", "arbitrary", "arbitrary")),
        name="nsa_sel",
    )(rb, qn, maskq, brow, ksa, vst)


def _nsa_win_kernel(rb_ref, q_ref, kwa_ref, vt_ref, o_ref, dw_ref, qs_ref):
    b = pl.program_id(0)
    g = pl.program_id(1)
    i = pl.program_id(2)
    tq = q_ref.shape[1]
    wk = tq + WINDOW

    @pl.when((b == 0) & (g == 0) & (i == 0))
    def _():
        dist = _iota((wk, tq), 1) + WINDOW - _iota((wk, tq), 0)
        bk = _bucket(jnp.clip(dist, 0, MAX_DISTANCE - 1))
        ok = (dist >= 0) & (dist < WINDOW)
        for h in range(H_NSA):
            val = _bias_lookup(bk, lambda bb, h=h: rb_ref[bb, h]) * LOG2E
            dw_ref[h // NSA_GH, :, (h % NSA_GH) * tq:(h % NSA_GH + 1) * tq] = jnp.where(ok, val, NEG)

    qs = pl.multiple_of(i * tq, tq)
    _stack_heads(q_ref[0], jnp.zeros((NSA_GH, LANES), F32), None, qs_ref, scale=SCALE * LOG2E)
    s = _dot_nt(kwa_ref[0, 0, pl.ds(qs, wk), :], qs_ref[...]) + dw_ref[g]
    s = jnp.where(_iota(s.shape, 0) + qs >= WINDOW, s, NEG)
    m = jnp.max(s, axis=0, keepdims=True)
    p = jnp.exp2(s - m).astype(BF16)
    acc = _dot(vt_ref[0, 0, :, pl.ds(qs, wk)], p)
    a = acc[0:HD, :] / jnp.maximum(acc[HD:HD + 1, :], TINY)
    o_ref[0] = jnp.concatenate([a[:, hh * tq:(hh + 1) * tq] for hh in range(NSA_GH)], axis=0).T


def _nsa_win(rb, qn, kwa_p, vwt_p):
    B, S, _ = qn.shape
    tq = min(256, S)
    G = NSA_KV
    sp = kwa_p.shape[2]
    return pl.pallas_call(
        _nsa_win_kernel,
        grid=(B, G, S // tq),
        in_specs=[pl.BlockSpec(memory_space=pltpu.SMEM),
                  pl.BlockSpec((1, tq, 2 * LANES), lambda b, g, i: (b, i, g)),
                  pl.BlockSpec((1, 1, sp, LANES), lambda b, g, i: (b, g, 0, 0)),
                  pl.BlockSpec((1, 1, HD + ONES_ROWS, sp), lambda b, g, i: (b, g, 0, 0))],
        out_specs=pl.BlockSpec((1, tq, 2 * LANES), lambda b, g, i: (b, i, g)),
        out_shape=jax.ShapeDtypeStruct((B, S, NSA_W), F32),
        scratch_shapes=[pltpu.VMEM((G, tq + WINDOW, NSA_GH * tq), F32),
                        pltpu.VMEM((NSA_GH * tq, LANES), BF16)],
        compiler_params=_cparams(("arbitrary", "arbitrary", "arbitrary")),
        name="nsa_win",
    )(rb, qn, kwa_p, vwt_p)


def _merge_kernel(x_ref, of_ref, oc_ref, os_ref, ow_ref, misc_ref, g_ref, w_ref, out_ref):
    tm = x_ref.shape[0]
    hi, lo = _split2(misc_ref[...])
    er = _iota((LANES, NSA_W), 0)
    ec = _iota((LANES, NSA_W), 1)
    onsa = jnp.zeros((tm, NSA_W), F32)
    for k, o_ref in enumerate((oc_ref, os_ref, ow_ref)):
        e = jnp.where(er == MISC_GATE + k * H_NSA + _div(ec, HD), 1.0, 0.0).astype(BF16)
        onsa = onsa + (_dot(hi, e) + _dot(lo, e)) * o_ref[...]
    g = g_ref[...]
    a = _rms(of_ref[...], g[:, :FOX_W]).astype(BF16)
    c = _rms(onsa, g[:, FOX_W:]).astype(BF16)
    out_ref[...] = x_ref[...] + _dot(a, w_ref[0:FOX_W, :]) + _dot(c, w_ref[FOX_W:, :])


def _merge(x2, ofox, oc, os_, ow, misc, g, w):
    T, D = x2.shape
    tm = min(512, T)
    row = lambda wd: pl.BlockSpec((tm, wd), lambda i: (i, 0))
    return pl.pallas_call(
        _merge_kernel,
        grid=(T // tm,),
        in_specs=[row(D), row(FOX_W), row(NSA_W), row(NSA_W), row(NSA_W), row(LANES),
                  pl.BlockSpec((1, D), lambda i: (0, 0)),
                  pl.BlockSpec((D, D), lambda i: (0, 0))],
        out_specs=row(D),
        out_shape=jax.ShapeDtypeStruct((T, D), F32),
        compiler_params=_cparams(("arbitrary",)),
        name="merge",
    )(x2, ofox, oc, os_, ow, misc, g, w)


FF_CHUNK = 1408


def _ffn_kernel(*refs, seq_len, short_len, final):
    if seq_len is None:
        (x_ref, g2_ref, wg_ref, wu_ref, cw_ref, cb_ref, wd_ref, gf_ref, hm1_ref, hm2_ref,
         out_ref, gt_ref, h_sc, acc_sc) = refs
    else:
        (x_ref, g2_ref, wg_ref, wu_ref, cw_ref, cb_ref, wd_ref, gf_ref,
         out_ref, gt_ref, h_sc, acc_sc, carry_sc) = refs
    i = pl.program_id(0)
    j = pl.program_id(1)
    tm = x_ref.shape[0]

    @pl.when(j == 0)
    def _():
        h_sc[...] = _rms(x_ref[...], g2_ref[...]).astype(BF16)
        acc_sc[...] = jnp.zeros_like(acc_sc)

    h = h_sc[...]
    gch = _dot(h, wg_ref[...])
    u = _dot(h, wu_ref[...])
    r1 = pltpu.roll(gch, 1, 0)
    r2 = pltpu.roll(gch, 2, 0)
    row = _iota(gch.shape, 0)
    if seq_len is None:
        t = _mod(row, short_len)
        m1 = jnp.where(t == 0, hm1_ref[...], r1)
        m2 = jnp.where(t < 2, hm2_ref[...], r2)
        gt_ref[...] = gch
    else:
        first = (i % (seq_len // tm)) == 0
        c = jnp.where(first, 0.0, carry_sc[j])
        m1 = jnp.where(row == 0, c[1:2, :], r1)
        m2 = jnp.where(row == 0, c[0:1, :], jnp.where(row == 1, c[1:2, :], r2))
        carry_sc[j, 0:2, :] = gch[tm - 2:tm, :]
        gt_ref[0] = gch[tm - 8:tm, :]
    cw = cw_ref[...]
    gc = cb_ref[...] + cw[0:1, :] * m2 + cw[1:2, :] * m1 + cw[2:3, :] * gch
    act = gc * (1.0 / (1.0 + jnp.exp(-gc)))
    acc_sc[...] += _dot((act * u).astype(BF16), wd_ref[...])

    @pl.when(j == pl.num_programs(1) - 1)
    def _():
        y = x_ref[...] + acc_sc[...]
        if final:
            y = _rms(y, gf_ref[...])
        out_ref[...] = y


def _ffn(x2, g2, wgu, cw, cb, wd, gf, *, seq_len, final, hist=None, short_len=None):
    T, D = x2.shape
    dff = wd.shape[0]
    fc = FF_CHUNK if dff % FF_CHUNK == 0 else LANES
    nff = dff // fc
    tm = min(512, T) if seq_len is not None else T
    nt = T // tm
    in_specs = [pl.BlockSpec((tm, D), lambda i, j: (i, 0)),
                pl.BlockSpec((1, D), lambda i, j: (0, 0)),
                pl.BlockSpec((D, fc), lambda i, j: (0, j)),
                pl.BlockSpec((D, fc), lambda i, j: (0, nff + j)),
                pl.BlockSpec((CONV_W, fc), lambda i, j: (0, j)),
                pl.BlockSpec((1, fc), lambda i, j: (0, j)),
                pl.BlockSpec((fc, D), lambda i, j: (j, 0)),
                pl.BlockSpec((1, D), lambda i, j: (0, 0))]
    args = [x2, g2, wgu, wgu, cw, cb, wd, gf]
    scratch = [pltpu.VMEM((tm, D), BF16), pltpu.VMEM((tm, D), F32)]
    if seq_len is None:
        in_specs += [pl.BlockSpec((tm, fc), lambda i, j: (i, j)), pl.BlockSpec((tm, fc), lambda i, j: (i, j))]
        args += list(hist)
        gt_spec = pl.BlockSpec((tm, fc), lambda i, j: (i, j))
        gt_shape = jax.ShapeDtypeStruct((T, dff), F32)
    else:
        scratch.append(pltpu.VMEM((nff, 8, fc), F32))
        gt_spec = pl.BlockSpec((1, 8, fc), lambda i, j: (i, 0, j))
        gt_shape = jax.ShapeDtypeStruct((nt, 8, dff), F32)
    return pl.pallas_call(
        functools.partial(_ffn_kernel, seq_len=seq_len, short_len=short_len, final=final),
        grid=(nt, nff),
        in_specs=in_specs,
        out_specs=[pl.BlockSpec((tm, D), lambda i, j: (i, 0)), gt_spec],
        out_shape=[jax.ShapeDtypeStruct((T, D), F32), gt_shape],
        scratch_shapes=scratch,
        compiler_params=_cparams(("arbitrary", "arbitrary")),
        name="ffn",
    )(*args)


def _rows_th(q):
    tq = q.shape[0]
    rows = jnp.concatenate([jnp.broadcast_to(q[t:t + 1, :], (8, q.shape[1])) for t in range(tq)], axis=0)
    keep = _div(_iota(rows.shape, 1), HD) == _mod(_iota(rows.shape, 0), 8)
    return jnp.where(keep, rows * SCALE, 0.0).astype(BF16)


def _diag_rows(o_ref, o32):
    keep = _div(_iota(o32.shape, 1), HD) == _mod(_iota(o32.shape, 0), 8)
    od = jnp.where(keep, o32, 0.0)
    for t in range(o32.shape[0] // 8):
        o_ref[0, t:t + 1, :] = jnp.sum(od[t * 8:(t + 1) * 8, :], axis=0, keepdims=True)


FOX_PAGES = 32


def _fox_sample_kernel(pt_tab, *refs):
    P = FOX_PAGES
    q_ref, kvn_ref, lfn_ref = refs[0:3]
    k_refs = refs[3:3 + P]
    v_refs = refs[3 + P:3 + 2 * P]
    lf_refs = refs[3 + 2 * P:3 + 3 * P]
    o_ref, q_sc, m_ref, l_ref, acc_ref, carry_ref, new_sc = refs[3 + 3 * P:]
    s = pl.program_id(1)
    tq = q_ref.shape[1]
    nr = tq * 8

    @pl.when(s == 0)
    def _():
        q_sc[...] = _rows_th(q_ref[0])
        _softmax_init(m_ref, l_ref, acc_ref)
        carry_ref[...] = jnp.zeros_like(carry_ref)
        new_sc[...] = jnp.zeros_like(new_sc)
        new_sc[0:tq, :] = kvn_ref[0]

    qq = q_sc[...]
    triu = (_iota((PAGE, PAGE), 0) <= _iota((PAGE, PAGE), 1)).astype(BF16)

    def attend(score_fns, value_fns, lfs, extra_mask):
        off = carry_ref[...]
        sc = []
        within = _dot3_l(jnp.concatenate(lfs, axis=0), triu)
        for k, fn in enumerate(score_fns):
            cum = within[H_FOX * k:H_FOX * (k + 1), :] + off
            off = jnp.broadcast_to(cum[:, PAGE - 1:PAGE], cum.shape)
            sc.append(fn() - jnp.concatenate([cum] * tq, axis=0))
        carry_ref[...] = off
        sc = jnp.concatenate(sc, axis=1)
        if extra_mask is not None:
            sc = jnp.where(extra_mask, sc, NEG)
        m_old = m_ref[...]
        m_new = jnp.maximum(m_old, jnp.max(sc, axis=1, keepdims=True))
        alpha = jnp.exp(m_old - m_new)
        p = jnp.exp(sc - m_new)
        l_ref[...] = alpha * l_ref[...] + jnp.sum(p, axis=1, keepdims=True)
        acc = alpha * acc_ref[...]
        for k, fn in enumerate(value_fns):
            acc = acc + fn(p[:, k * PAGE:(k + 1) * PAGE].astype(BF16))
        acc_ref[...] = acc
        m_ref[...] = m_new

    attend([lambda r=r: _dot(qq, r[...].reshape(FOX_W, PAGE).astype(BF16)) for r in k_refs],
           [lambda p, r=r: _dot_nt(p, r[...].reshape(FOX_W, PAGE).astype(BF16)) for r in v_refs],
           [r[...] for r in lf_refs], None)

    @pl.when(s == pl.num_programs(1) - 1)
    def _():
        key = _iota((nr, PAGE), 1)
        ok = (key < tq) & (key <= _div(_iota((nr, PAGE), 0), 8))
        new = new_sc[...]
        attend([lambda: _dot_nt(qq, new[:, :FOX_W].astype(BF16))],
               [lambda p: _dot(p, new[:, FOX_W:].astype(BF16))], [lfn_ref[0]], ok)
        _diag_rows(o_ref, acc_ref[...] / jnp.maximum(l_ref[...], TINY))


def _fox_sample(qf, fkv_new, lfn_t, cache_kt, cache_lft, page_table, layer):
    nb, npages = page_table.shape
    tq = qf.shape[1]
    P = FOX_PAGES
    kvw = 2 * FOX_W
    seq = lambda b, s, pt: (b, 0, 0)

    def kv_spec(k, c):
        return pl.BlockSpec((None, None, None, H_FOX, HD, PAGE),
                            lambda b, s, pt: (layer, pt[b, s * P + k], c, 0, 0, 0))

    def lf_spec(k):
        return pl.BlockSpec((None, None, H_FOX, PAGE), lambda b, s, pt: (layer, pt[b, s * P + k], 0, 0))

    return pl.pallas_call(
        _fox_sample_kernel,
        grid_spec=pltpu.PrefetchScalarGridSpec(
            num_scalar_prefetch=1,
            grid=(nb, npages // P),
            in_specs=[pl.BlockSpec((1, tq, FOX_W), seq), pl.BlockSpec((1, tq, kvw), seq),
                      pl.BlockSpec((1, H_FOX, PAGE), seq)]
            + [kv_spec(k, 0) for k in range(P)] + [kv_spec(k, 1) for k in range(P)]
            + [lf_spec(k) for k in range(P)],
            out_specs=pl.BlockSpec((1, tq, FOX_W), seq),
            scratch_shapes=[pltpu.VMEM((tq * 8, FOX_W), BF16),
                            pltpu.VMEM((tq * 8, 1), F32), pltpu.VMEM((tq * 8, 1), F32),
                            pltpu.VMEM((tq * 8, FOX_W), F32),
                            pltpu.VMEM((H_FOX, PAGE), F32),
                            pltpu.VMEM((PAGE, kvw), F32)]),
        out_shape=jax.ShapeDtypeStruct((nb, tq, FOX_W), F32),
        compiler_params=_cparams(("arbitrary", "arbitrary")),
        name="fox_sample",
    )(page_table, qf, fkv_new, lfn_t, *([cache_kt] * (2 * P)), *([cache_lft] * P))


def _place_wide():
    r = _iota((LANES, NSA_W), 0)
    c = _iota((LANES, NSA_W), 1)
    return jnp.where(r == _div(c, NSA_GH * HD) * HD + _mod(c, HD), 1.0, 0.0).astype(BF16)


def _rb_col(rbt_ref):
    return lambda bb: rbt_ref[:, bb:bb + 1]


def _fold_heads():
    r = _iota((NSA_W, LANES), 0)
    c = _iota((NSA_W, LANES), 1)
    return jnp.where(c == _div(r, NSA_GH * HD) * HD + _mod(r, HD), 1.0, 0.0).astype(BF16)


def _ns_attend_kernel(q_ref, kcw_ref, kwt_ref, vwt_ref, wn_ref, rbt_ref, oc_ref, ow_ref, mb_ref, new_sc, *, past):
    tq = q_ref.shape[1]
    nr = tq * 8
    n = kcw_ref.shape[2] - CMP_PAD_FRONT - CMP_PAD_BACK
    n_selp = past // SEL_LEN
    wb = kwt_ref.shape[2]
    qq = _rows_th(q_ref[0])
    kc = kcw_ref[0, 0, CMP_PAD_FRONT:CMP_PAD_FRONT + n, :]
    vc = kcw_ref[0, 1, CMP_PAD_FRONT:CMP_PAD_FRONT + n, :]
    trow = _div(_iota((nr, n), 0), 8)
    dist = past + trow - CMP_STRIDE * _iota((nr, n), 1) - (CMP_LEN - 1)
    bias = _bias_lookup(_bucket(jnp.clip(dist, 0, MAX_DISTANCE - 1)), _rb_col(rbt_ref))
    ok = dist >= 0
    s = jnp.where(ok, _dot_nt(qq, kc) + bias, NEG)
    m = jnp.max(s, axis=1, keepdims=True)
    p = jnp.where(ok, jnp.exp(s - m), 0.0)
    p = p / jnp.maximum(jnp.sum(p, axis=1, keepdims=True), TINY)
    _diag_rows(oc_ref, _dot(p.astype(BF16), vc))
    mov = jnp.where(_overlap(_iota((n, LANES), 0), _iota((n, LANES), 1)), 1.0, 0.0).astype(BF16)
    x = _dot2_l(p, mov)
    z = x + pltpu.roll(x, nr - 1, 0) + pltpu.roll(x, nr - 2, 0) + pltpu.roll(x, nr - 3, 0)
    z0 = jnp.where(_mod(_iota(z.shape, 0), NSA_GH) == 0, z, 0.0)
    p_slc = z0 + pltpu.roll(z0, 1, 0) + pltpu.roll(z0, 2, 0) + pltpu.roll(z0, 3, 0)
    jb = _iota((nr, LANES), 1)
    forced = (jb == 0) | (jb == n_selp - 1)
    score = jnp.where(forced, FORCE_SCORE, jnp.where(jb < n_selp, p_slc, -1.0))
    k_past = min(N_SELECT, n_selp + 1) - 1
    sel = _topk_mask(score, k_past)
    mb_ref[0] = jnp.where((sel > 0.5) & (score >= 0.0), 0.0, NEG).astype(BF16)
    qf = _dot(qq, _fold_heads()).astype(BF16)
    kwt = kwt_ref[...].reshape(LANES, wb).astype(BF16)
    vwt = vwt_ref[...].reshape(LANES, wb).astype(BF16)
    new_sc[...] = jnp.zeros_like(new_sc)
    new_sc[0:tq, :] = wn_ref[0]
    wnew = new_sc[...]
    dw = _div(_iota((nr, wb), 0), 8) + wb - _iota((nr, wb), 1)
    okw = (dw >= 0) & (dw < WINDOW)
    sw = _dot(qf, kwt) + _bias_lookup(_bucket(jnp.clip(dw, 0, MAX_DISTANCE - 1)), _rb_col(rbt_ref))
    sw = jnp.where(okw, sw, NEG)
    dn = _div(_iota((nr, PAGE), 0), 8) - _iota((nr, PAGE), 1)
    okn = (dn >= 0) & (_iota((nr, PAGE), 1) < tq)
    sn = _dot_nt(qf, wnew[:, 0:LANES].astype(BF16))
    sn = sn + _bias_lookup(_bucket(jnp.clip(dn, 0, MAX_DISTANCE - 1)), _rb_col(rbt_ref))
    sn = jnp.where(okn, sn, NEG)
    mw = jnp.maximum(jnp.max(sw, axis=1, keepdims=True), jnp.max(sn, axis=1, keepdims=True))
    pw = jnp.where(okw, jnp.exp(sw - mw), 0.0)
    pn = jnp.where(okn, jnp.exp(sn - mw), 0.0)
    lw = jnp.maximum(jnp.sum(pw, axis=1, keepdims=True) + jnp.sum(pn, axis=1, keepdims=True), TINY)
    a = (_dot_nt(pw.astype(BF16), vwt) + _dot(pn.astype(BF16), wnew[:, LANES:2 * LANES].astype(BF16))) / lw
    _diag_rows(ow_ref, _dot2_l(a, _place_wide()))


def _ns_attend(qn, kcw, win_t, wkv_new, rbt, layer, past):
    nb, tq, _ = qn.shape
    npad = kcw.shape[2]
    wb = win_t.shape[5]
    seq = lambda b: (b, 0, 0)

    def win_spec(c):
        return pl.BlockSpec((None, None, None, NSA_KV, HD, wb), lambda b: (layer, b, c, 0, 0, 0))

    return pl.pallas_call(
        functools.partial(_ns_attend_kernel, past=past),
        grid=(nb,),
        in_specs=[pl.BlockSpec((1, tq, NSA_W), seq),
                  pl.BlockSpec((1, 2, npad, NSA_W), lambda b: (b, 0, 0, 0)),
                  win_spec(0), win_spec(1),
                  pl.BlockSpec((1, tq, 2 * LANES), seq),
                  pl.BlockSpec((tq * 8, N_BUCKETS), lambda b: (0, 0))],
        out_specs=[pl.BlockSpec((1, tq, NSA_W), seq), pl.BlockSpec((1, tq, NSA_W), seq),
                   pl.BlockSpec((1, tq * 8, LANES), seq)],
        out_shape=[jax.ShapeDtypeStruct((nb, tq, NSA_W), F32), jax.ShapeDtypeStruct((nb, tq, NSA_W), F32),
                   jax.ShapeDtypeStruct((nb, tq * 8, LANES), BF16)],
        scratch_shapes=[pltpu.VMEM((PAGE, 2 * LANES), F32)],
        compiler_params=_cparams(("arbitrary",)),
        name="ns_attend",
    )(qn, kcw, win_t, win_t, wkv_new, rbt)


def _ns_select_kernel(pt_tab, *refs, past):
    P = PAGES_PER_STEP
    q_ref, mb_ref, nn_ref, rbt_ref = refs[0:4]
    k_refs = refs[4:4 + P]
    v_refs = refs[4 + P:4 + 2 * P]
    o_ref, q_sc, m_ref, l_ref, acc_ref, new_sc = refs[4 + 2 * P:]
    s = pl.program_id(1)
    tq = q_ref.shape[1]
    nr = tq * 8

    @pl.when(s == 0)
    def _():
        q_sc[...] = _dot(_rows_th(q_ref[0]), _fold_heads()).astype(BF16)
        _softmax_init(m_ref, l_ref, acc_ref)
        new_sc[...] = jnp.zeros_like(new_sc)
        new_sc[0:tq, :] = nn_ref[0][:, 2 * LANES:]

    qq = q_sc[...]
    trow = _div(_iota((nr, PAGE), 0), 8)
    key = _iota((nr, PAGE), 1)

    def attend(score_fns, value_fns, biases):
        sc = jnp.concatenate([fn() for fn in score_fns], axis=1) + biases
        m_old = m_ref[...]
        m_new = jnp.maximum(m_old, jnp.max(sc, axis=1, keepdims=True))
        alpha = jnp.exp(m_old - m_new)
        p = jnp.exp(sc - m_new)
        l_ref[...] = alpha * l_ref[...] + jnp.sum(p, axis=1, keepdims=True)
        acc = alpha * acc_ref[...]
        for k, fn in enumerate(value_fns):
            acc = acc + fn(p[:, k * PAGE:(k + 1) * PAGE].astype(BF16))
        acc_ref[...] = acc
        m_ref[...] = m_new

    blk = _iota((LANES, P * PAGE), 0)
    kcol = _iota((LANES, P * PAGE), 1)
    expand = jnp.where(blk == _div(s * (P * PAGE) + kcol, SEL_LEN), 1.0, 0.0).astype(BF16)
    bias = _dot(mb_ref[0], expand) + rbt_ref[:, N_BUCKETS - 1:N_BUCKETS]
    tiles = ([lambda r=r: _dot(qq, r[...].reshape(LANES, PAGE).astype(BF16)) for r in k_refs],
             [lambda p, r=r: _dot_nt(p, r[...].reshape(LANES, PAGE).astype(BF16)) for r in v_refs])

    @pl.when(s < pl.num_programs(1) - 1)
    def _():
        attend(*tiles, bias)

    @pl.when(s == pl.num_programs(1) - 1)
    def _():
        d_last = past + trow - (past - PAGE + key)
        b_last = _bias_lookup(_bucket(jnp.clip(d_last, 0, MAX_DISTANCE - 1)), _rb_col(rbt_ref))
        fix = jnp.concatenate([jnp.zeros((nr, (P - 1) * PAGE), F32),
                               b_last - rbt_ref[:, N_BUCKETS - 1:N_BUCKETS]], axis=1)
        attend(*tiles, bias + fix)
        d_new = trow - key
        b_new = _bias_lookup(_bucket(jnp.clip(d_new, 0, MAX_DISTANCE - 1)), _rb_col(rbt_ref))
        new = new_sc[...]
        attend([lambda: _dot_nt(qq, new[:, 0:LANES].astype(BF16))],
               [lambda p: _dot(p, new[:, LANES:2 * LANES].astype(BF16))],
               jnp.where((d_new >= 0) & (key < tq), b_new, NEG))
        a = acc_ref[...] / jnp.maximum(l_ref[...], TINY)
        place = _place_wide()
        _diag_rows(o_ref, _dot2_l(a, place))


def _ns_select(qn, maskb, nkv_new, rbt, cache_nsa, page_table, layer, past):
    nb, npages = page_table.shape
    tq = qn.shape[1]
    P = PAGES_PER_STEP
    seq = lambda b, s, pt: (b, 0, 0)

    def page_spec(k, c):
        return pl.BlockSpec((None, None, None, NSA_KV, HD, PAGE),
                            lambda b, s, pt: (layer, pt[b, s * P + k], c, 0, 0, 0))

    return pl.pallas_call(
        functools.partial(_ns_select_kernel, past=past),
        grid_spec=pltpu.PrefetchScalarGridSpec(
            num_scalar_prefetch=1,
            grid=(nb, npages // P),
            in_specs=[pl.BlockSpec((1, tq, NSA_W), seq), pl.BlockSpec((1, tq * 8, LANES), seq),
                      pl.BlockSpec((1, tq, 4 * LANES), seq),
                      pl.BlockSpec((tq * 8, N_BUCKETS), lambda b, s, pt: (0, 0))]
            + [page_spec(k, 2) for k in range(P)] + [page_spec(k, 3) for k in range(P)],
            out_specs=pl.BlockSpec((1, tq, NSA_W), seq),
            scratch_shapes=[pltpu.VMEM((tq * 8, LANES), BF16),
                            pltpu.VMEM((tq * 8, 1), F32), pltpu.VMEM((tq * 8, 1), F32),
                            pltpu.VMEM((tq * 8, LANES), F32),
                            pltpu.VMEM((PAGE, 2 * LANES), F32)]),
        out_shape=jax.ShapeDtypeStruct((nb, tq, NSA_W), F32),
        compiler_params=_cparams(("arbitrary", "arbitrary")),
        name="ns_select",
    )(page_table, qn, maskb, nkv_new, rbt, *([cache_nsa] * (2 * P)))


def _prep_w_in(w):
    d = w.shape[0]
    o_logf = 3 * FOX_W
    o_qn = o_logf + H_FOX
    o_kv = o_qn + NSA_W
    o_gate = o_kv + 6 * NSA_KV * HD
    misc = jnp.concatenate([w[:, o_logf:o_qn], w[:, o_gate:o_gate + 3 * H_NSA],
                            jnp.zeros((d, LANES - H_FOX - 3 * H_NSA), w.dtype)], axis=1)
    return jnp.concatenate([w[:, :o_logf], w[:, o_qn:o_kv], w[:, o_kv:o_gate], misc], axis=1).astype(BF16)


def _prep_cmp(pos, w1, b1, w2):
    eye = jnp.eye(NSA_KV, dtype=w1.dtype)
    w1r = w1.reshape(2, 2, CMP_STRIDE, HD, CMP_HID)
    wide = jnp.einsum("whjdc,ab->whjadbc", w1r, eye).reshape(2, 2, CMP_STRIDE * LANES, 2 * CMP_HID)
    posr = jnp.broadcast_to(pos.reshape(2, 2, CMP_STRIDE, 1, HD), (2, 2, CMP_STRIDE, NSA_KV, HD))
    posr = posr.reshape(2, 2, 1, CMP_STRIDE * LANES)
    group_of_head = (jnp.arange(H_NSA) // NSA_GH)[None, :] == jnp.arange(NSA_KV)[:, None]
    w2w = jnp.einsum("wcd,ah->wachd", w2, group_of_head.astype(w2.dtype)).reshape(2, 2 * CMP_HID, NSA_W)
    return {"pt": posr[:, 0], "pb": posr[:, 1],
            "w1t": wide[:, 0].astype(BF16), "w1b": wide[:, 1].astype(BF16),
            "b1": jnp.concatenate([b1, b1], axis=-1)[:, None, :],
            "w2w": w2w.astype(BF16)}


def _far_bias_rows(rel_bias):
    far = rel_bias[N_BUCKETS - 1]
    hi = far.astype(BF16).astype(F32)
    r = far - hi
    mid = r.astype(BF16).astype(F32)
    lo = (r - mid).astype(BF16).astype(F32)
    rows = jnp.zeros((H_NSA, LANES), F32)
    rows = rows.at[:, HD].set(hi).at[:, HD + 1].set(mid).at[:, HD + 2].set(lo)
    return rows.reshape(NSA_KV, NSA_GH, LANES)


def kernel(x_prompt, x_sample, cache_fox_kv, cache_fox_logf, cache_nsa_kv, state_win_kv, state_conv,
           page_table, norm1_g, w_in, b_forget, cmp_pos, cmp_w1, cmp_b1, cmp_w2, out_norm_g, w_out,
           norm2_g, w_gu, conv_w, conv_b, w_down, rel_bias, final_norm_g):
    B, S, D = x_prompt.shape
    nb, tq, _ = x_sample.shape
    depth = w_in.shape[0]
    n_pool = cache_fox_kv.shape[1]
    npages = page_table.shape[1]
    past = npages * PAGE
    dff = w_down.shape[1]
    wb = state_win_kv.shape[2]
    assert tq & (tq - 1) == 0 and tq >= CONV_W - 1
    assert S % 256 == 0 and S // SEL_LEN <= LANES and past // SEL_LEN <= LANES
    assert npages % PAGES_PER_STEP == 0 and wb == WINDOW and past >= WINDOW and tq * 8 <= LANES

    assert npages % FOX_PAGES == 0 and PAGE == LANES
    cache_kt = jnp.transpose(cache_fox_kv, (0, 1, 3, 4, 5, 2))
    cache_lft = jnp.swapaxes(cache_fox_logf, 2, 3)
    cache_nsa = jnp.transpose(cache_nsa_kv, (0, 1, 3, 4, 5, 2))
    win_t = jnp.transpose(state_win_kv, (0, 1, 3, 4, 5, 2))
    rbt = jnp.tile(rel_bias.T, (tq, 1))
    brow = _far_bias_rows(rel_bias * LOG2E)
    gf = final_norm_g.reshape(1, D)

    xp = x_prompt.reshape(B * S, D)
    xs = x_sample.reshape(nb * tq, D)
    outs = [[] for _ in range(10)]
    for l in range(depth):
        w_l = _prep_w_in(w_in[l])
        bf = jnp.zeros((1, LANES), F32).at[0, :H_FOX].set(b_forget[l])
        cw = _prep_cmp(cmp_pos[l], cmp_w1[l], cmp_b1[l], cmp_w2[l])
        g1 = norm1_g[l].reshape(1, D)
        g2 = norm2_g[l].reshape(1, D)
        go = out_norm_g[l].reshape(1, D)
        wo = w_out[l].astype(BF16)
        wgu = w_gu[l].astype(BF16)
        wd = w_down[l].astype(BF16)
        cb = conv_b[l].reshape(1, dff)
        final = l == depth - 1

        qf, fkv, qn, nkv, wkv, misc = _proj(xp, g1, w_l, bf)
        r3 = lambda a: a.reshape(B, S, a.shape[-1])
        qa, ka, vb = _fox_prep(r3(qf), r3(fkv), r3(misc))
        vt = jnp.swapaxes(vb, 1, 2).reshape(B, H_FOX, HD, S)
        vt = jnp.concatenate([vt, jnp.ones((B, H_FOX, ONES_ROWS, S), BF16)], axis=2)
        o_fox = _fox_flash(qa, ka, vt)
        kcw = _compress_prompt(r3(nkv), cw)
        ksa, kwa = _nsa_prep(r3(nkv), r3(wkv))
        vst = r3(nkv)[:, :, 3 * LANES:].astype(BF16).reshape(B, S, NSA_KV, HD).transpose(0, 2, 3, 1)
        vst = jnp.concatenate([vst, jnp.ones((B, NSA_KV, ONES_ROWS, S), BF16)], axis=2)
        o_c, maskq = _nsa_cmp(rel_bias, r3(qn), kcw)
        o_s = _nsa_sel(rel_bias, r3(qn), maskq, brow, ksa, vst)
        vwt = r3(wkv)[:, :, LANES:].astype(BF16).reshape(B, S, NSA_KV, HD).transpose(0, 2, 3, 1)
        vwt = jnp.pad(vwt, ((0, 0), (0, 0), (0, 0), (WINDOW, 0)))
        vwt = jnp.concatenate([vwt, jnp.ones((B, NSA_KV, ONES_ROWS, S + WINDOW), BF16)], axis=2)
        o_w = _nsa_win(rel_bias, r3(qn), jnp.pad(kwa, ((0, 0), (0, 0), (WINDOW, 0), (0, 0))), vwt)
        f2 = lambda a: a.reshape(B * S, a.shape[-1])
        xp = _merge(xp, f2(o_fox), f2(o_c), f2(o_s), f2(o_w), misc, go, wo)
        xp, gtail = _ffn(xp, g2, wgu, conv_w[l], cb, wd, gf, seq_len=S, final=final)
        tiles_per_seq = gtail.shape[0] // B
        conv_p = gtail.reshape(B, tiles_per_seq, 8, dff)[:, -1, 8 - (CONV_W - 1):, :]
        outs[0].append(fkv.reshape(B, S, 2, H_FOX, HD))
        outs[2].append(misc[:, :H_FOX].reshape(B, S, H_FOX))
        outs[4].append(nkv.reshape(B, S, 4, NSA_KV, HD))
        outs[6].append(wkv.reshape(B, S, 2, NSA_KV, HD)[:, -min(WINDOW, S):])
        outs[8].append(conv_p)

        qf, fkv, qn, nkv, wkv, misc = _proj(xs, g1, w_l, bf)
        s3 = lambda a: a.reshape(nb, tq, a.shape[-1])
        lfn = jnp.swapaxes(s3(misc)[:, :, :H_FOX], 1, 2)
        lfn = jnp.pad(lfn, ((0, 0), (0, 0), (0, PAGE - tq)))
        o_fox = _fox_sample(s3(qf), s3(fkv), lfn, cache_kt, cache_lft, page_table, l)
        kcw = _compress_sample(cache_nsa, page_table, l, cw)
        o_c, o_w, maskb = _ns_attend(s3(qn), kcw, win_t, s3(wkv), rbt, l, past)
        o_s = _ns_select(s3(qn), maskb, s3(nkv), rbt, cache_nsa, page_table, l, past)
        s2 = lambda a: a.reshape(nb * tq, a.shape[-1])
        xs = _merge(xs, s2(o_fox), s2(o_c), s2(o_s), s2(o_w), misc, go, wo)
        hist = state_conv[l]
        zero = jnp.zeros((nb, 1, dff), F32)
        hm1 = jnp.concatenate([hist[:, 1:2], zero, zero, zero][:tq], axis=1).reshape(nb * tq, dff)
        hm2 = jnp.concatenate([hist[:, 0:1], hist[:, 1:2], zero, zero][:tq], axis=1).reshape(nb * tq, dff)
        xs, gfull = _ffn(xs, g2, wgu, conv_w[l], cb, wd, gf, seq_len=None, final=final, hist=(hm1, hm2),
                         short_len=tq)
        win_new = jnp.concatenate([state_win_kv[l], wkv.reshape(nb, tq, 2, NSA_KV, HD)], axis=1)[:, -wb:]
        outs[1].append(fkv.reshape(nb, tq, 2, H_FOX, HD))
        outs[3].append(misc[:, :H_FOX].reshape(nb, tq, H_FOX))
        outs[5].append(nkv.reshape(nb, tq, 4, NSA_KV, HD))
        outs[7].append(win_new)
        outs[9].append(gfull.reshape(nb, tq, dff)[:, -(CONV_W - 1):])

    st = [jnp.stack(o) for o in outs]
    return (xp.reshape(B, S, D), xs.reshape(nb, tq, D),
            st[0], st[1], st[2], st[3], st[4], st[5], st[6], st[7], st[8], st[9])
```
